```python
import math
import jax
import jax.numpy as jnp
from jax import lax
import numpy as np

D_MODEL = 1024
BATCH = 4
SEQ = 4096
DEPTH = 4
DEC_BATCH = 128
DEC_SEQ = 1
PAST_LEN = 2048
PAGE_SIZE = 128

HEAD_DIM = 64
D_PLE = 256
A_HEADS = 8
A_BRANCHES = ((128, 1), (512, 4), (2048, 16))
A_WIN_MAX = max(w for w, _ in A_BRANCHES)
B_HEADS = 4
B_DK = 64
B_DV = 128
B_GATE_RANK = 16
B_GATE_TAU = 16.0
B_CHUNK = 64
C_HEADS = 16
C_KV_HEADS = 4
C_IDX_HEADS = 8
C_IDX_DIM = 64
C_TOPK_MAX = 256
D_FF = ((8 * D_MODEL + 3 * 256 - 1) // (3 * 256)) * 256
N_BUCKETS = 32
REL_MAX_DIST = A_WIN_MAX
BIAS_HEADS = max(A_HEADS, C_HEADS)
Q_BLOCK = 128
NORM_EPS = 1e-6
N_AB = (DEPTH + 1) // 2
N_C = DEPTH // 2
AB_SPLITS = (A_HEADS * HEAD_DIM, A_HEADS * HEAD_DIM, A_HEADS * HEAD_DIM,
             B_HEADS * B_DK, B_HEADS * B_DK, B_HEADS * B_DV, B_GATE_RANK, B_HEADS * B_DV)
C_SPLITS = (C_HEADS * HEAD_DIM, C_KV_HEADS * HEAD_DIM, C_KV_HEADS * HEAD_DIM,
            C_IDX_HEADS * C_IDX_DIM, C_IDX_DIM, C_IDX_HEADS)

kernel_name = 'hybrid_dilated_gla_dsa_decoder_step'


def rmsnorm(x, g):
    xf = x.astype(jnp.float32)
    y = xf * lax.rsqrt(jnp.mean(xf * xf, axis=-1, keepdims=True) + NORM_EPS)
    return (y * g.astype(jnp.float32)).astype(x.dtype)


def split_cols(y, sizes):
    bounds = [int(b) for b in np.cumsum(sizes)[:-1]]
    return jnp.split(y, bounds, axis=-1)


def rel_bucket(dist):
    n = jnp.maximum(dist, 0)
    exact = N_BUCKETS // 2
    nf = jnp.maximum(n, exact).astype(jnp.float32)
    large = exact + (jnp.log(nf / exact) / math.log(REL_MAX_DIST / exact)
                     * (N_BUCKETS - exact)).astype(jnp.int32)
    return jnp.where(n < exact, n, jnp.minimum(large, N_BUCKETS - 1))


def rel_bias_for(rel_bias, dist, n_heads):
    return rel_bias[rel_bucket(dist)][..., :n_heads].astype(jnp.float32)


def take_rows(a, idx):
    return jax.vmap(lambda a_b, i_b: a_b[i_b])(a, idx)


def gathered_attention(q, kg, vg, bias, valid):
    bx, t, h, d = q.shape
    k, n_kv = kg.shape[2], kg.shape[3]
    grp = h // n_kv
    qg = q.reshape(bx, t, n_kv, grp, d)
    s = jnp.einsum('btngd,btknd->btkng', qg, kg).reshape(bx, t, k, h).astype(jnp.float32) + bias
    s = jnp.where(valid[..., None], s, -jnp.inf)
    lse = jax.nn.logsumexp(s, axis=2)
    p = jnp.exp(s - lse[:, :, None, :]).reshape(bx, t, k, n_kv, grp)
    o = jnp.einsum('btkng,btknd->btngd', p.astype(vg.dtype), vg).reshape(bx, t, h, d)
    return o, lse


def combine_branches(outs, lses):
    wts = jax.nn.softmax(jnp.stack(lses, 0), axis=0)
    return jnp.einsum('nbth,nbthd->bthd', wts.astype(outs[0].dtype), jnp.stack(outs, 0))


def branch_offsets(w, d):
    return jnp.arange(w // d + 1, dtype=jnp.int32) * d


def to_blocks(a):
    b, s = a.shape[:2]
    return a.reshape(b, s // Q_BLOCK, Q_BLOCK, *a.shape[2:]).swapaxes(0, 1)


def from_blocks(a):
    nb, b = a.shape[:2]
    return a.swapaxes(0, 1).reshape(b, nb * Q_BLOCK, *a.shape[3:])


def dilated_prompt(q, k, v, rel_bias):
    s_len, h = q.shape[1], q.shape[2]
    starts = jnp.arange(s_len // Q_BLOCK, dtype=jnp.int32) * Q_BLOCK

    def block(args):
        q_blk, t0 = args
        t = t0 + jnp.arange(Q_BLOCK, dtype=jnp.int32)
        outs, lses = [], []
        for w, d in A_BRANCHES:
            offs = branch_offsets(w, d)
            idx = t[:, None] - offs[None, :]
            idc = jnp.maximum(idx, 0)
            o, lse = gathered_attention(q_blk, jnp.take(k, idc, axis=1), jnp.take(v, idc, axis=1),
                                        rel_bias_for(rel_bias, offs, h), (idx >= 0)[None])
            outs.append(o)
            lses.append(lse)
        return combine_branches(outs, lses)

    return from_blocks(lax.map(block, (to_blocks(q), starts)))


def dilated_sample(q, k_new, v_new, k_buf, v_buf, rel_bias):
    t_len, h = q.shape[1], q.shape[2]
    wb = k_buf.shape[1]
    j = jnp.arange(t_len, dtype=jnp.int32)
    outs, lses = [], []
    for w, d in A_BRANCHES:
        offs = branch_offsets(w, d)
        idx = wb + j[:, None] - offs[None, :]
        in_buf = (idx < wb)[None, :, :, None, None]
        ib = jnp.clip(idx, 0, wb - 1)
        inew = jnp.clip(idx - wb, 0, t_len - 1)
        kg = jnp.where(in_buf, jnp.take(k_buf, ib, axis=1), jnp.take(k_new, inew, axis=1))
        vg = jnp.where(in_buf, jnp.take(v_buf, ib, axis=1), jnp.take(v_new, inew, axis=1))
        o, lse = gathered_attention(q, kg, vg, rel_bias_for(rel_bias, offs, h), (idx >= 0)[None])
        outs.append(o)
        lses.append(lse)
    return combine_branches(outs, lses)


def gla_chunked(q, k, v, g):
    b, s_len, h, dk = q.shape
    dv = v.shape[-1]
    nc = s_len // B_CHUNK
    causal = jnp.tril(jnp.ones((B_CHUNK, B_CHUNK), dtype=bool))

    def to_chunks(a):
        return a.reshape(b, nc, B_CHUNK, h, a.shape[-1]).transpose(1, 0, 3, 2, 4)

    def step(st, inp):
        qc, kc, vc, gc = inp
        cum = jnp.cumsum(gc.astype(jnp.float32), axis=2)
        o_inter = jnp.einsum('bhcd,bhde->bhce', qc * jnp.exp(cum).astype(qc.dtype), st)
        diff = cum[:, :, :, None, :] - cum[:, :, None, :, :]
        decay = jnp.exp(jnp.where(causal[None, None, :, :, None], diff, -jnp.inf))
        att = jnp.einsum('bhtd,bhsd,bhtsd->bhts', qc, kc, decay)
        o = o_inter + jnp.einsum('bhts,bhse->bhte', att, vc)
        last = cum[:, :, -1:, :]
        st_new = (jnp.exp(last[:, :, 0, :])[..., None] * st
                  + jnp.einsum('bhsd,bhse->bhde', kc * jnp.exp(last - cum), vc))
        return st_new, o

    s0 = jnp.zeros((b, h, dk, dv), jnp.float32)
    s_fin, o = lax.scan(step, s0, (to_chunks(q), to_chunks(k), to_chunks(v), to_chunks(g)))
    o = o.transpose(1, 0, 3, 2, 4).reshape(b, s_len, h, dv)
    return o.astype(v.dtype), s_fin


def gla_recurrent(q, k, v, g, s0):
    def step(st, inp):
        qt, kt, vt, gt = inp
        st = jnp.exp(gt.astype(jnp.float32))[..., None] * st + jnp.einsum('bhd,bhe->bhde', kt, vt)
        return st, jnp.einsum('bhd,bhde->bhe', qt, st)

    xs = tuple(a.transpose(1, 0, 2, 3) for a in (q, k, v, g))
    s_fin, o = lax.scan(step, s0.astype(jnp.float32), xs)
    return o.transpose(1, 0, 2, 3).astype(v.dtype), s_fin


def index_scores(iq, iw, ik):
    s = jnp.einsum('bthd,bsd->btsh', iq, ik).astype(jnp.float32)
    return jnp.einsum('btsh,bth->bts', jax.nn.relu(s), iw.astype(jnp.float32))


def dsa_prompt(q, k, v, iq, ik, iw, rel_bias):
    s_len, h = q.shape[1], q.shape[2]
    topk = min(C_TOPK_MAX, s_len // 4)
    key_pos = jnp.arange(s_len, dtype=jnp.int32)
    starts = jnp.arange(s_len // Q_BLOCK, dtype=jnp.int32) * Q_BLOCK

    def block(args):
        q_blk, iq_blk, iw_blk, t0 = args
        t = t0 + jnp.arange(Q_BLOCK, dtype=jnp.int32)
        score = index_scores(iq_blk, iw_blk, ik)
        score = jnp.where((key_pos[None, :] <= t[:, None])[None], score, -jnp.inf)
        _, idx = lax.top_k(score, topk)
        dist = t[None, :, None] - idx
        o, _ = gathered_attention(q_blk, take_rows(k, idx), take_rows(v, idx),
                                  rel_bias_for(rel_bias, dist, h), dist >= 0)
        return o

    return from_blocks(lax.map(block, (to_blocks(q), to_blocks(iq), to_blocks(iw), starts)))


def dsa_sample(q, k_new, v_new, iq, ik_new, iw, pool_k, pool_v, pool_ik, page_table, rel_bias):
    bx, t_len, h = q.shape[0], q.shape[1], q.shape[2]
    past = page_table.shape[1] * PAGE_SIZE
    n_keys = past + t_len
    topk = min(C_TOPK_MAX, n_keys // 4)
    ik_all = jnp.concatenate([pool_ik[page_table].reshape(bx, past, C_IDX_DIM), ik_new], axis=1)
    t = past + jnp.arange(t_len, dtype=jnp.int32)
    score = index_scores(iq, iw, ik_all)
    score = jnp.where((jnp.arange(n_keys, dtype=jnp.int32)[None, :] <= t[:, None])[None], score, -jnp.inf)
    _, idx = lax.top_k(score, topk)
    pidx = jnp.minimum(idx, past - 1)
    phys = take_rows(page_table, pidx // PAGE_SIZE)
    off = pidx % PAGE_SIZE
    nidx = jnp.clip(idx - past, 0, t_len - 1)
    in_past = (idx < past)[..., None, None]
    kg = jnp.where(in_past, pool_k[phys, off], take_rows(k_new, nidx))
    vg = jnp.where(in_past, pool_v[phys, off], take_rows(v_new, nidx))
    dist = t[None, :, None] - idx
    o, _ = gathered_attention(q, kg, vg, rel_bias_for(rel_bias, dist, h), dist >= 0)
    return o


def ab_inputs(hn, li, prm):
    lead = hn.shape[:-1]
    y = hn @ prm['w_in_ab'][li]
    aq, ak, av, bq, bk, bv, blr, bog = split_cols(y, AB_SPLITS)
    aq = rmsnorm(aq.reshape(*lead, A_HEADS, HEAD_DIM), prm['a_q_norm'][li]) * (HEAD_DIM ** -0.5)
    ak = rmsnorm(ak.reshape(*lead, A_HEADS, HEAD_DIM), prm['a_k_norm'][li])
    av = av.reshape(*lead, A_HEADS, HEAD_DIM)
    bq = bq.reshape(*lead, B_HEADS, B_DK) * (B_DK ** -0.5)
    bk = bk.reshape(*lead, B_HEADS, B_DK)
    bv = bv.reshape(*lead, B_HEADS, B_DV)
    z = (blr @ prm['b_gate_w2'][li] + prm['b_gate_b'][li]).astype(jnp.float32)
    bg = (jax.nn.log_sigmoid(z) / B_GATE_TAU).reshape(*lead, B_HEADS, B_DK)
    return aq, ak, av, bq, bk, bv, bg, bog


def ab_output(oa, ob, bog, li, prm):
    lead = oa.shape[:-2]
    gate = jax.nn.silu(bog).reshape(*lead, B_HEADS, B_DV)
    ob = rmsnorm(ob, prm['b_out_norm'][li]) * gate
    o = jnp.concatenate([oa.reshape(*lead, A_HEADS * HEAD_DIM), ob.reshape(*lead, B_HEADS * B_DV)], axis=-1)
    return o @ prm['w_out_ab'][li]


def c_inputs(hn, li, prm):
    lead = hn.shape[:-1]
    y = hn @ prm['w_in_c'][li]
    cq, ck, cv, iq, ik, iw = split_cols(y, C_SPLITS)
    cq = rmsnorm(cq.reshape(*lead, C_HEADS, HEAD_DIM), prm['c_q_norm'][li]) * (HEAD_DIM ** -0.5)
    ck = rmsnorm(ck.reshape(*lead, C_KV_HEADS, HEAD_DIM), prm['c_k_norm'][li])
    cv = cv.reshape(*lead, C_KV_HEADS, HEAD_DIM)
    iq = iq.reshape(*lead, C_IDX_HEADS, C_IDX_DIM)
    ik = rmsnorm(ik, prm['c_ik_norm'][li])
    iw = iw * ((C_IDX_HEADS ** -0.5) * (C_IDX_DIM ** -0.5))
    return cq, ck, cv, iq, ik, iw


def c_output(oc, li, prm):
    lead = oc.shape[:-2]
    return oc.reshape(*lead, C_HEADS * HEAD_DIM) @ prm['w_out_c'][li]


def ffn_ple(x, p_l, l, prm):
    hn = rmsnorm(x, prm['g_ffn'][l])
    x = x + (jax.nn.silu(hn @ prm['w_ff1'][l]) * (hn @ prm['w_ff3'][l])) @ prm['w_ff2'][l]
    u = rmsnorm(x, prm['g_ple'][l])
    return x + jax.nn.sigmoid(u @ prm['w_ple_gate'][l]) * (p_l @ prm['w_ple_proj'][l])


def prompt_forward(x, p, prm):
    s_len = x.shape[1]
    wp = min(A_WIN_MAX, s_len)
    a_k, a_v, b_s, c_k, c_v, c_ik = [], [], [], [], [], []
    for l in range(DEPTH):
        hn = rmsnorm(x, prm['g_mix'][l])
        li = l // 2
        if l % 2 == 0:
            aq, ak, av, bq, bk, bv, bg, bog = ab_inputs(hn, li, prm)
            oa = dilated_prompt(aq, ak, av, prm['rel_bias'])
            ob, s_fin = gla_chunked(bq, bk, bv, bg)
            x = x + ab_output(oa, ob, bog, li, prm)
            a_k.append(ak[:, s_len - wp:])
            a_v.append(av[:, s_len - wp:])
            b_s.append(s_fin)
        else:
            cq, ck, cv, iq, ik, iw = c_inputs(hn, li, prm)
            oc = dsa_prompt(cq, ck, cv, iq, ik, iw, prm['rel_bias'])
            x = x + c_output(oc, li, prm)
            c_k.append(ck)
            c_v.append(cv)
            c_ik.append(ik)
        x = ffn_ple(x, p[l], l, prm)
    return x, jnp.stack(a_k), jnp.stack(a_v), jnp.stack(b_s), jnp.stack(c_k), jnp.stack(c_v), jnp.stack(c_ik)


def sample_forward(x, p, cache_a_k, cache_a_v, state_b, cache_c_k, cache_c_v, cache_c_ik, page_table, prm):
    a_k, a_v, b_s, c_k, c_v, c_ik = [], [], [], [], [], []
    for l in range(DEPTH):
        hn = rmsnorm(x, prm['g_mix'][l])
        li = l // 2
        if l % 2 == 0:
            aq, ak, av, bq, bk, bv, bg, bog = ab_inputs(hn, li, prm)
            oa = dilated_sample(aq, ak, av, cache_a_k[li], cache_a_v[li], prm['rel_bias'])
            ob, s_fin = gla_recurrent(bq, bk, bv, bg, state_b[li])
            x = x + ab_output(oa, ob, bog, li, prm)
            a_k.append(ak)
            a_v.append(av)
            b_s.append(s_fin)
        else:
            cq, ck, cv, iq, ik, iw = c_inputs(hn, li, prm)
            oc = dsa_sample(cq, ck, cv, iq, ik, iw, cache_c_k[li], cache_c_v[li], cache_c_ik[li],
                            page_table, prm['rel_bias'])
            x = x + c_output(oc, li, prm)
            c_k.append(ck)
            c_v.append(cv)
            c_ik.append(ik)
        x = ffn_ple(x, p[l], l, prm)
    return x, jnp.stack(a_k), jnp.stack(a_v), jnp.stack(b_s), jnp.stack(c_k), jnp.stack(c_v), jnp.stack(c_ik)


def setup_inputs(seed: int = 0) -> dict:
    key = jax.random.key(seed)
    ks = jax.random.split(key, 32)

    def nrm(i, shape, scale=1.0):
        return jax.random.normal(ks[i], shape, jnp.float32) * scale

    def gain(i, shape):
        return 1.0 + nrm(i, shape, 0.05)

    n_pages = PAST_LEN // PAGE_SIZE
    n_used = DEC_BATCH * n_pages
    n_pool = n_used + n_used // 4
    wb = min(A_WIN_MAX, PAST_LEN)
    page_table = jax.random.permutation(ks[8], n_pool)[:n_used].reshape(DEC_BATCH, n_pages).astype(jnp.int32)
    d_ab_in = sum(AB_SPLITS)
    d_c_in = sum(C_SPLITS)
    d_ab_out = A_HEADS * HEAD_DIM + B_HEADS * B_DV
    d_c_out = C_HEADS * HEAD_DIM
    return {
        'x_prompt': nrm(0, (BATCH, SEQ, D_MODEL)),
        'x_sample': nrm(1, (DEC_BATCH, DEC_SEQ, D_MODEL)),
        'cache_a_k': nrm(2, (N_AB, DEC_BATCH, wb, A_HEADS, HEAD_DIM)),
        'cache_a_v': nrm(3, (N_AB, DEC_BATCH, wb, A_HEADS, HEAD_DIM)),
        'state_b': nrm(4, (N_AB, DEC_BATCH, B_HEADS, B_DK, B_DV)),
        'cache_c_k': nrm(5, (N_C, n_pool, PAGE_SIZE, C_KV_HEADS, HEAD_DIM)),
        'cache_c_v': nrm(6, (N_C, n_pool, PAGE_SIZE, C_KV_HEADS, HEAD_DIM)),
        'cache_c_ik': nrm(7, (N_C, n_pool, PAGE_SIZE, C_IDX_DIM)),
        'page_table': page_table,
        'p_prompt': nrm(9, (DEPTH, BATCH, SEQ, D_PLE)),
        'p_sample': nrm(10, (DEPTH, DEC_BATCH, DEC_SEQ, D_PLE)),
        'rel_bias': nrm(11, (N_BUCKETS, BIAS_HEADS), 0.5),
        'g_mix': gain(12, (DEPTH, D_MODEL)),
        'w_in_ab': nrm(13, (N_AB, D_MODEL, d_ab_in), D_MODEL ** -0.5),
        'a_q_norm': gain(14, (N_AB, HEAD_DIM)),
        'a_k_norm': gain(15, (N_AB, HEAD_DIM)),
        'b_gate_w2': nrm(16, (N_AB, B_GATE_RANK, B_HEADS * B_DK), B_GATE_RANK ** -0.5),
        'b_gate_b': nrm(17, (N_AB, B_HEADS * B_DK), 0.1),
        'b_out_norm': gain(18, (N_AB, B_DV)),
        'w_out_ab': nrm(19, (N_AB, d_ab_out, D_MODEL), d_ab_out ** -0.5),
        'w_in_c': nrm(20, (N_C, D_MODEL, d_c_in), D_MODEL ** -0.5),
        'c_q_norm': gain(21, (N_C, HEAD_DIM)),
        'c_k_norm': gain(22, (N_C, HEAD_DIM)),
        'c_ik_norm': gain(23, (N_C, C_IDX_DIM)),
        'w_out_c': nrm(24, (N_C, d_c_out, D_MODEL), d_c_out ** -0.5),
        'g_ffn': gain(25, (DEPTH, D_MODEL)),
        'w_ff1': nrm(26, (DEPTH, D_MODEL, D_FF), D_MODEL ** -0.5),
        'w_ff3': nrm(27, (DEPTH, D_MODEL, D_FF), D_MODEL ** -0.5),
        'w_ff2': nrm(28, (DEPTH, D_FF, D_MODEL), D_FF ** -0.5),
        'g_ple': gain(29, (DEPTH, D_MODEL)),
        'w_ple_gate': nrm(30, (DEPTH, D_MODEL, D_MODEL), D_MODEL ** -0.5),
        'w_ple_proj': nrm(31, (DEPTH, D_PLE, D_MODEL), D_PLE ** -0.5),
    }


def reference(x_prompt, x_sample, cache_a_k, cache_a_v, state_b, cache_c_k, cache_c_v, cache_c_ik,
              page_table, p_prompt, p_sample, rel_bias, g_mix, w_in_ab, a_q_norm, a_k_norm,
              b_gate_w2, b_gate_b, b_out_norm, w_out_ab, w_in_c, c_q_norm, c_k_norm, c_ik_norm,
              w_out_c, g_ffn, w_ff1, w_ff3, w_ff2, g_ple, w_ple_gate, w_ple_proj):
    prm = dict(rel_bias=rel_bias, g_mix=g_mix, w_in_ab=w_in_ab, a_q_norm=a_q_norm, a_k_norm=a_k_norm,
               b_gate_w2=b_gate_w2, b_gate_b=b_gate_b, b_out_norm=b_out_norm, w_out_ab=w_out_ab,
               w_in_c=w_in_c, c_q_norm=c_q_norm, c_k_norm=c_k_norm, c_ik_norm=c_ik_norm,
               w_out_c=w_out_c, g_ffn=g_ffn, w_ff1=w_ff1, w_ff3=w_ff3, w_ff2=w_ff2,
               g_ple=g_ple, w_ple_gate=w_ple_gate, w_ple_proj=w_ple_proj)
    y_prompt, pa_k, pa_v, pb_s, pc_k, pc_v, pc_ik = prompt_forward(x_prompt, p_prompt, prm)
    y_sample, sa_k, sa_v, sb_s, sc_k, sc_v, sc_ik = sample_forward(
        x_sample, p_sample, cache_a_k, cache_a_v, state_b, cache_c_k, cache_c_v, cache_c_ik, page_table, prm)
    return (y_prompt, y_sample, pa_k, pa_v, pb_s, pc_k, pc_v, pc_ik, sa_k, sa_v, sb_s, sc_k, sc_v, sc_ik)
```

```python
import functools
import math

import jax
import jax.numpy as jnp
import numpy as np
from jax import lax
from jax.experimental import pallas as pl
from jax.experimental.pallas import tpu as pltpu

F32 = jnp.float32
BF16 = jnp.bfloat16

LANES = 128
HEAD_DIM = 64
A_HEADS = 8
A_BRANCHES = ((128, 1), (512, 4), (2048, 16))
A_WIN_MAX = 2048
B_HEADS = 4
B_DK = 64
B_DV = 128
B_GATE_RANK = 16
B_GATE_TAU = 16.0
C_HEADS = 16
C_KV_HEADS = 4
C_IDX_HEADS = 8
C_IDX_DIM = 64
C_TOPK_MAX = 256
N_BUCKETS = 32
REL_MAX_DIST = A_WIN_MAX
Q_BLOCK = 128
PAGE_SIZE = 128
NORM_EPS = 1e-6
NEG = -1e30
VMEM_LIMIT = 56 * 1024 * 1024


def _params(*sem):
    return pltpu.CompilerParams(dimension_semantics=sem, vmem_limit_bytes=VMEM_LIMIT)


def _rms(x, g):
    ms = jnp.mean(x * x, axis=-1, keepdims=True)
    return x * lax.rsqrt(ms + NORM_EPS) * g


def _dot(a, b):
    return jnp.dot(a, b, preferred_element_type=F32)


def _dot_nt(a, b):
    return lax.dot_general(a, b, (((1,), (1,)), ((), ())), preferred_element_type=F32)


def _dot_tn(a, b):
    return lax.dot_general(a, b, (((0,), (0,)), ((), ())), preferred_element_type=F32)


PROJ_CHUNK = 256


def _proj_kernel(x_ref, g_ref, w_ref, cs_ref, p_ref, o_ref, *, head_norm):
    hn = _rms(x_ref[...], g_ref[...]).astype(BF16)
    for c in range(o_ref.shape[1] // PROJ_CHUNK):
        sl = slice(c * PROJ_CHUNK, (c + 1) * PROJ_CHUNK)
        y = _dot(hn, w_ref[:, sl])
        if head_norm:
            ms = _dot((y * y).astype(BF16), p_ref[...])
            y = y * lax.rsqrt(ms + NORM_EPS)
        o_ref[:, sl] = y * cs_ref[:, sl]


def _group_mean_matrix():
    r = np.arange(PROJ_CHUNK) // HEAD_DIM
    return jnp.asarray((r[:, None] == r[None, :]).astype(np.float32) / HEAD_DIM, BF16)


def norm_proj(x, g, w, colscale, *, head_norm, tm):
    n, d = x.shape
    dout = w.shape[1]
    assert n % tm == 0 and dout % PROJ_CHUNK == 0
    return pl.pallas_call(
        functools.partial(_proj_kernel, head_norm=head_norm),
        grid=(n // tm,),
        in_specs=[
            pl.BlockSpec((tm, d), lambda i: (i, 0)),
            pl.BlockSpec((1, d), lambda i: (0, 0)),
            pl.BlockSpec((d, dout), lambda i: (0, 0)),
            pl.BlockSpec((1, dout), lambda i: (0, 0)),
            pl.BlockSpec((PROJ_CHUNK, PROJ_CHUNK), lambda i: (0, 0)),
        ],
        out_specs=pl.BlockSpec((tm, dout), lambda i: (i, 0)),
        out_shape=jax.ShapeDtypeStruct((n, dout), F32),
        compiler_params=_params("parallel"),
        name="norm_proj_hn" if head_norm else "norm_proj",
    )(x, g.reshape(1, d), w, colscale.reshape(1, dout), _group_mean_matrix())


def _out_kernel(*refs, n_in):
    res_ref, o_ref = refs[2 * n_in], refs[2 * n_in + 1]
    acc = res_ref[...]
    for a_ref, w_ref in zip(refs[:n_in], refs[n_in:2 * n_in]):
        acc = acc + _dot(a_ref[...].astype(BF16), w_ref[...])
    o_ref[...] = acc


def out_proj(a_list, w_list, res, *, tm):
    n, d = res.shape
    n_in = len(a_list)
    in_specs = [pl.BlockSpec((tm, a.shape[1]), lambda i: (i, 0)) for a in a_list]
    in_specs += [pl.BlockSpec(w.shape, lambda i: (0, 0)) for w in w_list]
    in_specs += [pl.BlockSpec((tm, d), lambda i: (i, 0))]
    return pl.pallas_call(
        functools.partial(_out_kernel, n_in=n_in),
        grid=(n // tm,),
        in_specs=in_specs,
        out_specs=pl.BlockSpec((tm, d), lambda i: (i, 0)),
        out_shape=jax.ShapeDtypeStruct((n, d), F32),
        compiler_params=_params("parallel"),
        name="out_proj",
    )(*a_list, *w_list, res)


def _ffn_kernel(x_ref, gf_ref, w1_ref, w3_ref, w2_ref, gp_ref, wg_ref, p_ref, wp_ref, o_ref,
                hn_ref, acc_ref):
    f = pl.program_id(1)

    @pl.when(f == 0)
    def _():
        hn_ref[...] = _rms(x_ref[...], gf_ref[...]).astype(BF16)
        acc_ref[...] = jnp.zeros_like(acc_ref)

    hn = hn_ref[...]
    h1 = _dot(hn, w1_ref[...])
    h3 = _dot(hn, w3_ref[...])
    a = h1 * jax.nn.sigmoid(h1) * h3
    acc_ref[...] += _dot(a.astype(BF16), w2_ref[...])

    @pl.when(f == pl.num_programs(1) - 1)
    def _():
        x2 = x_ref[...] + acc_ref[...]
        u = _rms(x2, gp_ref[...]).astype(BF16)
        gate = jax.nn.sigmoid(_dot(u, wg_ref[...]))
        o_ref[...] = x2 + gate * _dot(p_ref[...].astype(BF16), wp_ref[...])


def ffn_ple(x, p, gf, w1, w3, w2, gp, wg, wp, *, tm, tf):
    n, d = x.shape
    dff = w1.shape[1]
    dple = p.shape[1]
    assert n % tm == 0 and dff % tf == 0
    return pl.pallas_call(
        _ffn_kernel,
        grid=(n // tm, dff // tf),
        in_specs=[
            pl.BlockSpec((tm, d), lambda i, f: (i, 0)),
            pl.BlockSpec((1, d), lambda i, f: (0, 0)),
            pl.BlockSpec((d, tf), lambda i, f: (0, f)),
            pl.BlockSpec((d, tf), lambda i, f: (0, f)),
            pl.BlockSpec((tf, d), lambda i, f: (f, 0)),
            pl.BlockSpec((1, d), lambda i, f: (0, 0)),
            pl.BlockSpec((d, d), lambda i, f: (0, 0)),
            pl.BlockSpec((tm, dple), lambda i, f: (i, 0)),
            pl.BlockSpec((dple, d), lambda i, f: (0, 0)),
        ],
        out_specs=pl.BlockSpec((tm, d), lambda i, f: (i, 0)),
        out_shape=jax.ShapeDtypeStruct((n, d), F32),
        scratch_shapes=[pltpu.VMEM((tm, d), BF16), pltpu.VMEM((tm, d), F32)],
        compiler_params=_params("parallel", "arbitrary"),
        name="ffn_ple",
    )(x, gf.reshape(1, d), w1, w3, w2, gp.reshape(1, d), wg, p, wp)


def _rel_bucket(dist):
    n = jnp.maximum(dist, 0)
    exact = N_BUCKETS // 2
    nf = jnp.maximum(n, exact).astype(F32)
    large = exact + (jnp.log(nf / exact) / math.log(REL_MAX_DIST / exact)
                     * (N_BUCKETS - exact)).astype(jnp.int32)
    return jnp.where(n < exact, n, jnp.minimum(large, N_BUCKETS - 1))


def _branch_multiplicity(delta):
    mult = jnp.zeros(delta.shape, jnp.int32)
    for w, d in A_BRANCHES:
        mult = mult + ((delta >= 0) & (delta <= w) & (delta % d == 0)).astype(jnp.int32)
    return mult


DIL_WIN_CHUNKS = A_WIN_MAX // Q_BLOCK + 1


def dilated_bias_table(rel_bias):
    c = jnp.arange(DIL_WIN_CHUNKS)[:, None, None]
    i = jnp.arange(Q_BLOCK)[None, :, None]
    j = jnp.arange(Q_BLOCK)[None, None, :]
    delta = (DIL_WIN_CHUNKS - 1 - c) * Q_BLOCK + i - j
    mult = _branch_multiplicity(delta)
    bias = rel_bias[_rel_bucket(delta)][..., :A_HEADS].astype(F32)
    t = jnp.where(mult[..., None] > 0, bias + jnp.log(jnp.maximum(mult, 1).astype(F32))[..., None], NEG)
    t = t.transpose(3, 0, 1, 2).reshape(A_HEADS // 2, 2, DIL_WIN_CHUNKS, Q_BLOCK, Q_BLOCK)
    return t.transpose(0, 2, 1, 3, 4).reshape(A_HEADS // 2, DIL_WIN_CHUNKS, 2 * Q_BLOCK, Q_BLOCK)


DSA_BIAS_TILES = 14


def dsa_bias_table(rel_bias):
    e = jnp.arange(DSA_BIAS_TILES)[:, None, None]
    i = jnp.arange(Q_BLOCK)[None, :, None]
    j = jnp.arange(Q_BLOCK)[None, None, :]
    bias = rel_bias[_rel_bucket(e * Q_BLOCK + i - j)][..., :C_HEADS].astype(F32)
    grp = C_HEADS // C_KV_HEADS
    t = bias.transpose(3, 0, 1, 2).reshape(C_KV_HEADS, grp, DSA_BIAS_TILES, Q_BLOCK, Q_BLOCK)
    return t.transpose(0, 2, 1, 3, 4).reshape(C_KV_HEADS, DSA_BIAS_TILES, grp * Q_BLOCK, Q_BLOCK)


def _lane_lt64():
    return lax.broadcasted_iota(jnp.int32, (1, LANES), 1) < HEAD_DIM


def _online_softmax_step(s, v_bf16, m, l, acc):
    m_new = jnp.maximum(m, jnp.max(s, axis=1, keepdims=True))
    alpha = jnp.exp(m - m_new)
    p = jnp.exp(s - m_new)
    l = alpha * l + jnp.sum(p, axis=1, keepdims=True)
    acc = alpha * acc + _dot(p.astype(BF16), v_bf16)
    return m_new, l, acc


def _dil_kernel(q_ref, k_ref, v_ref, t_ref, o_ref):
    i = pl.program_id(2)
    lo = _lane_lt64()
    q = q_ref[...]
    q2 = jnp.concatenate([jnp.where(lo, q, 0.0), jnp.where(lo, 0.0, q)], axis=0).astype(BF16)
    last = DIL_WIN_CHUNKS - 1

    def body(it, carry):
        r0 = pl.multiple_of((i - it) * Q_BLOCK, Q_BLOCK)
        kc = k_ref[pl.ds(r0, Q_BLOCK), :].astype(BF16)
        vc = v_ref[pl.ds(r0, Q_BLOCK), :].astype(BF16)
        s = _dot_nt(q2, kc) + t_ref[0, last - it]
        return _online_softmax_step(s, vc, *carry)

    init = (jnp.full((2 * Q_BLOCK, 1), NEG, F32), jnp.zeros((2 * Q_BLOCK, 1), F32),
            jnp.zeros((2 * Q_BLOCK, LANES), F32))
    _, l, acc = lax.fori_loop(0, jnp.minimum(i, last) + 1, body, init)
    o = acc / l
    o_ref[...] = jnp.where(lo, o[:Q_BLOCK], o[Q_BLOCK:])


def dilated_prompt(qk, yv, table, batch, seq):
    n = batch * seq
    nblk = seq // Q_BLOCK
    npair = A_HEADS // 2
    return pl.pallas_call(
        _dil_kernel,
        grid=(batch, npair, nblk),
        in_specs=[
            pl.BlockSpec((Q_BLOCK, LANES), lambda b, p, i: (b * nblk + i, p)),
            pl.BlockSpec((seq, LANES), lambda b, p, i: (b, npair + p)),
            pl.BlockSpec((seq, LANES), lambda b, p, i: (b, p)),
            pl.BlockSpec((1,) + table.shape[1:], lambda b, p, i: (p, 0, 0, 0)),
        ],
        out_specs=pl.BlockSpec((Q_BLOCK, LANES), lambda b, p, i: (b * nblk + i, p)),
        out_shape=jax.ShapeDtypeStruct((n, A_HEADS * HEAD_DIM), F32),
        compiler_params=_params("parallel", "parallel", "arbitrary"),
        name="dilated_prompt",
    )(qk, qk, yv, table)


GLA_CHUNK = 64
GLA_SUB = 16
GLA_EXP_CLAMP = 60.0


def _log_sigmoid(z):
    return jnp.minimum(z, 0.0) - jnp.log1p(jnp.exp(-jnp.abs(z)))


def _gla_kernel(bq_ref, bk_ref, bv_ref, bog_ref, blr_ref, w2_ref, gb_ref, on_ref, o_ref, s_ref, st_ref):
    it = pl.program_id(1)

    @pl.when(it == 0)
    def _():
        st_ref[...] = jnp.zeros_like(st_ref)

    lo = _lane_lt64()
    ch = GLA_CHUNK
    row = lax.broadcasted_iota(jnp.int32, (ch, ch), 0)
    col = lax.broadcasted_iota(jnp.int32, (ch, ch), 1)
    causal = row >= col
    ltri = jnp.where(causal, 1.0, 0.0).astype(BF16)
    top_half = lax.broadcasted_iota(jnp.int32, (LANES, LANES), 0) < HEAD_DIM

    def chunk_body(c, carry):
        rows = pl.ds(pl.multiple_of(c * ch, ch), ch)
        z = _dot(blr_ref[rows, :].astype(BF16), w2_ref[...]) + gb_ref[...]
        gg = _log_sigmoid(z) / B_GATE_TAU
        for p in range(B_HEADS // 2):
            sl = slice(p * LANES, (p + 1) * LANES)
            q = bq_ref[rows, sl]
            k = bk_ref[rows, sl]
            g = gg[:, sl]
            g_hi = g.astype(BF16)
            g_lo = (g - g_hi.astype(F32)).astype(BF16)
            cum = _dot(ltri, g_hi) + _dot(ltri, g_lo)
            last = cum[ch - 1:ch, :]
            st = st_ref[p]
            qd = q * jnp.exp(cum)
            q2 = jnp.concatenate([jnp.where(lo, qd, 0.0), jnp.where(lo, 0.0, qd)], axis=0)
            o_inter = _dot(q2.astype(BF16), st.astype(BF16))
            atts = []
            for sb in range(ch // GLA_SUB):
                rs = slice(sb * GLA_SUB, (sb + 1) * GLA_SUB)
                ref_row = cum[sb * GLA_SUB:sb * GLA_SUB + 1, :]
                qs = q[rs] * jnp.exp(cum[rs] - ref_row)
                ks = k * jnp.exp(jnp.minimum(ref_row - cum, GLA_EXP_CLAMP))
                qq = jnp.concatenate([jnp.where(lo, qs, 0.0), jnp.where(lo, 0.0, qs)], axis=0)
                atts.append(_dot_nt(qq.astype(BF16), ks.astype(BF16)))
            kd = (k * jnp.exp(last - cum)).astype(BF16)
            upd = []
            for e in range(2):
                hs = slice((2 * p + e) * LANES, (2 * p + e + 1) * LANES)
                v = bv_ref[rows, hs].astype(BF16)
                att = jnp.concatenate([a[e * GLA_SUB:(e + 1) * GLA_SUB] for a in atts], axis=0)
                att = jnp.where(causal, att, 0.0)
                o = o_inter[e * ch:(e + 1) * ch] + _dot(att.astype(BF16), v)
                og = _rms(o, on_ref[...])
                gate = bog_ref[rows, hs]
                o_ref[rows, hs] = og * (gate * jax.nn.sigmoid(gate))
                upd.append(_dot_tn(kd, v))
            decay = jnp.transpose(jnp.broadcast_to(jnp.exp(last), (LANES, LANES)))
            st_ref[p] = decay * st + jnp.where(top_half, upd[0], upd[1])
        return carry

    lax.fori_loop(0, bq_ref.shape[0] // ch, chunk_body, 0)

    @pl.when(it == pl.num_programs(1) - 1)
    def _():
        for p in range(B_HEADS // 2):
            s_ref[0, 2 * p] = st_ref[p, :HEAD_DIM, :]
            s_ref[0, 2 * p + 1] = st_ref[p, HEAD_DIM:, :]


def gla_prompt(y, w2p, gate_b, out_norm, batch, seq, *, tg):
    n = batch * seq
    nt = seq // tg
    dv = B_HEADS * B_DV
    return pl.pallas_call(
        _gla_kernel,
        grid=(batch, nt),
        in_specs=[
            pl.BlockSpec((tg, 256), lambda b, t: (b * nt + t, 2)),
            pl.BlockSpec((tg, 256), lambda b, t: (b * nt + t, 3)),
            pl.BlockSpec((tg, dv), lambda b, t: (b * nt + t, 2)),
            pl.BlockSpec((tg, dv), lambda b, t: (b * nt + t, 3)),
            pl.BlockSpec((tg, LANES), lambda b, t: (b * nt + t, 16)),
            pl.BlockSpec((LANES, 256), lambda b, t: (0, 0)),
            pl.BlockSpec((1, 256), lambda b, t: (0, 0)),
            pl.BlockSpec((1, B_DV), lambda b, t: (0, 0)),
        ],
        out_specs=[
            pl.BlockSpec((tg, dv), lambda b, t: (b * nt + t, 0)),
            pl.BlockSpec((1, B_HEADS, B_DK, B_DV), lambda b, t: (b, 0, 0, 0)),
        ],
        out_shape=[jax.ShapeDtypeStruct((n, dv), F32),
                   jax.ShapeDtypeStruct((batch, B_HEADS, B_DK, B_DV), F32)],
        scratch_shapes=[pltpu.VMEM((B_HEADS // 2, LANES, LANES), F32)],
        compiler_params=_params("parallel", "arbitrary"),
        name="gla_prompt",
    )(y, y, y, y, y, w2p, gate_b.reshape(1, 256), out_norm.reshape(1, B_DV))


SEL_CHUNK = 256
INT_MIN = -2 ** 31


def _sortable_key(score):
    bits = pltpu.bitcast(score + 0.0, jnp.int32)
    return jnp.where(bits < 0, bits ^ jnp.int32(0x7FFFFFFF), bits)


def _threshold_select(key_ref, n_ch, topk, idx_bits, row_limit, write):
    r = key_ref.shape[0]
    kc = SEL_CHUNK
    lane = lax.broadcasted_iota(jnp.int32, (1, kc), 1)

    def count(hits):
        def body(c, acc):
            c0 = pl.multiple_of(c * kc, kc)
            hit = hits(key_ref[:, pl.ds(c0, kc)], c0 + lane)
            for t in range(kc // LANES):
                acc = acc + hit[:, t * LANES:(t + 1) * LANES]
            return acc
        acc = lax.fori_loop(0, n_ch, body, jnp.zeros((r, LANES), jnp.int32))
        return jnp.sum(acc, axis=1, keepdims=True)

    base = jnp.where(count(lambda k, _: jnp.where(k >= 0, 1, 0)) >= topk, 0, INT_MIN).astype(jnp.int32)

    def bit_body(b, base):
        cand = base | jnp.left_shift(jnp.int32(1), 30 - b)
        return jnp.where(count(lambda k, _: jnp.where(k >= cand, 1, 0)) >= topk, cand, base)

    tau = lax.fori_loop(0, 31, bit_body, base)
    need = topk - count(lambda k, _: jnp.where(k > tau, 1, 0))

    def idx_body(b, lo):
        cand = lo | jnp.left_shift(jnp.int32(1), idx_bits - 1 - b)
        cnt = count(lambda k, col: jnp.where(k == tau, jnp.where(col < cand, 1, 0), 0))
        return jnp.where(cnt < need, cand, lo)

    last_eq = lax.fori_loop(0, idx_bits, idx_body, jnp.zeros((r, 1), jnp.int32))

    def out_body(c, carry):
        c0 = pl.multiple_of(c * kc, kc)
        k = key_ref[:, pl.ds(c0, kc)]
        col = c0 + lane
        val = jnp.where(k > tau, 0.0, jnp.where(k == tau, jnp.where(col <= last_eq, 0.0, NEG), NEG))
        write(c0, jnp.where(col <= row_limit, val, NEG))
        return carry

    lax.fori_loop(0, n_ch, out_body, 0)


def _select_kernel(iq_ref, iw_ref, ik_ref, o_ref, key_ref, *, topk, idx_bits):
    blk = pl.program_id(1)
    kc = SEL_CHUNK
    n_ch = (blk * Q_BLOCK + Q_BLOCK + kc - 1) // kc
    t_idx = blk * Q_BLOCK + lax.broadcasted_iota(jnp.int32, (Q_BLOCK, 1), 0)
    lane = lax.broadcasted_iota(jnp.int32, (1, kc), 1)
    iw = iw_ref[...]

    def score_body(c, carry):
        c0 = pl.multiple_of(c * kc, kc)
        ik = ik_ref[pl.ds(c0, kc), :].astype(BF16)
        sc = jnp.zeros((Q_BLOCK, kc), F32)
        for h in range(C_IDX_HEADS):
            qh = iq_ref[:, h * LANES:(h + 1) * LANES].astype(BF16)
            sc = sc + jnp.maximum(_dot_nt(qh, ik), 0.0) * iw[:, h:h + 1]
        key_ref[:, pl.ds(c0, kc)] = jnp.where(c0 + lane <= t_idx, _sortable_key(sc), INT_MIN)
        return carry

    lax.fori_loop(0, n_ch, score_body, 0)
    o_ref[...] = jnp.full(o_ref.shape, NEG, o_ref.dtype)

    def write(c0, val):
        o_ref[:, pl.ds(c0, kc)] = val.astype(o_ref.dtype)

    _threshold_select(key_ref, n_ch, topk, idx_bits, t_idx, write)


def dsa_select_prompt(y2, y1, batch, seq):
    n = batch * seq
    assert seq % SEL_CHUNK == 0
    nblk = seq // Q_BLOCK
    topk = min(C_TOPK_MAX, seq // 4)
    return pl.pallas_call(
        functools.partial(_select_kernel, topk=topk, idx_bits=max(1, (seq - 1).bit_length())),
        grid=(batch, nblk),
        in_specs=[
            pl.BlockSpec((Q_BLOCK, C_IDX_HEADS * LANES), lambda b, i: (b * nblk + i, 0)),
            pl.BlockSpec((Q_BLOCK, LANES), lambda b, i: (b * nblk + i, 12)),
            pl.BlockSpec((seq, LANES), lambda b, i: (b, 12)),
        ],
        out_specs=pl.BlockSpec((Q_BLOCK, seq), lambda b, i: (b * nblk + i, 0)),
        out_shape=jax.ShapeDtypeStruct((n, seq), BF16),
        scratch_shapes=[pltpu.VMEM((Q_BLOCK, seq), jnp.int32)],
        compiler_params=_params("parallel", "arbitrary"),
        name="dsa_select_prompt",
    )(y2, y2, y1)


def _stack_heads(q):
    lo = _lane_lt64()
    parts = []
    for j in range(q.shape[1] // LANES):
        qp = q[:, j * LANES:(j + 1) * LANES]
        parts += [jnp.where(lo, qp, 0.0), jnp.where(lo, 0.0, qp)]
    return jnp.concatenate(parts, axis=0)


def _unstack_heads(o, rows):
    lo = _lane_lt64()
    return [jnp.where(lo, o[(2 * j) * rows:(2 * j + 1) * rows], o[(2 * j + 1) * rows:(2 * j + 2) * rows])
            for j in range(o.shape[0] // (2 * rows))]


def _dsa_attn_kernel(q_ref, k_ref, v_ref, mask_ref, tb_ref, o_ref):
    blk = pl.program_id(2)
    kc = SEL_CHUNK
    grp = C_HEADS // C_KV_HEADS
    q4 = _stack_heads(q_ref[...]).astype(BF16)

    def body(c, carry):
        c0 = pl.multiple_of(c * kc, kc)
        k = k_ref[pl.ds(c0, kc), :].astype(BF16)
        v = v_ref[pl.ds(c0, kc), :].astype(BF16)
        s = _dot_nt(q4, k)
        mk = mask_ref[:, pl.ds(c0, kc)].astype(F32)
        halves = []
        for hf in range(kc // Q_BLOCK):
            e = jnp.clip(blk - (c * (kc // Q_BLOCK) + hf), 0, DSA_BIAS_TILES - 1)
            cols = slice(hf * Q_BLOCK, (hf + 1) * Q_BLOCK)
            halves.append(s[:, cols] + tb_ref[0, e] + jnp.concatenate([mk[:, cols]] * grp, axis=0))
        return _online_softmax_step(jnp.concatenate(halves, axis=1), v, *carry)

    rows = grp * Q_BLOCK
    init = (jnp.full((rows, 1), NEG, F32), jnp.zeros((rows, 1), F32), jnp.zeros((rows, LANES), F32))
    n_ch = (blk * Q_BLOCK + Q_BLOCK + kc - 1) // kc
    _, l, acc = lax.fori_loop(0, n_ch, body, init)
    outs = _unstack_heads(acc / l, Q_BLOCK)
    for j, o in enumerate(outs):
        o_ref[:, j * LANES:(j + 1) * LANES] = o


def dsa_attn_prompt(y1, y2, mask, table, batch, seq):
    n = batch * seq
    nblk = seq // Q_BLOCK
    qw = (C_HEADS // C_KV_HEADS) * HEAD_DIM
    return pl.pallas_call(
        _dsa_attn_kernel,
        grid=(batch, C_KV_HEADS, nblk),
        in_specs=[
            pl.BlockSpec((Q_BLOCK, qw), lambda b, g, i: (b * nblk + i, g)),
            pl.BlockSpec((seq, LANES), lambda b, g, i: (b, 8 + g)),
            pl.BlockSpec((seq, LANES), lambda b, g, i: (b, 8 + g)),
            pl.BlockSpec((Q_BLOCK, seq), lambda b, g, i: (b * nblk + i, 0)),
            pl.BlockSpec((1,) + table.shape[1:], lambda b, g, i: (g, 0, 0, 0)),
        ],
        out_specs=pl.BlockSpec((Q_BLOCK, qw), lambda b, g, i: (b * nblk + i, g)),
        out_shape=jax.ShapeDtypeStruct((n, C_HEADS * HEAD_DIM), F32),
        compiler_params=_params("parallel", "parallel", "arbitrary"),
        name="dsa_attn_prompt",
    )(y1, y1, y2, mask, table)


IW_SCALE = (C_IDX_HEADS ** -0.5) * (C_IDX_DIM ** -0.5)
Q_SCALE = HEAD_DIM ** -0.5
AB_Y_COLS = 2304
C_Y_COLS = 1792


def _dup_heads(w, n_heads):
    d = w.shape[0]
    w = w.reshape(d, n_heads, 1, HEAD_DIM)
    return jnp.broadcast_to(w, (d, n_heads, 2, HEAD_DIM)).reshape(d, n_heads * LANES)


def _pad_heads(w, n_heads):
    d = w.shape[0]
    w = w.reshape(d, n_heads, HEAD_DIM)
    return jnp.pad(w, ((0, 0), (0, 0), (0, HEAD_DIM))).reshape(d, n_heads * LANES)


def _pad_cols(w, total):
    return jnp.pad(w, ((0, 0), (0, total - w.shape[1])))


def prep_ab(w_in, a_q_norm, a_k_norm, w2, w_out):
    hd = A_HEADS * HEAD_DIM
    w1 = w_in[:, :2 * hd]
    s1 = jnp.concatenate([jnp.tile(a_q_norm, A_HEADS) * Q_SCALE, jnp.tile(a_k_norm, A_HEADS)])
    nbq = B_HEADS * B_DK
    blr0 = 3 * hd + 2 * nbq + B_HEADS * B_DV
    wy = jnp.concatenate([w_in[:, 2 * hd:blr0], w_in[:, blr0 + B_GATE_RANK:], w_in[:, blr0:blr0 + B_GATE_RANK]],
                         axis=1)
    wy = _pad_cols(wy, AB_Y_COLS)
    sy = jnp.ones((AB_Y_COLS,), F32).at[hd:hd + nbq].set(Q_SCALE)
    w2p = jnp.pad(w2, ((0, LANES - B_GATE_RANK), (0, 0)))
    return dict(w1=w1.astype(BF16), s1=s1, wy=wy.astype(BF16), sy=sy, w2p=w2p.astype(BF16),
                wo_a=w_out[:hd].astype(BF16), wo_b=w_out[hd:].astype(BF16))


def prep_c(w_in, c_q_norm, c_k_norm, c_ik_norm, w_out):
    o = np.cumsum((0, C_HEADS * HEAD_DIM, C_KV_HEADS * HEAD_DIM, C_KV_HEADS * HEAD_DIM,
                   C_IDX_HEADS * C_IDX_DIM, C_IDX_DIM, C_IDX_HEADS))
    cq, ck, cv, iq, ik, iw = (w_in[:, o[t]:o[t + 1]] for t in range(6))
    w1 = _pad_cols(jnp.concatenate([cq, _dup_heads(ck, C_KV_HEADS), ik], axis=1), C_Y_COLS)
    s1 = jnp.concatenate([jnp.tile(c_q_norm, C_HEADS) * Q_SCALE, jnp.tile(c_k_norm, 2 * C_KV_HEADS), c_ik_norm])
    s1 = jnp.pad(s1, (0, C_Y_COLS - s1.shape[0]))
    w2 = _pad_cols(jnp.concatenate([_pad_heads(iq, C_IDX_HEADS), _dup_heads(cv, C_KV_HEADS), iw], axis=1), C_Y_COLS)
    iw0 = C_IDX_HEADS * LANES + C_KV_HEADS * LANES
    s2 = jnp.ones((C_Y_COLS,), F32).at[iw0:iw0 + C_IDX_HEADS].set(IW_SCALE)
    return dict(w1=w1.astype(BF16), s1=s1, w2=w2.astype(BF16), s2=s2, wo=w_out.astype(BF16))


def _first_half(a, n_heads):
    return a.reshape(a.shape[0], n_heads, LANES)[:, :, :HEAD_DIM]


def prompt_forward(x, p, w, tables):
    batch, seq, d = x.shape
    n = batch * seq
    tm = min(512, n)
    wp = min(A_WIN_MAX, seq)
    hd = A_HEADS * HEAD_DIM
    xf = x.reshape(n, d)
    a_k, a_v, b_s, c_k, c_v, c_ik = [], [], [], [], [], []
    for l in range(len(w["ffn"])):
        li = l // 2
        if l % 2 == 0:
            ab = w["ab"][li]
            qk = norm_proj(xf, w["g_mix"][l], ab["w1"], ab["s1"], head_norm=True, tm=tm)
            y = norm_proj(xf, w["g_mix"][l], ab["wy"], ab["sy"], head_norm=False, tm=tm)
            oa = dilated_prompt(qk, y, tables["dil"], batch, seq)
            ob, s_fin = gla_prompt(y, ab["w2p"], w["b_gate_b"][li], w["b_out_norm"][li], batch, seq,
                                   tg=min(256, seq))
            xf = out_proj([oa, ob], [ab["wo_a"], ab["wo_b"]], xf, tm=tm)
            a_k.append(qk[:, hd:].reshape(batch, seq, A_HEADS, HEAD_DIM)[:, seq - wp:])
            a_v.append(y[:, :hd].reshape(batch, seq, A_HEADS, HEAD_DIM)[:, seq - wp:])
            b_s.append(s_fin)
        else:
            c = w["c"][li]
            y1 = norm_proj(xf, w["g_mix"][l], c["w1"], c["s1"], head_norm=True, tm=tm)
            y2 = norm_proj(xf, w["g_mix"][l], c["w2"], c["s2"], head_norm=False, tm=tm)
            mask = dsa_select_prompt(y2, y1, batch, seq)
            oc = dsa_attn_prompt(y1, y2, mask, tables["dsa"], batch, seq)
            xf = out_proj([oc], [c["wo"]], xf, tm=tm)
            k0 = C_HEADS * HEAD_DIM
            c_k.append(_first_half(y1[:, k0:k0 + C_KV_HEADS * LANES], C_KV_HEADS).reshape(batch, seq, C_KV_HEADS, HEAD_DIM))
            c_v.append(_first_half(y2[:, k0:k0 + C_KV_HEADS * LANES], C_KV_HEADS).reshape(batch, seq, C_KV_HEADS, HEAD_DIM))
            ik0 = k0 + C_KV_HEADS * LANES
            c_ik.append(y1[:, ik0:ik0 + C_IDX_DIM].reshape(batch, seq, C_IDX_DIM))
        f = w["ffn"][l]
        xf = ffn_ple(xf, p[l].reshape(n, -1), f["gf"], f["w1"], f["w3"], f["w2"], f["gp"], f["wg"], f["wp"],
                     tm=tm, tf=256)
    return (xf.reshape(batch, seq, d), jnp.stack(a_k), jnp.stack(a_v), jnp.stack(b_s),
            jnp.stack(c_k), jnp.stack(c_v), jnp.stack(c_ik))


def dilated_sample_bias(rel_bias, wb):
    rows = jnp.arange(Q_BLOCK)
    deltas = []
    for w, d in A_BRANCHES:
        assert w // d == Q_BLOCK and wb >= w
        deltas.append(w - d * rows)
    delta = jnp.stack(deltas)
    bias = rel_bias[_rel_bucket(delta)][..., :A_HEADS].astype(F32)[..., None]
    b0 = rel_bias[_rel_bucket(jnp.zeros((), jnp.int32))][:A_HEADS].astype(F32)
    return bias, (b0 + math.log(len(A_BRANCHES)))[:, None]


def _dil_sample_kernel(qk_ref, vn_ref, k1, k2, k3, v1, v2, v3, bias_ref, b0_ref, o_ref):
    q = qk_ref[:A_HEADS, :]
    k_new = qk_ref[A_HEADS:, :]
    s_new = jnp.sum(q * k_new, axis=-1, keepdims=True) + b0_ref[...]
    scores = [jnp.sum(k[...] * q[None], axis=-1, keepdims=True) + bias_ref[t]
              for t, k in enumerate((k1, k2, k3))]
    m = s_new
    for s in scores:
        m = jnp.maximum(m, jnp.max(s, axis=0))
    p_new = jnp.exp(s_new - m)
    l = p_new
    acc = p_new * vn_ref[...]
    for s, v in zip(scores, (v1, v2, v3)):
        p = jnp.exp(s - m[None])
        l = l + jnp.sum(p, axis=0)
        acc = acc + jnp.sum(p * v[...], axis=0)
    o_ref[...] = acc / l


def dilated_sample(qk3, v_new, cache_k, cache_v, li, bias, b0):
    bx = qk3.shape[0]
    n_ab, _, wb, nh, hd = cache_k.shape
    views, specs = [], []
    for w, d in A_BRANCHES:
        rows_total = wb // d
        blk_idx = rows_total // Q_BLOCK - 1
        shape6 = (n_ab, bx, rows_total // Q_BLOCK, Q_BLOCK, d, nh, hd)
        views.append(shape6)
        specs.append(pl.BlockSpec((None, None, None, Q_BLOCK, None, nh, hd),
                                  functools.partial(lambda b, bi: (li, b, bi, 0, 0, 0, 0), bi=blk_idx)))
    cache_specs = specs + specs
    cache_args = [cache_k.reshape(s) for s in views] + [cache_v.reshape(s) for s in views]
    return pl.pallas_call(
        _dil_sample_kernel,
        grid=(bx,),
        in_specs=[pl.BlockSpec((None, 2 * nh, hd), lambda b: (b, 0, 0)),
                  pl.BlockSpec((None, nh, hd), lambda b: (b, 0, 0))] + cache_specs + [
                  pl.BlockSpec(bias.shape, lambda b: (0, 0, 0, 0)),
                  pl.BlockSpec(b0.shape, lambda b: (0, 0))],
        out_specs=pl.BlockSpec((None, nh, hd), lambda b: (b, 0, 0)),
        out_shape=jax.ShapeDtypeStruct((bx, nh, hd), F32),
        compiler_params=_params("parallel"),
        name="dilated_sample",
    )(qk3, v_new, *cache_args, bias, b0)


def _gla_gate_kernel(blr_ref, w2_ref, gb_ref, o_ref):
    z = _dot(blr_ref[...].astype(BF16), w2_ref[...]) + gb_ref[...]
    o_ref[...] = _log_sigmoid(z) / B_GATE_TAU


def gla_gate(y, w2p, gate_b):
    n = y.shape[0]
    nk = B_HEADS * B_DK
    return pl.pallas_call(
        _gla_gate_kernel,
        grid=(1,),
        in_specs=[pl.BlockSpec((n, LANES), lambda i: (0, 16)),
                  pl.BlockSpec((LANES, nk), lambda i: (0, 0)),
                  pl.BlockSpec((1, nk), lambda i: (0, 0))],
        out_specs=pl.BlockSpec((n, nk), lambda i: (0, 0)),
        out_shape=jax.ShapeDtypeStruct((n, nk), F32),
        compiler_params=_params("arbitrary"),
        name="gla_gate",
    )(y, w2p, gate_b.reshape(1, nk))


def _gla_step_kernel(q_ref, k_ref, g_ref, v_ref, bog_ref, on_ref, s_ref, o_ref, sn_ref):
    st = jnp.exp(g_ref[...]) * s_ref[...] + k_ref[...] * v_ref[...]
    sn_ref[...] = st
    o = jnp.sum(q_ref[...] * st, axis=2, keepdims=True)
    gate = bog_ref[...]
    o_ref[...] = _rms(o, on_ref[...]) * (gate * jax.nn.sigmoid(gate))


def gla_step(q, k, g, v, bog, out_norm, state, li, *, tb):
    bx = q.shape[0]
    col = pl.BlockSpec((tb, B_HEADS, B_DK, 1), lambda i: (i, 0, 0, 0))
    rowspec = pl.BlockSpec((tb, B_HEADS, 1, B_DV), lambda i: (i, 0, 0, 0))
    return pl.pallas_call(
        _gla_step_kernel,
        grid=(bx // tb,),
        in_specs=[col, col, col, rowspec, rowspec,
                  pl.BlockSpec((1, B_DV), lambda i: (0, 0)),
                  pl.BlockSpec((None, tb, B_HEADS, B_DK, B_DV), lambda i: (li, i, 0, 0, 0))],
        out_specs=[rowspec, pl.BlockSpec((tb, B_HEADS, B_DK, B_DV), lambda i: (i, 0, 0, 0))],
        out_shape=[jax.ShapeDtypeStruct((bx, B_HEADS, 1, B_DV), F32),
                   jax.ShapeDtypeStruct((bx, B_HEADS, B_DK, B_DV), F32)],
        compiler_params=_params("parallel"),
        name="gla_step",
    )(q, k, g, v, bog, out_norm.reshape(1, B_DV), state)


def _dsa_sample_score_kernel(pt_ref, iq_ref, iw_ref, ikn_ref, *refs):
    pages, o_ref = refs[:-1], refs[-1]
    iq = iq_ref[...].astype(BF16)
    iw = iw_ref[...]
    for j, pg in enumerate(pages):
        s = _dot_nt(iq[:, :C_IDX_DIM], pg[...].astype(BF16))
        o_ref[j:j + 1, :] = jnp.sum(jnp.maximum(s, 0.0) * iw, axis=0, keepdims=True)
    s_new = jnp.sum(iq.astype(F32) * ikn_ref[...].astype(BF16).astype(F32), axis=-1, keepdims=True)
    sc_new = jnp.sum(jnp.maximum(s_new, 0.0) * iw, axis=0, keepdims=True)
    np_ = len(pages)
    o_ref[np_:, :] = jnp.broadcast_to(sc_new, (o_ref.shape[0] - np_, LANES))


def dsa_sample_scores(page_table, iq3, iw3, ik_new3, pool_ik, li, n_rows):
    bx, n_pages = page_table.shape
    page_specs = [pl.BlockSpec((None, None, PAGE_SIZE, C_IDX_DIM),
                               functools.partial(lambda b, pt, j: (li, pt[b, j], 0, 0), j=j))
                  for j in range(n_pages)]
    return pl.pallas_call(
        _dsa_sample_score_kernel,
        grid_spec=pltpu.PrefetchScalarGridSpec(
            num_scalar_prefetch=1,
            grid=(bx,),
            in_specs=[pl.BlockSpec((None, C_IDX_HEADS, LANES), lambda b, pt: (b, 0, 0)),
                      pl.BlockSpec((None, C_IDX_HEADS, 1), lambda b, pt: (b, 0, 0)),
                      pl.BlockSpec((None, 1, LANES), lambda b, pt: (b, 0, 0))] + page_specs,
            out_specs=pl.BlockSpec((None, n_rows, LANES), lambda b, pt: (b, 0, 0)),
        ),
        out_shape=jax.ShapeDtypeStruct((bx, n_rows, LANES), F32),
        compiler_params=_params("parallel"),
        name="dsa_sample_scores",
    )(page_table, iq3, iw3, ik_new3, *([pool_ik] * n_pages))


def _select_rows_kernel(sc_ref, o_ref, key_ref, *, topk, idx_bits, n_keys):
    width = sc_ref.shape[1]
    col = lax.broadcasted_iota(jnp.int32, (1, width), 1)
    key_ref[...] = jnp.where(col < n_keys, _sortable_key(sc_ref[...]), INT_MIN)

    def write(c0, val):
        o_ref[:, pl.ds(c0, SEL_CHUNK)] = val

    _threshold_select(key_ref, width // SEL_CHUNK, topk, idx_bits, n_keys - 1, write)


def dsa_select_sample(scores, n_keys):
    bx, width = scores.shape
    assert width % SEL_CHUNK == 0
    topk = min(C_TOPK_MAX, n_keys // 4)
    return pl.pallas_call(
        functools.partial(_select_rows_kernel, topk=topk, idx_bits=max(1, (width - 1).bit_length()),
                          n_keys=n_keys),
        grid=(1,),
        in_specs=[pl.BlockSpec((bx, width), lambda i: (0, 0))],
        out_specs=pl.BlockSpec((bx, width), lambda i: (0, 0)),
        out_shape=jax.ShapeDtypeStruct((bx, width), F32),
        scratch_shapes=[pltpu.VMEM((bx, width), jnp.int32)],
        compiler_params=_params("arbitrary"),
        name="dsa_select_sample",
    )(scores)


def dsa_sample_bias(rel_bias, past):
    n_pages = past // PAGE_SIZE
    grp = C_HEADS // C_KV_HEADS
    bias = rel_bias[_rel_bucket(past - jnp.arange(past))][:, :C_HEADS].astype(F32)
    bias = bias.reshape(n_pages, PAGE_SIZE, C_HEADS).transpose(0, 2, 1)
    own = (jnp.arange(C_HEADS)[:, None] // grp) == jnp.arange(C_KV_HEADS)[None, :]
    t = jnp.where(own[None, :, None, :], bias[..., None], NEG)
    b0 = rel_bias[_rel_bucket(jnp.zeros((), jnp.int32))][:C_HEADS].astype(F32)[:, None]
    return t.reshape(n_pages, C_HEADS, PAGE_SIZE * C_KV_HEADS), b0


def _dsa_sample_attn_kernel(pt_ref, q_ref, kn_ref, vn_ref, mask_ref, bias_ref, b0_ref, *refs):
    n_pages = (len(refs) - 1) // 2
    k_pages, v_pages, o_ref = refs[:n_pages], refs[n_pages:2 * n_pages], refs[-1]
    q = q_ref[...]
    qb = q.astype(BF16)
    s_new = (jnp.sum(q * kn_ref[...], axis=-1, keepdims=True) + b0_ref[...]
             + mask_ref[n_pages:n_pages + 1, 0:1])
    m = s_new
    l = jnp.ones_like(s_new)
    acc = jnp.broadcast_to(vn_ref[...], (C_HEADS, HEAD_DIM)).astype(F32)
    for j in range(n_pages):
        s = _dot_nt(qb, k_pages[j][...].astype(BF16)) + bias_ref[j] + mask_ref[j:j + 1, :]
        m, l, acc = _online_softmax_step(s, v_pages[j][...].astype(BF16), m, l, acc)
    o_ref[...] = acc / l


def dsa_sample_attn(page_table, q3, k_new16, v_new16, mask4, bias4, b0, pool_k, pool_v, li):
    bx, n_pages = page_table.shape
    pw = PAGE_SIZE * C_KV_HEADS
    hspec = pl.BlockSpec((None, C_HEADS, HEAD_DIM), lambda b, pt: (b, 0, 0))
    page_specs = [pl.BlockSpec((None, None, pw, HEAD_DIM),
                               functools.partial(lambda b, pt, j: (li, pt[b, j], 0, 0), j=j))
                  for j in range(n_pages)]
    return pl.pallas_call(
        _dsa_sample_attn_kernel,
        grid_spec=pltpu.PrefetchScalarGridSpec(
            num_scalar_prefetch=1,
            grid=(bx,),
            in_specs=[hspec, hspec, hspec,
                      pl.BlockSpec((None,) + mask4.shape[1:], lambda b, pt: (b, 0, 0)),
                      pl.BlockSpec(bias4.shape, lambda b, pt: (0, 0, 0)),
                      pl.BlockSpec(b0.shape, lambda b, pt: (0, 0))] + page_specs + page_specs,
            out_specs=hspec,
        ),
        out_shape=jax.ShapeDtypeStruct((bx, C_HEADS, HEAD_DIM), F32),
        compiler_params=_params("parallel"),
        name="dsa_sample_attn",
    )(page_table, q3, k_new16, v_new16, mask4, bias4, b0, *([pool_k] * n_pages), *([pool_v] * n_pages))


def sample_forward(x, p, cache_a_k, cache_a_v, state_b, cache_c_k, cache_c_v, cache_c_ik, page_table, w,
                   rel_bias):
    bx, t_len, d = x.shape
    assert t_len == 1
    hd = A_HEADS * HEAD_DIM
    wb = cache_a_k.shape[2]
    n_pages = page_table.shape[1]
    past = n_pages * PAGE_SIZE
    n_rows = -(-(past + 1) // (2 * LANES)) * 2
    dil_bias, dil_b0 = dilated_sample_bias(rel_bias, wb)
    dsa_bias, dsa_b0 = dsa_sample_bias(rel_bias, past)
    pool_k = cache_c_k.reshape(cache_c_k.shape[0], cache_c_k.shape[1], PAGE_SIZE * C_KV_HEADS, HEAD_DIM)
    pool_v = cache_c_v.reshape(pool_k.shape)
    grp = C_HEADS // C_KV_HEADS
    xf = x.reshape(bx, d)
    a_k, a_v, b_s, c_k, c_v, c_ik = [], [], [], [], [], []
    for l in range(len(w["ffn"])):
        li = l // 2
        if l % 2 == 0:
            ab = w["ab"][li]
            qk = norm_proj(xf, w["g_mix"][l], ab["w1"], ab["s1"], head_norm=True, tm=bx)
            y = norm_proj(xf, w["g_mix"][l], ab["wy"], ab["sy"], head_norm=False, tm=bx)
            v_new = y[:, :hd].reshape(bx, A_HEADS, HEAD_DIM)
            oa = dilated_sample(qk.reshape(bx, 2 * A_HEADS, HEAD_DIM), v_new, cache_a_k, cache_a_v, li,
                                dil_bias, dil_b0)
            g = gla_gate(y, ab["w2p"], w["b_gate_b"][li])
            nk = B_HEADS * B_DK
            colv = lambda a: a.reshape(bx, B_HEADS, B_DK, 1)
            rowv = lambda a: a.reshape(bx, B_HEADS, 1, B_DV)
            ob, s_fin = gla_step(colv(y[:, hd:hd + nk]), colv(y[:, hd + nk:hd + 2 * nk]), colv(g),
                                 rowv(y[:, 2 * hd:2 * hd + B_HEADS * B_DV]),
                                 rowv(y[:, 2 * hd + B_HEADS * B_DV:2 * hd + 2 * B_HEADS * B_DV]),
                                 w["b_out_norm"][li], state_b, li, tb=8)
            xf = out_proj([oa.reshape(bx, hd), ob.reshape(bx, B_HEADS * B_DV)], [ab["wo_a"], ab["wo_b"]], xf, tm=bx)
            a_k.append(qk[:, hd:].reshape(bx, 1, A_HEADS, HEAD_DIM))
            a_v.append(v_new.reshape(bx, 1, A_HEADS, HEAD_DIM))
            b_s.append(s_fin)
        else:
            c = w["c"][li]
            y1 = norm_proj(xf, w["g_mix"][l], c["w1"], c["s1"], head_norm=True, tm=bx)
            y2 = norm_proj(xf, w["g_mix"][l], c["w2"], c["s2"], head_norm=False, tm=bx)
            k0 = C_HEADS * HEAD_DIM
            ik0 = k0 + C_KV_HEADS * LANES
            iw0 = C_IDX_HEADS * LANES + C_KV_HEADS * LANES
            k_new = _first_half(y1[:, k0:ik0], C_KV_HEADS)
            v_new = _first_half(y2[:, k0:ik0], C_KV_HEADS)
            scores = dsa_sample_scores(page_table, y2[:, :k0].reshape(bx, C_IDX_HEADS, LANES),
                                       y2[:, iw0:iw0 + C_IDX_HEADS].reshape(bx, C_IDX_HEADS, 1),
                                       y1[:, ik0:ik0 + LANES].reshape(bx, 1, LANES),
                                       cache_c_ik, li, n_rows)
            mask = dsa_select_sample(scores.reshape(bx, n_rows * LANES), past + 1)
            mask4 = jnp.repeat(mask.reshape(bx, n_rows, LANES), C_KV_HEADS, axis=-1)
            oc = dsa_sample_attn(page_table, y1[:, :k0].reshape(bx, C_HEADS, HEAD_DIM),
                                 jnp.repeat(k_new, grp, axis=1), jnp.repeat(v_new, grp, axis=1),
                                 mask4, dsa_bias, dsa_b0, pool_k, pool_v, li)
            xf = out_proj([oc.reshape(bx, k0)], [c["wo"]], xf, tm=bx)
            c_k.append(k_new.reshape(bx, 1, C_KV_HEADS, HEAD_DIM))
            c_v.append(v_new.reshape(bx, 1, C_KV_HEADS, HEAD_DIM))
            c_ik.append(y1[:, ik0:ik0 + C_IDX_DIM].reshape(bx, 1, C_IDX_DIM))
        f = w["ffn"][l]
        xf = ffn_ple(xf, p[l].reshape(bx, -1), f["gf"], f["w1"], f["w3"], f["w2"], f["gp"], f["wg"], f["wp"],
                     tm=bx, tf=256)
    return (xf.reshape(bx, 1, d), jnp.stack(a_k), jnp.stack(a_v), jnp.stack(b_s),
            jnp.stack(c_k), jnp.stack(c_v), jnp.stack(c_ik))


def prep_weights(rel_bias, g_mix, w_in_ab, a_q_norm, a_k_norm, b_gate_w2, b_gate_b, b_out_norm, w_out_ab,
                 w_in_c, c_q_norm, c_k_norm, c_ik_norm, w_out_c, g_ffn, w_ff1, w_ff3, w_ff2, g_ple,
                 w_ple_gate, w_ple_proj):
    w = dict(g_mix=g_mix, b_gate_b=b_gate_b, b_out_norm=b_out_norm)
    w["ab"] = [prep_ab(w_in_ab[i], a_q_norm[i], a_k_norm[i], b_gate_w2[i], w_out_ab[i])
               for i in range(w_in_ab.shape[0])]
    w["c"] = [prep_c(w_in_c[i], c_q_norm[i], c_k_norm[i], c_ik_norm[i], w_out_c[i])
              for i in range(w_in_c.shape[0])]
    w["ffn"] = [dict(gf=g_ffn[l], w1=w_ff1[l].astype(BF16), w3=w_ff3[l].astype(BF16), w2=w_ff2[l].astype(BF16),
                     gp=g_ple[l], wg=w_ple_gate[l].astype(BF16), wp=w_ple_proj[l].astype(BF16))
                for l in range(g_ffn.shape[0])]
    tables = dict(dil=dilated_bias_table(rel_bias), dsa=dsa_bias_table(rel_bias))
    return w, tables


def kernel(x_prompt, x_sample, cache_a_k, cache_a_v, state_b, cache_c_k, cache_c_v, cache_c_ik, page_table,
           p_prompt, p_sample, rel_bias, g_mix, w_in_ab, a_q_norm, a_k_norm, b_gate_w2, b_gate_b, b_out_norm,
           w_out_ab, w_in_c, c_q_norm, c_k_norm, c_ik_norm, w_out_c, g_ffn, w_ff1, w_ff3, w_ff2, g_ple,
           w_ple_gate, w_ple_proj):
    w, tables = prep_weights(rel_bias, g_mix, w_in_ab, a_q_norm, a_k_norm, b_gate_w2, b_gate_b, b_out_norm,
                             w_out_ab, w_in_c, c_q_norm, c_k_norm, c_ik_norm, w_out_c, g_ffn, w_ff1, w_ff3,
                             w_ff2, g_ple, w_ple_gate, w_ple_proj)
    prompt = prompt_forward(x_prompt, p_prompt, w, tables)
    sample = sample_forward(x_sample, p_sample, cache_a_k, cache_a_v, state_b, cache_c_k, cache_c_v,
                            cache_c_ik, page_table, w, rel_bias)
    return (prompt[0], sample[0]) + tuple(prompt[1:]) + tuple(sample[1:])
```

```python
import functools
import math

import jax
import jax.numpy as jnp
import numpy as np
from jax import lax
from jax.experimental import pallas as pl
from jax.experimental.pallas import tpu as pltpu

F32 = jnp.float32
BF16 = jnp.bfloat16

LANES = 128
HEAD_DIM = 64
A_HEADS = 8
A_BRANCHES = ((128, 1), (512, 4), (2048, 16))
A_WIN_MAX = 2048
B_HEADS = 4
B_DK = 64
B_DV = 128
B_GATE_RANK = 16
B_GATE_TAU = 16.0
C_HEADS = 16
C_KV_HEADS = 4
C_IDX_HEADS = 8
C_IDX_DIM = 64
C_TOPK_MAX = 256
N_BUCKETS = 32
REL_MAX_DIST = A_WIN_MAX
Q_BLOCK = 128
PAGE_SIZE = 128
NORM_EPS = 1e-6
NEG = -1e30
VMEM_LIMIT = 56 * 1024 * 1024


def _params(*sem):
    return pltpu.CompilerParams(dimension_semantics=sem, vmem_limit_bytes=VMEM_LIMIT)


def _rms(x, g):
    ms = jnp.mean(x * x, axis=-1, keepdims=True)
    return x * lax.rsqrt(ms + NORM_EPS) * g


def _dot(a, b):
    return jnp.dot(a, b, preferred_element_type=F32)


def _dot_nt(a, b):
    return lax.dot_general(a, b, (((1,), (1,)), ((), ())), preferred_element_type=F32)


def _dot_tn(a, b):
    return lax.dot_general(a, b, (((0,), (0,)), ((), ())), preferred_element_type=F32)


PROJ_CHUNK = 256


def _proj_kernel(x_ref, g_ref, w_ref, cs_ref, p_ref, o_ref, *, head_norm):
    hn = _rms(x_ref[...], g_ref[...]).astype(BF16)
    for c in range(o_ref.shape[1] // PROJ_CHUNK):
        sl = slice(c * PROJ_CHUNK, (c + 1) * PROJ_CHUNK)
        y = _dot(hn, w_ref[:, sl])
        if head_norm:
            ms = _dot((y * y).astype(BF16), p_ref[...])
            y = y * lax.rsqrt(ms + NORM_EPS)
        o_ref[:, sl] = y * cs_ref[:, sl]


def _group_mean_matrix():
    r = np.arange(PROJ_CHUNK) // HEAD_DIM
    return jnp.asarray((r[:, None] == r[None, :]).astype(np.float32) / HEAD_DIM, BF16)


def norm_proj(x, g, w, colscale, *, head_norm, tm):
    n, d = x.shape
    dout = w.shape[1]
    assert n % tm == 0 and dout % PROJ_CHUNK == 0
    return pl.pallas_call(
        functools.partial(_proj_kernel, head_norm=head_norm),
        grid=(n // tm,),
        in_specs=[
            pl.BlockSpec((tm, d), lambda i: (i, 0)),
            pl.BlockSpec((1, d), lambda i: (0, 0)),
            pl.BlockSpec((d, dout), lambda i: (0, 0)),
            pl.BlockSpec((1, dout), lambda i: (0, 0)),
            pl.BlockSpec((PROJ_CHUNK, PROJ_CHUNK), lambda i: (0, 0)),
        ],
        out_specs=pl.BlockSpec((tm, dout), lambda i: (i, 0)),
        out_shape=jax.ShapeDtypeStruct((n, dout), F32),
        compiler_params=_params("parallel"),
        name="norm_proj_hn" if head_norm else "norm_proj",
    )(x, g.reshape(1, d), w, colscale.reshape(1, dout), _group_mean_matrix())


def _out_kernel(*refs, n_in):
    res_ref, o_ref = refs[2 * n_in], refs[2 * n_in + 1]
    acc = res_ref[...]
    for a_ref, w_ref in zip(refs[:n_in], refs[n_in:2 * n_in]):
        acc = acc + _dot(a_ref[...].astype(BF16), w_ref[...])
    o_ref[...] = acc


def out_proj(a_list, w_list, res, *, tm):
    n, d = res.shape
    n_in = len(a_list)
    in_specs = [pl.BlockSpec((tm, a.shape[1]), lambda i: (i, 0)) for a in a_list]
    in_specs += [pl.BlockSpec(w.shape, lambda i: (0, 0)) for w in w_list]
    in_specs += [pl.BlockSpec((tm, d), lambda i: (i, 0))]
    return pl.pallas_call(
        functools.partial(_out_kernel, n_in=n_in),
        grid=(n // tm,),
        in_specs=in_specs,
        out_specs=pl.BlockSpec((tm, d), lambda i: (i, 0)),
        out_shape=jax.ShapeDtypeStruct((n, d), F32),
        compiler_params=_params("parallel"),
        name="out_proj",
    )(*a_list, *w_list, res)


def _ffn_kernel(x_ref, gf_ref, w1_ref, w3_ref, w2_ref, gp_ref, wg_ref, p_ref, wp_ref, o_ref,
                hn_ref, acc_ref):
    f = pl.program_id(1)

    @pl.when(f == 0)
    def _():
        hn_ref[...] = _rms(x_ref[...], gf_ref[...]).astype(BF16)
        acc_ref[...] = jnp.zeros_like(acc_ref)

    hn = hn_ref[...]
    h1 = _dot(hn, w1_ref[...])
    h3 = _dot(hn, w3_ref[...])
    a = h1 * jax.nn.sigmoid(h1) * h3
    acc_ref[...] += _dot(a.astype(BF16), w2_ref[...])

    @pl.when(f == pl.num_programs(1) - 1)
    def _():
        x2 = x_ref[...] + acc_ref[...]
        u = _rms(x2, gp_ref[...]).astype(BF16)
        gate = jax.nn.sigmoid(_dot(u, wg_ref[...]))
        o_ref[...] = x2 + gate * _dot(p_ref[...].astype(BF16), wp_ref[...])


def ffn_ple(x, p, gf, w1, w3, w2, gp, wg, wp, *, tm, tf):
    n, d = x.shape
    dff = w1.shape[1]
    dple = p.shape[1]
    assert n % tm == 0 and dff % tf == 0
    return pl.pallas_call(
        _ffn_kernel,
        grid=(n // tm, dff // tf),
        in_specs=[
            pl.BlockSpec((tm, d), lambda i, f: (i, 0)),
            pl.BlockSpec((1, d), lambda i, f: (0, 0)),
            pl.BlockSpec((d, tf), lambda i, f: (0, f)),
            pl.BlockSpec((d, tf), lambda i, f: (0, f)),
            pl.BlockSpec((tf, d), lambda i, f: (f, 0)),
            pl.BlockSpec((1, d), lambda i, f: (0, 0)),
            pl.BlockSpec((d, d), lambda i, f: (0, 0)),
            pl.BlockSpec((tm, dple), lambda i, f: (i, 0)),
            pl.BlockSpec((dple, d), lambda i, f: (0, 0)),
        ],
        out_specs=pl.BlockSpec((tm, d), lambda i, f: (i, 0)),
        out_shape=jax.ShapeDtypeStruct((n, d), F32),
        scratch_shapes=[pltpu.VMEM((tm, d), BF16), pltpu.VMEM((tm, d), F32)],
        compiler_params=_params("parallel", "arbitrary"),
        name="ffn_ple",
    )(x, gf.reshape(1, d), w1, w3, w2, gp.reshape(1, d), wg, p, wp)


def _rel_bucket(dist):
    n = jnp.maximum(dist, 0)
    exact = N_BUCKETS // 2
    nf = jnp.maximum(n, exact).astype(F32)
    large = exact + (jnp.log(nf / exact) / math.log(REL_MAX_DIST / exact)
                     * (N_BUCKETS - exact)).astype(jnp.int32)
    return jnp.where(n < exact, n, jnp.minimum(large, N_BUCKETS - 1))


def _branch_multiplicity(delta):
    mult = jnp.zeros(delta.shape, jnp.int32)
    for w, d in A_BRANCHES:
        mult = mult + ((delta >= 0) & (delta <= w) & (delta % d == 0)).astype(jnp.int32)
    return mult


DIL_WIN_CHUNKS = A_WIN_MAX // Q_BLOCK + 1


def _toeplitz(value_of_delta, base, width):
    period = width + Q_BLOCK - 1
    x = jnp.concatenate([jnp.arange(width), jnp.arange(-(Q_BLOCK - 1), 0)])
    v = value_of_delta(base - x)
    t = jnp.tile(v, (1, Q_BLOCK))[:, :Q_BLOCK * (period - 1)]
    return t.reshape(v.shape[0], Q_BLOCK, period - 1)[:, :, :width]


def dilated_bias_table(rel_bias):
    def value(delta):
        mult = _branch_multiplicity(delta)
        bias = rel_bias[_rel_bucket(delta)][:, :A_HEADS].astype(F32).T
        return jnp.where(mult[None] > 0, bias + jnp.log(jnp.maximum(mult, 1).astype(F32))[None], NEG)

    t = _toeplitz(value, A_WIN_MAX, DIL_WIN_CHUNKS * Q_BLOCK)
    t = t.reshape(A_HEADS // 2, 2 * Q_BLOCK, DIL_WIN_CHUNKS * Q_BLOCK)
    return jnp.pad(t, ((0, 0), (0, 0), (0, Q_BLOCK)), constant_values=NEG)


DSA_BIAS_TILES = 14


def dsa_bias_table(rel_bias):
    def value(delta):
        return rel_bias[_rel_bucket(delta)][:, :C_HEADS].astype(F32).T

    last = DSA_BIAS_TILES - 1
    t = _toeplitz(value, last * Q_BLOCK, DSA_BIAS_TILES * Q_BLOCK)
    grp = C_HEADS // C_KV_HEADS
    t = t.reshape(C_KV_HEADS, grp, Q_BLOCK, DSA_BIAS_TILES, Q_BLOCK)[:, :, :, ::-1, :]
    return t.transpose(0, 3, 1, 2, 4).reshape(C_KV_HEADS, DSA_BIAS_TILES, grp * Q_BLOCK, Q_BLOCK)


def _lane_lt64():
    return lax.broadcasted_iota(jnp.int32, (1, LANES), 1) < HEAD_DIM


def _online_softmax_step(s, v_bf16, m, l, acc):
    m_new = jnp.maximum(m, jnp.max(s, axis=1, keepdims=True))
    alpha = jnp.exp(m - m_new)
    p = jnp.exp(s - m_new)
    l = alpha * l + jnp.sum(p, axis=1, keepdims=True)
    acc = alpha * acc + _dot(p.astype(BF16), v_bf16)
    return m_new, l, acc


def _dil_kernel(q_ref, k_ref, v_ref, t_ref, o_ref, s_ref, p_ref):
    i = pl.program_id(2)
    nw = s_ref.shape[1] // Q_BLOCK
    last = DIL_WIN_CHUNKS - 1
    lo = _lane_lt64()
    q = q_ref[...]
    q2 = jnp.concatenate([jnp.where(lo, q, 0.0), jnp.where(lo, 0.0, q)], axis=0).astype(BF16)
    w0 = jnp.maximum(i - (nw - 1), 0)
    rows = pl.ds(pl.multiple_of(w0 * Q_BLOCK, Q_BLOCK), nw * Q_BLOCK)
    s_ref[...] = _dot_nt(q2, k_ref[rows, :].astype(BF16))
    m = jnp.full((2 * Q_BLOCK, LANES), NEG, F32)
    for w in range(nw):
        c = last - i + w0 + w
        tcol = pl.multiple_of(jnp.where(c <= last, c, last + 1) * Q_BLOCK, Q_BLOCK)
        cols = slice(w * Q_BLOCK, (w + 1) * Q_BLOCK)
        s = s_ref[:, cols] + t_ref[0, :, pl.ds(tcol, Q_BLOCK)]
        s_ref[:, cols] = s
        m = jnp.maximum(m, s)
    m = jnp.max(m, axis=1, keepdims=True)
    l = jnp.zeros((2 * Q_BLOCK, LANES), F32)
    for w in range(nw):
        cols = slice(w * Q_BLOCK, (w + 1) * Q_BLOCK)
        p = jnp.exp(s_ref[:, cols] - m)
        l = l + p
        p_ref[:, cols] = p.astype(BF16)
    o = _dot(p_ref[...], v_ref[rows, :].astype(BF16)) / jnp.sum(l, axis=1, keepdims=True)
    o_ref[...] = jnp.where(lo, o[:Q_BLOCK], o[Q_BLOCK:])


def dilated_prompt(qk, yv, table, batch, seq):
    n = batch * seq
    nblk = seq // Q_BLOCK
    npair = A_HEADS // 2
    win = min(DIL_WIN_CHUNKS, nblk) * Q_BLOCK
    return pl.pallas_call(
        _dil_kernel,
        grid=(batch, npair, nblk),
        in_specs=[
            pl.BlockSpec((Q_BLOCK, LANES), lambda b, p, i: (b * nblk + i, p)),
            pl.BlockSpec((seq, LANES), lambda b, p, i: (b, npair + p)),
            pl.BlockSpec((seq, LANES), lambda b, p, i: (b, p)),
            pl.BlockSpec((1,) + table.shape[1:], lambda b, p, i: (p, 0, 0)),
        ],
        out_specs=pl.BlockSpec((Q_BLOCK, LANES), lambda b, p, i: (b * nblk + i, p)),
        out_shape=jax.ShapeDtypeStruct((n, A_HEADS * HEAD_DIM), F32),
        scratch_shapes=[pltpu.VMEM((2 * Q_BLOCK, win), F32), pltpu.VMEM((2 * Q_BLOCK, win), BF16)],
        compiler_params=_params("parallel", "parallel", "arbitrary"),
        name="dilated_prompt",
    )(qk, qk, yv, table)


GLA_CHUNK = 64
GLA_SUB = 16
GLA_EXP_CLAMP = 60.0


def _log_sigmoid(z):
    return jnp.minimum(z, 0.0) - jnp.log1p(jnp.exp(-jnp.abs(z)))


def _gla_kernel(bq_ref, bk_ref, bv_ref, bog_ref, blr_ref, w2_ref, gb_ref, on_ref, o_ref, s_ref, st_ref):
    it = pl.program_id(1)

    @pl.when(it == 0)
    def _():
        st_ref[...] = jnp.zeros_like(st_ref)

    lo = _lane_lt64()
    ch = GLA_CHUNK
    row = lax.broadcasted_iota(jnp.int32, (ch, ch), 0)
    col = lax.broadcasted_iota(jnp.int32, (ch, ch), 1)
    causal = row >= col
    ltri = jnp.where(causal, 1.0, 0.0).astype(BF16)
    top_half = lax.broadcasted_iota(jnp.int32, (LANES, LANES), 0) < HEAD_DIM

    def chunk_body(c, carry):
        rows = pl.ds(pl.multiple_of(c * ch, ch), ch)
        z = _dot(blr_ref[rows, :].astype(BF16), w2_ref[...]) + gb_ref[...]
        gg = _log_sigmoid(z) / B_GATE_TAU
        for p in range(B_HEADS // 2):
            sl = slice(p * LANES, (p + 1) * LANES)
            q = bq_ref[rows, sl]
            k = bk_ref[rows, sl]
            g = gg[:, sl]
            g_hi = g.astype(BF16)
            g_lo = (g - g_hi.astype(F32)).astype(BF16)
            cum = _dot(ltri, g_hi) + _dot(ltri, g_lo)
            last = cum[ch - 1:ch, :]
            st = st_ref[p]
            qd = q * jnp.exp(cum)
            q2 = jnp.concatenate([jnp.where(lo, qd, 0.0), jnp.where(lo, 0.0, qd)], axis=0)
            o_inter = _dot(q2.astype(BF16), st.astype(BF16))
            atts = []
            for sb in range(ch // GLA_SUB):
                rs = slice(sb * GLA_SUB, (sb + 1) * GLA_SUB)
                ref_row = cum[sb * GLA_SUB:sb * GLA_SUB + 1, :]
                qs = q[rs] * jnp.exp(cum[rs] - ref_row)
                ks = k * jnp.exp(jnp.minimum(ref_row - cum, GLA_EXP_CLAMP))
                qq = jnp.concatenate([jnp.where(lo, qs, 0.0), jnp.where(lo, 0.0, qs)], axis=0)
                atts.append(_dot_nt(qq.astype(BF16), ks.astype(BF16)))
            kd = (k * jnp.exp(last - cum)).astype(BF16)
            upd = []
            for e in range(2):
                hs = slice((2 * p + e) * LANES, (2 * p + e + 1) * LANES)
                v = bv_ref[rows, hs].astype(BF16)
                att = jnp.concatenate([a[e * GLA_SUB:(e + 1) * GLA_SUB] for a in atts], axis=0)
                att = jnp.where(causal, att, 0.0)
                o = o_inter[e * ch:(e + 1) * ch] + _dot(att.astype(BF16), v)
                og = _rms(o, on_ref[...])
                gate = bog_ref[rows, hs]
                o_ref[rows, hs] = og * (gate * jax.nn.sigmoid(gate))
                upd.append(_dot_tn(kd, v))
            decay = jnp.transpose(jnp.broadcast_to(jnp.exp(last), (LANES, LANES)))
            st_ref[p] = decay * st + jnp.where(top_half, upd[0], upd[1])
        return carry

    lax.fori_loop(0, bq_ref.shape[0] // ch, chunk_body, 0)

    @pl.when(it == pl.num_programs(1) - 1)
    def _():
        for p in range(B_HEADS // 2):
            s_ref[0, 2 * p] = st_ref[p, :HEAD_DIM, :]
            s_ref[0, 2 * p + 1] = st_ref[p, HEAD_DIM:, :]


def gla_prompt(y, w2p, gate_b, out_norm, batch, seq, *, tg):
    n = batch * seq
    nt = seq // tg
    dv = B_HEADS * B_DV
    return pl.pallas_call(
        _gla_kernel,
        grid=(batch, nt),
        in_specs=[
            pl.BlockSpec((tg, 256), lambda b, t: (b * nt + t, 2)),
            pl.BlockSpec((tg, 256), lambda b, t: (b * nt + t, 3)),
            pl.BlockSpec((tg, dv), lambda b, t: (b * nt + t, 2)),
            pl.BlockSpec((tg, dv), lambda b, t: (b * nt + t, 3)),
            pl.BlockSpec((tg, LANES), lambda b, t: (b * nt + t, 16)),
            pl.BlockSpec((LANES, 256), lambda b, t: (0, 0)),
            pl.BlockSpec((1, 256), lambda b, t: (0, 0)),
            pl.BlockSpec((1, B_DV), lambda b, t: (0, 0)),
        ],
        out_specs=[
            pl.BlockSpec((tg, dv), lambda b, t: (b * nt + t, 0)),
            pl.BlockSpec((1, B_HEADS, B_DK, B_DV), lambda b, t: (b, 0, 0, 0)),
        ],
        out_shape=[jax.ShapeDtypeStruct((n, dv), F32),
                   jax.ShapeDtypeStruct((batch, B_HEADS, B_DK, B_DV), F32)],
        scratch_shapes=[pltpu.VMEM((B_HEADS // 2, LANES, LANES), F32)],
        compiler_params=_params("parallel", "arbitrary"),
        name="gla_prompt",
    )(y, y, y, y, y, w2p, gate_b.reshape(1, 256), out_norm.reshape(1, B_DV))


SEL_CHUNK = 256
INT_MIN = -2 ** 31


def _sortable_key(score):
    bits = pltpu.bitcast(score + 0.0, jnp.int32)
    return jnp.where(bits < 0, bits ^ jnp.int32(0x7FFFFFFF), bits)


def _threshold_select(key_ref, n_ch, topk, idx_bits, row_limit, write):
    r = key_ref.shape[0]
    kc = SEL_CHUNK
    lane = lax.broadcasted_iota(jnp.int32, (1, kc), 1)

    def count(hits):
        def body(c, acc):
            c0 = pl.multiple_of(c * kc, kc)
            hit = hits(key_ref[:, pl.ds(c0, kc)], c0 + lane)
            for t in range(kc // LANES):
                acc = acc + hit[:, t * LANES:(t + 1) * LANES]
            return acc
        acc = lax.fori_loop(0, n_ch, body, jnp.zeros((r, LANES), jnp.int32))
        return jnp.sum(acc, axis=1, keepdims=True)

    n_nonneg = count(lambda k, _: jnp.where(k >= 0, 1, 0))
    base = jnp.where(n_nonneg >= topk, 0, INT_MIN).astype(jnp.int32)
    n_ge = jnp.where(n_nonneg >= topk, n_nonneg, n_ch * kc)

    def bit_body(b, carry):
        base, n_ge = carry
        cand = base | jnp.left_shift(jnp.int32(1), 30 - b)
        cnt = count(lambda k, _: jnp.where(k >= cand, 1, 0))
        return jnp.where(cnt >= topk, cand, base), jnp.where(cnt >= topk, cnt, n_ge)

    tau, n_ge = lax.fori_loop(0, 31, bit_body, (base, n_ge))

    def tie_break():
        need = topk - count(lambda k, _: jnp.where(k > tau, 1, 0))

        def idx_body(b, lo):
            cand = lo | jnp.left_shift(jnp.int32(1), idx_bits - 1 - b)
            cnt = count(lambda k, col: jnp.where(k == tau, jnp.where(col < cand, 1, 0), 0))
            return jnp.where(cnt < need, cand, lo)

        return lax.fori_loop(0, idx_bits, idx_body, jnp.zeros((r, 1), jnp.int32))

    last_eq = lax.cond(jnp.max(n_ge) > topk, tie_break,
                       lambda: jnp.full((r, 1), 2 ** idx_bits, jnp.int32))

    def out_body(c, carry):
        c0 = pl.multiple_of(c * kc, kc)
        k = key_ref[:, pl.ds(c0, kc)]
        col = c0 + lane
        val = jnp.where(k > tau, 0.0, jnp.where(k == tau, jnp.where(col <= last_eq, 0.0, NEG), NEG))
        write(c0, jnp.where(col <= row_limit, val, NEG))
        return carry

    lax.fori_loop(0, n_ch, out_body, 0)


def _select_kernel(iq_ref, iw_ref, ik_ref, o_ref, key_ref, *, topk, idx_bits):
    blk = pl.program_id(1)
    kc = SEL_CHUNK
    n_ch = (blk * Q_BLOCK + Q_BLOCK + kc - 1) // kc
    t_idx = blk * Q_BLOCK + lax.broadcasted_iota(jnp.int32, (Q_BLOCK, 1), 0)
    lane = lax.broadcasted_iota(jnp.int32, (1, kc), 1)
    iw = iw_ref[...]

    def score_body(c, carry):
        c0 = pl.multiple_of(c * kc, kc)
        ik = ik_ref[pl.ds(c0, kc), :].astype(BF16)
        sc = jnp.zeros((Q_BLOCK, kc), F32)
        for h in range(C_IDX_HEADS):
            qh = iq_ref[:, h * LANES:(h + 1) * LANES].astype(BF16)
            sc = sc + jnp.maximum(_dot_nt(qh, ik), 0.0) * iw[:, h:h + 1]
        key_ref[:, pl.ds(c0, kc)] = jnp.where(c0 + lane <= t_idx, _sortable_key(sc), INT_MIN)
        return carry

    lax.fori_loop(0, n_ch, score_body, 0)
    o_ref[...] = jnp.full(o_ref.shape, NEG, o_ref.dtype)

    def write(c0, val):
        o_ref[:, pl.ds(c0, kc)] = val.astype(o_ref.dtype)

    _threshold_select(key_ref, n_ch, topk, idx_bits, t_idx, write)


def dsa_select_prompt(y2, y1, batch, seq):
    n = batch * seq
    assert seq % SEL_CHUNK == 0
    nblk = seq // Q_BLOCK
    topk = min(C_TOPK_MAX, seq // 4)
    return pl.pallas_call(
        functools.partial(_select_kernel, topk=topk, idx_bits=max(1, (seq - 1).bit_length())),
        grid=(batch, nblk),
        in_specs=[
            pl.BlockSpec((Q_BLOCK, C_IDX_HEADS * LANES), lambda b, i: (b * nblk + i, 0)),
            pl.BlockSpec((Q_BLOCK, LANES), lambda b, i: (b * nblk + i, 12)),
            pl.BlockSpec((seq, LANES), lambda b, i: (b, 12)),
        ],
        out_specs=pl.BlockSpec((Q_BLOCK, seq), lambda b, i: (b * nblk + i, 0)),
        out_shape=jax.ShapeDtypeStruct((n, seq), BF16),
        scratch_shapes=[pltpu.VMEM((Q_BLOCK, seq), jnp.int32)],
        compiler_params=_params("parallel", "arbitrary"),
        name="dsa_select_prompt",
    )(y2, y2, y1)


def _stack_heads(q):
    lo = _lane_lt64()
    parts = []
    for j in range(q.shape[1] // LANES):
        qp = q[:, j * LANES:(j + 1) * LANES]
        parts += [jnp.where(lo, qp, 0.0), jnp.where(lo, 0.0, qp)]
    return jnp.concatenate(parts, axis=0)


def _unstack_heads(o, rows):
    lo = _lane_lt64()
    return [jnp.where(lo, o[(2 * j) * rows:(2 * j + 1) * rows], o[(2 * j + 1) * rows:(2 * j + 2) * rows])
            for j in range(o.shape[0] // (2 * rows))]


def _dsa_attn_kernel(q_ref, k_ref, v_ref, mask_ref, tb_ref, o_ref):
    blk = pl.program_id(2)
    kc = SEL_CHUNK
    grp = C_HEADS // C_KV_HEADS
    q4 = _stack_heads(q_ref[...]).astype(BF16)

    def body(c, carry):
        c0 = pl.multiple_of(c * kc, kc)
        k = k_ref[pl.ds(c0, kc), :].astype(BF16)
        v = v_ref[pl.ds(c0, kc), :].astype(BF16)
        s = _dot_nt(q4, k)
        mk = mask_ref[:, pl.ds(c0, kc)].astype(F32)
        halves = []
        for hf in range(kc // Q_BLOCK):
            e = jnp.clip(blk - (c * (kc // Q_BLOCK) + hf), 0, DSA_BIAS_TILES - 1)
            cols = slice(hf * Q_BLOCK, (hf + 1) * Q_BLOCK)
            halves.append(s[:, cols] + tb_ref[0, e] + jnp.concatenate([mk[:, cols]] * grp, axis=0))
        return _online_softmax_step(jnp.concatenate(halves, axis=1), v, *carry)

    rows = grp * Q_BLOCK
    init = (jnp.full((rows, 1), NEG, F32), jnp.zeros((rows, 1), F32), jnp.zeros((rows, LANES), F32))
    n_ch = (blk * Q_BLOCK + Q_BLOCK + kc - 1) // kc
    _, l, acc = lax.fori_loop(0, n_ch, body, init)
    outs = _unstack_heads(acc / l, Q_BLOCK)
    for j, o in enumerate(outs):
        o_ref[:, j * LANES:(j + 1) * LANES] = o


def dsa_attn_prompt(y1, y2, mask, table, batch, seq):
    n = batch * seq
    nblk = seq // Q_BLOCK
    qw = (C_HEADS // C_KV_HEADS) * HEAD_DIM
    return pl.pallas_call(
        _dsa_attn_kernel,
        grid=(batch, C_KV_HEADS, nblk),
        in_specs=[
            pl.BlockSpec((Q_BLOCK, qw), lambda b, g, i: (b * nblk + i, g)),
            pl.BlockSpec((seq, LANES), lambda b, g, i: (b, 8 + g)),
            pl.BlockSpec((seq, LANES), lambda b, g, i: (b, 8 + g)),
            pl.BlockSpec((Q_BLOCK, seq), lambda b, g, i: (b * nblk + i, 0)),
            pl.BlockSpec((1,) + table.shape[1:], lambda b, g, i: (g, 0, 0, 0)),
        ],
        out_specs=pl.BlockSpec((Q_BLOCK, qw), lambda b, g, i: (b * nblk + i, g)),
        out_shape=jax.ShapeDtypeStruct((n, C_HEADS * HEAD_DIM), F32),
        compiler_params=_params("parallel", "parallel", "arbitrary"),
        name="dsa_attn_prompt",
    )(y1, y1, y2, mask, table)


IW_SCALE = (C_IDX_HEADS ** -0.5) * (C_IDX_DIM ** -0.5)
Q_SCALE = HEAD_DIM ** -0.5
AB_Y_COLS = 2304
C_Y_COLS = 1792


def _dup_heads(w, n_heads):
    d = w.shape[0]
    w = w.reshape(d, n_heads, 1, HEAD_DIM)
    return jnp.broadcast_to(w, (d, n_heads, 2, HEAD_DIM)).reshape(d, n_heads * LANES)


def _pad_heads(w, n_heads):
    d = w.shape[0]
    w = w.reshape(d, n_heads, HEAD_DIM)
    return jnp.pad(w, ((0, 0), (0, 0), (0, HEAD_DIM))).reshape(d, n_heads * LANES)


def _pad_cols(w, total):
    return jnp.pad(w, ((0, 0), (0, total - w.shape[1])))


def prep_ab(w_in, a_q_norm, a_k_norm, w2, w_out):
    hd = A_HEADS * HEAD_DIM
    w1 = w_in[:, :2 * hd]
    s1 = jnp.concatenate([jnp.tile(a_q_norm, A_HEADS) * Q_SCALE, jnp.tile(a_k_norm, A_HEADS)])
    nbq = B_HEADS * B_DK
    blr0 = 3 * hd + 2 * nbq + B_HEADS * B_DV
    wy = jnp.concatenate([w_in[:, 2 * hd:blr0], w_in[:, blr0 + B_GATE_RANK:], w_in[:, blr0:blr0 + B_GATE_RANK]],
                         axis=1)
    wy = _pad_cols(wy, AB_Y_COLS)
    sy = jnp.ones((AB_Y_COLS,), F32).at[hd:hd + nbq].set(Q_SCALE)
    w2p = jnp.pad(w2, ((0, LANES - B_GATE_RANK), (0, 0)))
    return dict(w1=w1.astype(BF16), s1=s1, wy=wy.astype(BF16), sy=sy, w2p=w2p.astype(BF16),
                wo_a=w_out[:hd].astype(BF16), wo_b=w_out[hd:].astype(BF16))


def prep_c(w_in, c_q_norm, c_k_norm, c_ik_norm, w_out):
    o = np.cumsum((0, C_HEADS * HEAD_DIM, C_KV_HEADS * HEAD_DIM, C_KV_HEADS * HEAD_DIM,
                   C_IDX_HEADS * C_IDX_DIM, C_IDX_DIM, C_IDX_HEADS))
    cq, ck, cv, iq, ik, iw = (w_in[:, o[t]:o[t + 1]] for t in range(6))
    w1 = _pad_cols(jnp.concatenate([cq, _dup_heads(ck, C_KV_HEADS), ik], axis=1), C_Y_COLS)
    s1 = jnp.concatenate([jnp.tile(c_q_norm, C_HEADS) * Q_SCALE, jnp.tile(c_k_norm, 2 * C_KV_HEADS), c_ik_norm])
    s1 = jnp.pad(s1, (0, C_Y_COLS - s1.shape[0]))
    w2 = _pad_cols(jnp.concatenate([_pad_heads(iq, C_IDX_HEADS), _dup_heads(cv, C_KV_HEADS), iw], axis=1), C_Y_COLS)
    iw0 = C_IDX_HEADS * LANES + C_KV_HEADS * LANES
    s2 = jnp.ones((C_Y_COLS,), F32).at[iw0:iw0 + C_IDX_HEADS].set(IW_SCALE)
    return dict(w1=w1.astype(BF16), s1=s1, w2=w2.astype(BF16), s2=s2, wo=w_out.astype(BF16))


def _first_half(a, n_heads):
    return a.reshape(a.shape[0], n_heads, LANES)[:, :, :HEAD_DIM]


def prompt_forward(x, p, w, tables):
    batch, seq, d = x.shape
    n = batch * seq
    tm = min(512, n)
    wp = min(A_WIN_MAX, seq)
    hd = A_HEADS * HEAD_DIM
    xf = x.reshape(n, d)
    a_k, a_v, b_s, c_k, c_v, c_ik = [], [], [], [], [], []
    for l in range(len(w["ffn"])):
        li = l // 2
        if l % 2 == 0:
            ab = w["ab"][li]
            qk = norm_proj(xf, w["g_mix"][l], ab["w1"], ab["s1"], head_norm=True, tm=tm)
            y = norm_proj(xf, w["g_mix"][l], ab["wy"], ab["sy"], head_norm=False, tm=tm)
            oa = dilated_prompt(qk, y, tables["dil"], batch, seq)
            ob, s_fin = gla_prompt(y, ab["w2p"], w["b_gate_b"][li], w["b_out_norm"][li], batch, seq,
                                   tg=min(256, seq))
            xf = out_proj([oa, ob], [ab["wo_a"], ab["wo_b"]], xf, tm=tm)
            a_k.append(qk[:, hd:].reshape(batch, seq, A_HEADS, HEAD_DIM)[:, seq - wp:])
            a_v.append(y[:, :hd].reshape(batch, seq, A_HEADS, HEAD_DIM)[:, seq - wp:])
            b_s.append(s_fin)
        else:
            c = w["c"][li]
            y1 = norm_proj(xf, w["g_mix"][l], c["w1"], c["s1"], head_norm=True, tm=tm)
            y2 = norm_proj(xf, w["g_mix"][l], c["w2"], c["s2"], head_norm=False, tm=tm)
            mask = dsa_select_prompt(y2, y1, batch, seq)
            oc = dsa_attn_prompt(y1, y2, mask, tables["dsa"], batch, seq)
            xf = out_proj([oc], [c["wo"]], xf, tm=tm)
            k0 = C_HEADS * HEAD_DIM
            c_k.append(_first_half(y1[:, k0:k0 + C_KV_HEADS * LANES], C_KV_HEADS).reshape(batch, seq, C_KV_HEADS, HEAD_DIM))
            c_v.append(_first_half(y2[:, k0:k0 + C_KV_HEADS * LANES], C_KV_HEADS).reshape(batch, seq, C_KV_HEADS, HEAD_DIM))
            ik0 = k0 + C_KV_HEADS * LANES
            c_ik.append(y1[:, ik0:ik0 + C_IDX_DIM].reshape(batch, seq, C_IDX_DIM))
        f = w["ffn"][l]
        xf = ffn_ple(xf, p[l].reshape(n, -1), f["gf"], f["w1"], f["w3"], f["w2"], f["gp"], f["wg"], f["wp"],
                     tm=tm, tf=256)
    return (xf.reshape(batch, seq, d), jnp.stack(a_k), jnp.stack(a_v), jnp.stack(b_s),
            jnp.stack(c_k), jnp.stack(c_v), jnp.stack(c_ik))


def dilated_sample_bias(rel_bias, wb):
    delta = wb - jnp.arange(wb)
    mult = _branch_multiplicity(delta)
    bias = rel_bias[_rel_bucket(delta)][:, :A_HEADS].astype(F32).T
    t = jnp.where(mult[None] > 0, bias + jnp.log(jnp.maximum(mult, 1).astype(F32))[None], NEG)
    b0 = rel_bias[_rel_bucket(jnp.zeros((), jnp.int32))][:A_HEADS].astype(F32)
    return t[:, None, :], (b0 + math.log(len(A_BRANCHES)))[:, None, None]


def _dil_sample_kernel(q_ref, kn_ref, vn_ref, kt_ref, vt_ref, t_ref, b0_ref, o_ref):
    q = q_ref[...]
    s_new = jnp.sum(q * kn_ref[...], axis=1, keepdims=True) + b0_ref[...]
    s = jnp.sum(q * kt_ref[...], axis=1, keepdims=True) + t_ref[...]
    m = jnp.maximum(s_new, jnp.max(s, axis=2, keepdims=True))
    p = jnp.exp(s - m)
    p_new = jnp.exp(s_new - m)
    l = p_new + jnp.sum(p, axis=2, keepdims=True)
    acc = p_new * vn_ref[...] + jnp.sum(p * vt_ref[...], axis=2, keepdims=True)
    o_ref[...] = acc / l


def dilated_sample(q_col, kn_col, vn_col, cache_kt, cache_vt, li, table, b0):
    bx, nh, hd, _ = q_col.shape
    wb = cache_kt.shape[-1]
    col = pl.BlockSpec((None, nh, hd, 1), lambda b: (b, 0, 0, 0))
    cache = pl.BlockSpec((None, None, nh, hd, wb), lambda b: (li, b, 0, 0, 0))
    return pl.pallas_call(
        _dil_sample_kernel,
        grid=(bx,),
        in_specs=[col, col, col, cache, cache,
                  pl.BlockSpec(table.shape, lambda b: (0, 0, 0)),
                  pl.BlockSpec(b0.shape, lambda b: (0, 0, 0))],
        out_specs=col,
        out_shape=jax.ShapeDtypeStruct((bx, nh, hd, 1), F32),
        compiler_params=_params("parallel"),
        name="dilated_sample",
    )(q_col, kn_col, vn_col, cache_kt, cache_vt, table, b0)


def _gla_gate_kernel(blr_ref, w2_ref, gb_ref, o_ref):
    z = _dot(blr_ref[...].astype(BF16), w2_ref[...]) + gb_ref[...]
    o_ref[...] = _log_sigmoid(z) / B_GATE_TAU


def gla_gate(y, w2p, gate_b):
    n = y.shape[0]
    nk = B_HEADS * B_DK
    return pl.pallas_call(
        _gla_gate_kernel,
        grid=(1,),
        in_specs=[pl.BlockSpec((n, LANES), lambda i: (0, 16)),
                  pl.BlockSpec((LANES, nk), lambda i: (0, 0)),
                  pl.BlockSpec((1, nk), lambda i: (0, 0))],
        out_specs=pl.BlockSpec((n, nk), lambda i: (0, 0)),
        out_shape=jax.ShapeDtypeStruct((n, nk), F32),
        compiler_params=_params("arbitrary"),
        name="gla_gate",
    )(y, w2p, gate_b.reshape(1, nk))


def _gla_step_kernel(q_ref, k_ref, g_ref, v_ref, bog_ref, on_ref, s_ref, o_ref, sn_ref):
    st = jnp.exp(g_ref[...]) * s_ref[...] + k_ref[...] * v_ref[...]
    sn_ref[...] = st
    o = jnp.sum(q_ref[...] * st, axis=2, keepdims=True)
    gate = bog_ref[...]
    o_ref[...] = _rms(o, on_ref[...]) * (gate * jax.nn.sigmoid(gate))


def gla_step(q, k, g, v, bog, out_norm, state, li, *, tb):
    bx = q.shape[0]
    col = pl.BlockSpec((tb, B_HEADS, B_DK, 1), lambda i: (i, 0, 0, 0))
    rowspec = pl.BlockSpec((tb, B_HEADS, 1, B_DV), lambda i: (i, 0, 0, 0))
    return pl.pallas_call(
        _gla_step_kernel,
        grid=(bx // tb,),
        in_specs=[col, col, col, rowspec, rowspec,
                  pl.BlockSpec((1, B_DV), lambda i: (0, 0)),
                  pl.BlockSpec((None, tb, B_HEADS, B_DK, B_DV), lambda i: (li, i, 0, 0, 0))],
        out_specs=[rowspec, pl.BlockSpec((tb, B_HEADS, B_DK, B_DV), lambda i: (i, 0, 0, 0))],
        out_shape=[jax.ShapeDtypeStruct((bx, B_HEADS, 1, B_DV), F32),
                   jax.ShapeDtypeStruct((bx, B_HEADS, B_DK, B_DV), F32)],
        compiler_params=_params("parallel"),
        name="gla_step",
    )(q, k, g, v, bog, out_norm.reshape(1, B_DV), state)


def _dsa_sample_score_kernel(pt_ref, iq_ref, iw_ref, ikn_ref, *refs):
    pages, o_ref = refs[:-1], refs[-1]
    iq = iq_ref[...].astype(BF16)
    iw = iw_ref[...]
    for j, pg in enumerate(pages):
        s = _dot(iq[:, :C_IDX_DIM], pg[...].astype(BF16))
        o_ref[j:j + 1, :] = jnp.sum(jnp.maximum(s, 0.0) * iw, axis=0, keepdims=True)
    s_new = jnp.sum(iq.astype(F32) * ikn_ref[...].astype(BF16).astype(F32), axis=-1, keepdims=True)
    sc_new = jnp.sum(jnp.maximum(s_new, 0.0) * iw, axis=0, keepdims=True)
    np_ = len(pages)
    o_ref[np_:, :] = jnp.broadcast_to(sc_new, (o_ref.shape[0] - np_, LANES))


def dsa_sample_scores(page_table, iq3, iw3, ik_new3, pool_ik, li, n_rows):
    bx, n_pages = page_table.shape
    page_specs = [pl.BlockSpec((None, None, C_IDX_DIM, PAGE_SIZE),
                               functools.partial(lambda b, pt, j: (li, pt[b, j], 0, 0), j=j))
                  for j in range(n_pages)]
    return pl.pallas_call(
        _dsa_sample_score_kernel,
        grid_spec=pltpu.PrefetchScalarGridSpec(
            num_scalar_prefetch=1,
            grid=(bx,),
            in_specs=[pl.BlockSpec((None, C_IDX_HEADS, LANES), lambda b, pt: (b, 0, 0)),
                      pl.BlockSpec((None, C_IDX_HEADS, 1), lambda b, pt: (b, 0, 0)),
                      pl.BlockSpec((None, 1, LANES), lambda b, pt: (b, 0, 0))] + page_specs,
            out_specs=pl.BlockSpec((None, n_rows, LANES), lambda b, pt: (b, 0, 0)),
        ),
        out_shape=jax.ShapeDtypeStruct((bx, n_rows, LANES), F32),
        compiler_params=_params("parallel"),
        name="dsa_sample_scores",
    )(page_table, iq3, iw3, ik_new3, *([pool_ik] * n_pages))


def _select_rows_kernel(sc_ref, o_ref, key_ref, *, topk, idx_bits, n_keys):
    width = sc_ref.shape[1]
    col = lax.broadcasted_iota(jnp.int32, (1, width), 1)
    key_ref[...] = jnp.where(col < n_keys, _sortable_key(sc_ref[...]), INT_MIN)

    def write(c0, val):
        o_ref[:, pl.ds(c0, SEL_CHUNK)] = val

    _threshold_select(key_ref, width // SEL_CHUNK, topk, idx_bits, n_keys - 1, write)


def dsa_select_sample(scores, n_keys):
    bx, width = scores.shape
    assert width % SEL_CHUNK == 0
    topk = min(C_TOPK_MAX, n_keys // 4)
    return pl.pallas_call(
        functools.partial(_select_rows_kernel, topk=topk, idx_bits=max(1, (width - 1).bit_length()),
                          n_keys=n_keys),
        grid=(1,),
        in_specs=[pl.BlockSpec((bx, width), lambda i: (0, 0))],
        out_specs=pl.BlockSpec((bx, width), lambda i: (0, 0)),
        out_shape=jax.ShapeDtypeStruct((bx, width), F32),
        scratch_shapes=[pltpu.VMEM((bx, width), jnp.int32)],
        compiler_params=_params("arbitrary"),
        name="dsa_select_sample",
    )(scores)


def dsa_sample_bias(rel_bias, past):
    n_pages = past // PAGE_SIZE
    bias = rel_bias[_rel_bucket(past - jnp.arange(past))][:, :C_HEADS].astype(F32)
    b0 = rel_bias[_rel_bucket(jnp.zeros((), jnp.int32))][:C_HEADS].astype(F32)[:, None]
    return bias.reshape(n_pages, PAGE_SIZE, C_HEADS).transpose(0, 2, 1), b0


def _lane_tiling_matrix():
    return jnp.asarray(np.tile(np.eye(HEAD_DIM, dtype=np.float32), (1, C_KV_HEADS)), BF16)


def _dsa_sample_attn_kernel(pt_ref, q_ref, kn_ref, vn_ref, mask_ref, bias_ref, b0_ref, e_ref, *refs):
    n_pages = (len(refs) - 1) // 2
    k_pages, v_pages, o_ref = refs[:n_pages], refs[n_pages:2 * n_pages], refs[-1]
    kvw = C_KV_HEADS * HEAD_DIM
    grp_shift = (C_HEADS // C_KV_HEADS).bit_length() - 1
    own = (jnp.right_shift(lax.broadcasted_iota(jnp.int32, (C_HEADS, kvw), 1), HEAD_DIM.bit_length() - 1)
           == jnp.right_shift(lax.broadcasted_iota(jnp.int32, (C_HEADS, kvw), 0), grp_shift))
    q = q_ref[...]
    q_bd = jnp.where(own, _dot(q.astype(BF16), e_ref[...]), 0.0).astype(BF16)
    s_new = (jnp.sum(q * kn_ref[...], axis=-1, keepdims=True) + b0_ref[...]
             + mask_ref[n_pages:n_pages + 1, 0:1])
    m = s_new
    l = jnp.ones_like(s_new)
    w_new = jnp.ones_like(s_new)
    acc = jnp.zeros((C_HEADS, kvw), F32)
    for j in range(n_pages):
        s = _dot(q_bd, k_pages[j][...].astype(BF16)) + bias_ref[j] + mask_ref[j:j + 1, :]
        m_new = jnp.maximum(m, jnp.max(s, axis=1, keepdims=True))
        alpha = jnp.exp(m - m_new)
        p = jnp.exp(s - m_new)
        l = alpha * l + jnp.sum(p, axis=1, keepdims=True)
        acc = alpha * acc + _dot_nt(p.astype(BF16), v_pages[j][...].astype(BF16))
        w_new = w_new * alpha
        m = m_new
    acc = jnp.where(own, acc, 0.0)
    o = acc[:, :HEAD_DIM]
    for g in range(1, C_KV_HEADS):
        o = o + acc[:, g * HEAD_DIM:(g + 1) * HEAD_DIM]
    o_ref[...] = (o + w_new * vn_ref[...]) / l


def dsa_sample_attn(page_table, q3, k_new16, v_new16, mask3, bias3, b0, pool_kt, pool_vt, li):
    bx, n_pages = page_table.shape
    kvw = C_KV_HEADS * HEAD_DIM
    hspec = pl.BlockSpec((None, C_HEADS, HEAD_DIM), lambda b, pt: (b, 0, 0))
    page_specs = [pl.BlockSpec((None, None, kvw, PAGE_SIZE),
                               functools.partial(lambda b, pt, j: (li, pt[b, j], 0, 0), j=j))
                  for j in range(n_pages)]
    return pl.pallas_call(
        _dsa_sample_attn_kernel,
        grid_spec=pltpu.PrefetchScalarGridSpec(
            num_scalar_prefetch=1,
            grid=(bx,),
            in_specs=[hspec, hspec, hspec,
                      pl.BlockSpec((None,) + mask3.shape[1:], lambda b, pt: (b, 0, 0)),
                      pl.BlockSpec(bias3.shape, lambda b, pt: (0, 0, 0)),
                      pl.BlockSpec(b0.shape, lambda b, pt: (0, 0)),
                      pl.BlockSpec((HEAD_DIM, kvw), lambda b, pt: (0, 0))] + page_specs + page_specs,
            out_specs=hspec,
        ),
        out_shape=jax.ShapeDtypeStruct((bx, C_HEADS, HEAD_DIM), F32),
        compiler_params=_params("parallel"),
        name="dsa_sample_attn",
    )(page_table, q3, k_new16, v_new16, mask3, bias3, b0, _lane_tiling_matrix(),
      *([pool_kt] * n_pages), *([pool_vt] * n_pages))


def sample_forward(x, p, cache_a_k, cache_a_v, state_b, cache_c_k, cache_c_v, cache_c_ik, page_table, w,
                   rel_bias):
    bx, t_len, d = x.shape
    assert t_len == 1
    hd = A_HEADS * HEAD_DIM
    wb = cache_a_k.shape[2]
    n_pages = page_table.shape[1]
    past = n_pages * PAGE_SIZE
    n_rows = -(-(past + 1) // (2 * LANES)) * 2
    dil_bias, dil_b0 = dilated_sample_bias(rel_bias, wb)
    dsa_bias, dsa_b0 = dsa_sample_bias(rel_bias, past)
    cache_a_kt = cache_a_k.transpose(0, 1, 3, 4, 2)
    cache_a_vt = cache_a_v.transpose(0, 1, 3, 4, 2)
    pool_shape = cache_c_k.shape[:2] + (C_KV_HEADS * HEAD_DIM, PAGE_SIZE)
    pool_kt = cache_c_k.transpose(0, 1, 3, 4, 2).reshape(pool_shape)
    pool_vt = cache_c_v.transpose(0, 1, 3, 4, 2).reshape(pool_shape)
    pool_ikt = cache_c_ik.transpose(0, 1, 3, 2)
    grp = C_HEADS // C_KV_HEADS
    xf = x.reshape(bx, d)
    a_k, a_v, b_s, c_k, c_v, c_ik = [], [], [], [], [], []
    for l in range(len(w["ffn"])):
        li = l // 2
        if l % 2 == 0:
            ab = w["ab"][li]
            qk = norm_proj(xf, w["g_mix"][l], ab["w1"], ab["s1"], head_norm=True, tm=bx)
            y = norm_proj(xf, w["g_mix"][l], ab["wy"], ab["sy"], head_norm=False, tm=bx)
            v_new = y[:, :hd].reshape(bx, A_HEADS, HEAD_DIM)
            acol = lambda a: a.reshape(bx, A_HEADS, HEAD_DIM, 1)
            oa = dilated_sample(acol(qk[:, :hd]), acol(qk[:, hd:]), acol(y[:, :hd]), cache_a_kt, cache_a_vt, li,
                                dil_bias, dil_b0)
            g = gla_gate(y, ab["w2p"], w["b_gate_b"][li])
            nk = B_HEADS * B_DK
            colv = lambda a: a.reshape(bx, B_HEADS, B_DK, 1)
            rowv = lambda a: a.reshape(bx, B_HEADS, 1, B_DV)
            ob, s_fin = gla_step(colv(y[:, hd:hd + nk]), colv(y[:, hd + nk:hd + 2 * nk]), colv(g),
                                 rowv(y[:, 2 * hd:2 * hd + B_HEADS * B_DV]),
                                 rowv(y[:, 2 * hd + B_HEADS * B_DV:2 * hd + 2 * B_HEADS * B_DV]),
                                 w["b_out_norm"][li], state_b, li, tb=8)
            xf = out_proj([oa.reshape(bx, hd), ob.reshape(bx, B_HEADS * B_DV)], [ab["wo_a"], ab["wo_b"]], xf, tm=bx)
            a_k.append(qk[:, hd:].reshape(bx, 1, A_HEADS, HEAD_DIM))
            a_v.append(v_new.reshape(bx, 1, A_HEADS, HEAD_DIM))
            b_s.append(s_fin)
        else:
            c = w["c"][li]
            y1 = norm_proj(xf, w["g_mix"][l], c["w1"], c["s1"], head_norm=True, tm=bx)
            y2 = norm_proj(xf, w["g_mix"][l], c["w2"], c["s2"], head_norm=False, tm=bx)
            k0 = C_HEADS * HEAD_DIM
            ik0 = k0 + C_KV_HEADS * LANES
            iw0 = C_IDX_HEADS * LANES + C_KV_HEADS * LANES
            k_new = _first_half(y1[:, k0:ik0], C_KV_HEADS)
            v_new = _first_half(y2[:, k0:ik0], C_KV_HEADS)
            scores = dsa_sample_scores(page_table, y2[:, :k0].reshape(bx, C_IDX_HEADS, LANES),
                                       y2[:, iw0:iw0 + C_IDX_HEADS].reshape(bx, C_IDX_HEADS, 1),
                                       y1[:, ik0:ik0 + LANES].reshape(bx, 1, LANES),
                                       pool_ikt, li, n_rows)
            mask = dsa_select_sample(scores.reshape(bx, n_rows * LANES), past + 1)
            oc = dsa_sample_attn(page_table, y1[:, :k0].reshape(bx, C_HEADS, HEAD_DIM),
                                 jnp.repeat(k_new, grp, axis=1), jnp.repeat(v_new, grp, axis=1),
                                 mask.reshape(bx, n_rows, LANES), dsa_bias, dsa_b0, pool_kt, pool_vt, li)
            xf = out_proj([oc.reshape(bx, k0)], [c["wo"]], xf, tm=bx)
            c_k.append(k_new.reshape(bx, 1, C_KV_HEADS, HEAD_DIM))
            c_v.append(v_new.reshape(bx, 1, C_KV_HEADS, HEAD_DIM))
            c_ik.append(y1[:, ik0:ik0 + C_IDX_DIM].reshape(bx, 1, C_IDX_DIM))
        f = w["ffn"][l]
        xf = ffn_ple(xf, p[l].reshape(bx, -1), f["gf"], f["w1"], f["w3"], f["w2"], f["gp"], f["wg"], f["wp"],
                     tm=bx, tf=256)
    return (xf.reshape(bx, 1, d), jnp.stack(a_k), jnp.stack(a_v), jnp.stack(b_s),
            jnp.stack(c_k), jnp.stack(c_v), jnp.stack(c_ik))


def prep_weights(rel_bias, g_mix, w_in_ab, a_q_norm, a_k_norm, b_gate_w2, b_gate_b, b_out_norm, w_out_ab,
                 w_in_c, c_q_norm, c_k_norm, c_ik_norm, w_out_c, g_ffn, w_ff1, w_ff3, w_ff2, g_ple,
                 w_ple_gate, w_ple_proj):
    w = dict(g_mix=g_mix, b_gate_b=b_gate_b, b_out_norm=b_out_norm)
    w["ab"] = [prep_ab(w_in_ab[i], a_q_norm[i], a_k_norm[i], b_gate_w2[i], w_out_ab[i])
               for i in range(w_in_ab.shape[0])]
    w["c"] = [prep_c(w_in_c[i], c_q_norm[i], c_k_norm[i], c_ik_norm[i], w_out_c[i])
              for i in range(w_in_c.shape[0])]
    w["ffn"] = [dict(gf=g_ffn[l], w1=w_ff1[l].astype(BF16), w3=w_ff3[l].astype(BF16), w2=w_ff2[l].astype(BF16),
                     gp=g_ple[l], wg=w_ple_gate[l].astype(BF16), wp=w_ple_proj[l].astype(BF16))
                for l in range(g_ffn.shape[0])]
    tables = dict(dil=dilated_bias_table(rel_bias), dsa=dsa_bias_table(rel_bias))
    return w, tables


def kernel(x_prompt, x_sample, cache_a_k, cache_a_v, state_b, cache_c_k, cache_c_v, cache_c_ik, page_table,
           p_prompt, p_sample, rel_bias, g_mix, w_in_ab, a_q_norm, a_k_norm, b_gate_w2, b_gate_b, b_out_norm,
           w_out_ab, w_in_c, c_q_norm, c_k_norm, c_ik_norm, w_out_c, g_ffn, w_ff1, w_ff3, w_ff2, g_ple,
           w_ple_gate, w_ple_proj):
    w, tables = prep_weights(rel_bias, g_mix, w_in_ab, a_q_norm, a_k_norm, b_gate_w2, b_gate_b, b_out_norm,
                             w_out_ab, w_in_c, c_q_norm, c_k_norm, c_ik_norm, w_out_c, g_ffn, w_ff1, w_ff3,
                             w_ff2, g_ple, w_ple_gate, w_ple_proj)
    prompt = prompt_forward(x_prompt, p_prompt, w, tables)
    sample = sample_forward(x_sample, p_sample, cache_a_k, cache_a_v, state_b, cache_c_k, cache_c_v,
                            cache_c_ik, page_table, w, rel_bias)
    return (prompt[0], sample[0]) + tuple(prompt[1:]) + tuple(sample[1:])
```

```python
import functools
import math

import jax
import jax.numpy as jnp
import numpy as np
from jax import lax
from jax.experimental import pallas as pl
from jax.experimental.pallas import tpu as pltpu

F32 = jnp.float32
BF16 = jnp.bfloat16

LANES = 128
HEAD_DIM = 64
A_HEADS = 8
A_BRANCHES = ((128, 1), (512, 4), (2048, 16))
A_WIN_MAX = 2048
B_HEADS = 4
B_DK = 64
B_DV = 128
B_GATE_RANK = 16
B_GATE_TAU = 16.0
C_HEADS = 16
C_KV_HEADS = 4
C_IDX_HEADS = 8
C_IDX_DIM = 64
C_TOPK_MAX = 256
N_BUCKETS = 32
REL_MAX_DIST = A_WIN_MAX
Q_BLOCK = 128
PAGE_SIZE = 128
NORM_EPS = 1e-6
NEG = -1e30
VMEM_LIMIT = 56 * 1024 * 1024


def _params(*sem):
    return pltpu.CompilerParams(dimension_semantics=sem, vmem_limit_bytes=VMEM_LIMIT)


def _rms(x, g):
    ms = jnp.mean(x * x, axis=-1, keepdims=True)
    return x * lax.rsqrt(ms + NORM_EPS) * g


def _dot(a, b):
    return jnp.dot(a, b, preferred_element_type=F32)


def _dot_nt(a, b):
    return lax.dot_general(a, b, (((1,), (1,)), ((), ())), preferred_element_type=F32)


def _dot_tn(a, b):
    return lax.dot_general(a, b, (((0,), (0,)), ((), ())), preferred_element_type=F32)


PROJ_CHUNK = 256


def _proj_kernel(x_ref, g_ref, w_ref, cs_ref, p_ref, o_ref, *, head_norm):
    hn = _rms(x_ref[...], g_ref[...]).astype(BF16)
    for c in range(o_ref.shape[1] // PROJ_CHUNK):
        sl = slice(c * PROJ_CHUNK, (c + 1) * PROJ_CHUNK)
        y = _dot(hn, w_ref[:, sl])
        if head_norm:
            ms = _dot((y * y).astype(BF16), p_ref[...])
            y = y * lax.rsqrt(ms + NORM_EPS)
        o_ref[:, sl] = y * cs_ref[:, sl]


def _group_mean_matrix():
    r = np.arange(PROJ_CHUNK) // HEAD_DIM
    return jnp.asarray((r[:, None] == r[None, :]).astype(np.float32) / HEAD_DIM, BF16)


def norm_proj(x, g, w, colscale, *, head_norm, tm):
    n, d = x.shape
    dout = w.shape[1]
    assert n % tm == 0 and dout % PROJ_CHUNK == 0
    return pl.pallas_call(
        functools.partial(_proj_kernel, head_norm=head_norm),
        grid=(n // tm,),
        in_specs=[
            pl.BlockSpec((tm, d), lambda i: (i, 0)),
            pl.BlockSpec((1, d), lambda i: (0, 0)),
            pl.BlockSpec((d, dout), lambda i: (0, 0)),
            pl.BlockSpec((1, dout), lambda i: (0, 0)),
            pl.BlockSpec((PROJ_CHUNK, PROJ_CHUNK), lambda i: (0, 0)),
        ],
        out_specs=pl.BlockSpec((tm, dout), lambda i: (i, 0)),
        out_shape=jax.ShapeDtypeStruct((n, dout), F32),
        compiler_params=_params("parallel"),
        name="norm_proj_hn" if head_norm else "norm_proj",
    )(x, g.reshape(1, d), w, colscale.reshape(1, dout), _group_mean_matrix())


def _out_kernel(*refs, n_in):
    res_ref, o_ref = refs[2 * n_in], refs[2 * n_in + 1]
    acc = res_ref[...]
    for a_ref, w_ref in zip(refs[:n_in], refs[n_in:2 * n_in]):
        acc = acc + _dot(a_ref[...].astype(BF16), w_ref[...])
    o_ref[...] = acc


def out_proj(a_list, w_list, res, *, tm):
    n, d = res.shape
    n_in = len(a_list)
    in_specs = [pl.BlockSpec((tm, a.shape[1]), lambda i: (i, 0)) for a in a_list]
    in_specs += [pl.BlockSpec(w.shape, lambda i: (0, 0)) for w in w_list]
    in_specs += [pl.BlockSpec((tm, d), lambda i: (i, 0))]
    return pl.pallas_call(
        functools.partial(_out_kernel, n_in=n_in),
        grid=(n // tm,),
        in_specs=in_specs,
        out_specs=pl.BlockSpec((tm, d), lambda i: (i, 0)),
        out_shape=jax.ShapeDtypeStruct((n, d), F32),
        compiler_params=_params("parallel"),
        name="out_proj",
    )(*a_list, *w_list, res)


def _ffn_kernel(x_ref, gf_ref, w1_ref, w3_ref, w2_ref, gp_ref, wg_ref, p_ref, wp_ref, o_ref,
                hn_ref, acc_ref):
    f = pl.program_id(1)

    @pl.when(f == 0)
    def _():
        hn_ref[...] = _rms(x_ref[...], gf_ref[...]).astype(BF16)
        acc_ref[...] = jnp.zeros_like(acc_ref)

    hn = hn_ref[...]
    h1 = _dot(hn, w1_ref[...])
    h3 = _dot(hn, w3_ref[...])
    a = h1 * jax.nn.sigmoid(h1) * h3
    acc_ref[...] += _dot(a.astype(BF16), w2_ref[...])

    @pl.when(f == pl.num_programs(1) - 1)
    def _():
        x2 = x_ref[...] + acc_ref[...]
        u = _rms(x2, gp_ref[...]).astype(BF16)
        gate = jax.nn.sigmoid(_dot(u, wg_ref[...]))
        o_ref[...] = x2 + gate * _dot(p_ref[...].astype(BF16), wp_ref[...])


def ffn_ple(x, p, gf, w1, w3, w2, gp, wg, wp, *, tm, tf):
    n, d = x.shape
    dff = w1.shape[1]
    dple = p.shape[1]
    assert n % tm == 0 and dff % tf == 0
    return pl.pallas_call(
        _ffn_kernel,
        grid=(n // tm, dff // tf),
        in_specs=[
            pl.BlockSpec((tm, d), lambda i, f: (i, 0)),
            pl.BlockSpec((1, d), lambda i, f: (0, 0)),
            pl.BlockSpec((d, tf), lambda i, f: (0, f)),
            pl.BlockSpec((d, tf), lambda i, f: (0, f)),
            pl.BlockSpec((tf, d), lambda i, f: (f, 0)),
            pl.BlockSpec((1, d), lambda i, f: (0, 0)),
            pl.BlockSpec((d, d), lambda i, f: (0, 0)),
            pl.BlockSpec((tm, dple), lambda i, f: (i, 0)),
            pl.BlockSpec((dple, d), lambda i, f: (0, 0)),
        ],
        out_specs=pl.BlockSpec((tm, d), lambda i, f: (i, 0)),
        out_shape=jax.ShapeDtypeStruct((n, d), F32),
        scratch_shapes=[pltpu.VMEM((tm, d), BF16), pltpu.VMEM((tm, d), F32)],
        compiler_params=_params("parallel", "arbitrary"),
        name="ffn_ple",
    )(x, gf.reshape(1, d), w1, w3, w2, gp.reshape(1, d), wg, p, wp)


def _rel_bucket(dist):
    n = jnp.maximum(dist, 0)
    exact = N_BUCKETS // 2
    nf = jnp.maximum(n, exact).astype(F32)
    large = exact + (jnp.log(nf / exact) / math.log(REL_MAX_DIST / exact)
                     * (N_BUCKETS - exact)).astype(jnp.int32)
    return jnp.where(n < exact, n, jnp.minimum(large, N_BUCKETS - 1))


def _branch_multiplicity(delta):
    mult = jnp.zeros(delta.shape, jnp.int32)
    for w, d in A_BRANCHES:
        mult = mult + ((delta >= 0) & (delta <= w) & (delta % d == 0)).astype(jnp.int32)
    return mult


DIL_WIN_CHUNKS = A_WIN_MAX // Q_BLOCK + 1


def _toeplitz(value_of_delta, base, width):
    period = width + Q_BLOCK - 1
    x = jnp.concatenate([jnp.arange(width), jnp.arange(-(Q_BLOCK - 1), 0)])
    v = value_of_delta(base - x)
    t = jnp.tile(v, (1, Q_BLOCK))[:, :Q_BLOCK * (period - 1)]
    return t.reshape(v.shape[0], Q_BLOCK, period - 1)[:, :, :width]


def dilated_bias_table(rel_bias):
    def value(delta):
        mult = _branch_multiplicity(delta)
        bias = rel_bias[_rel_bucket(delta)][:, :A_HEADS].astype(F32).T
        return jnp.where(mult[None] > 0, bias + jnp.log(jnp.maximum(mult, 1).astype(F32))[None], NEG)

    t = _toeplitz(value, A_WIN_MAX, DIL_WIN_CHUNKS * Q_BLOCK)
    t = t.reshape(A_HEADS // 2, 2 * Q_BLOCK, DIL_WIN_CHUNKS * Q_BLOCK)
    return jnp.pad(t, ((0, 0), (0, 0), (0, Q_BLOCK)), constant_values=NEG)


DSA_BIAS_TILES = 14


def dsa_bias_table(rel_bias):
    def value(delta):
        return rel_bias[_rel_bucket(delta)][:, :C_HEADS].astype(F32).T

    last = DSA_BIAS_TILES - 1
    t = _toeplitz(value, last * Q_BLOCK, DSA_BIAS_TILES * Q_BLOCK)
    grp = C_HEADS // C_KV_HEADS
    t = t.reshape(C_KV_HEADS, grp, Q_BLOCK, DSA_BIAS_TILES, Q_BLOCK)[:, :, :, ::-1, :]
    return t.transpose(0, 3, 4, 1, 2).reshape(C_KV_HEADS, DSA_BIAS_TILES, Q_BLOCK, grp * Q_BLOCK)


def _lane_lt64():
    return lax.broadcasted_iota(jnp.int32, (1, LANES), 1) < HEAD_DIM


def _online_softmax_step(s, v_bf16, m, l, acc):
    m_new = jnp.maximum(m, jnp.max(s, axis=1, keepdims=True))
    alpha = jnp.exp(m - m_new)
    p = jnp.exp(s - m_new)
    l = alpha * l + jnp.sum(p, axis=1, keepdims=True)
    acc = alpha * acc + _dot(p.astype(BF16), v_bf16)
    return m_new, l, acc


def _dil_kernel(q_ref, k_ref, v_ref, t_ref, o_ref, s_ref, p_ref):
    i = pl.program_id(2)
    nw = s_ref.shape[1] // Q_BLOCK
    last = DIL_WIN_CHUNKS - 1
    lo = _lane_lt64()
    q = q_ref[...]
    q2 = jnp.concatenate([jnp.where(lo, q, 0.0), jnp.where(lo, 0.0, q)], axis=0).astype(BF16)
    w0 = jnp.maximum(i - (nw - 1), 0)
    rows = pl.ds(pl.multiple_of(w0 * Q_BLOCK, Q_BLOCK), nw * Q_BLOCK)
    s_ref[...] = _dot_nt(q2, k_ref[rows, :].astype(BF16))
    m = jnp.full((2 * Q_BLOCK, LANES), NEG, F32)
    for w in range(nw):
        c = last - i + w0 + w
        tcol = pl.multiple_of(jnp.where(c <= last, c, last + 1) * Q_BLOCK, Q_BLOCK)
        cols = slice(w * Q_BLOCK, (w + 1) * Q_BLOCK)
        s = s_ref[:, cols] + t_ref[0, :, pl.ds(tcol, Q_BLOCK)]
        s_ref[:, cols] = s
        m = jnp.maximum(m, s)
    m = jnp.max(m, axis=1, keepdims=True)
    l = jnp.zeros((2 * Q_BLOCK, LANES), F32)
    for w in range(nw):
        cols = slice(w * Q_BLOCK, (w + 1) * Q_BLOCK)
        p = jnp.exp(s_ref[:, cols] - m)
        l = l + p
        p_ref[:, cols] = p.astype(BF16)
    o = _dot(p_ref[...], v_ref[rows, :].astype(BF16)) / jnp.sum(l, axis=1, keepdims=True)
    o_ref[...] = jnp.where(lo, o[:Q_BLOCK], o[Q_BLOCK:])


def dilated_prompt(qk, yv, table, batch, seq):
    n = batch * seq
    nblk = seq // Q_BLOCK
    npair = A_HEADS // 2
    win = min(DIL_WIN_CHUNKS, nblk) * Q_BLOCK
    return pl.pallas_call(
        _dil_kernel,
        grid=(batch, npair, nblk),
        in_specs=[
            pl.BlockSpec((Q_BLOCK, LANES), lambda b, p, i: (b * nblk + i, p)),
            pl.BlockSpec((seq, LANES), lambda b, p, i: (b, npair + p)),
            pl.BlockSpec((seq, LANES), lambda b, p, i: (b, p)),
            pl.BlockSpec((1,) + table.shape[1:], lambda b, p, i: (p, 0, 0)),
        ],
        out_specs=pl.BlockSpec((Q_BLOCK, LANES), lambda b, p, i: (b * nblk + i, p)),
        out_shape=jax.ShapeDtypeStruct((n, A_HEADS * HEAD_DIM), F32),
        scratch_shapes=[pltpu.VMEM((2 * Q_BLOCK, win), F32), pltpu.VMEM((2 * Q_BLOCK, win), BF16)],
        compiler_params=_params("parallel", "parallel", "arbitrary"),
        name="dilated_prompt",
    )(qk, qk, yv, table)


GLA_CHUNK = 64
GLA_SUB = 16
GLA_EXP_CLAMP = 60.0


def _log_sigmoid(z):
    return jnp.minimum(z, 0.0) - jnp.log1p(jnp.exp(-jnp.abs(z)))


def _gla_kernel(bq_ref, bk_ref, bv_ref, bog_ref, blr_ref, w2_ref, gb_ref, on_ref, o_ref, s_ref, st_ref):
    it = pl.program_id(1)

    @pl.when(it == 0)
    def _():
        st_ref[...] = jnp.zeros_like(st_ref)

    lo = _lane_lt64()
    ch = GLA_CHUNK
    row = lax.broadcasted_iota(jnp.int32, (ch, ch), 0)
    col = lax.broadcasted_iota(jnp.int32, (ch, ch), 1)
    causal = row >= col
    ltri = jnp.where(causal, 1.0, 0.0).astype(BF16)
    top_half = lax.broadcasted_iota(jnp.int32, (LANES, LANES), 0) < HEAD_DIM

    def chunk_body(c, carry):
        rows = pl.ds(pl.multiple_of(c * ch, ch), ch)
        z = _dot(blr_ref[rows, :].astype(BF16), w2_ref[...]) + gb_ref[...]
        gg = _log_sigmoid(z) / B_GATE_TAU
        for p in range(B_HEADS // 2):
            sl = slice(p * LANES, (p + 1) * LANES)
            q = bq_ref[rows, sl]
            k = bk_ref[rows, sl]
            g = gg[:, sl]
            g_hi = g.astype(BF16)
            g_lo = (g - g_hi.astype(F32)).astype(BF16)
            cum = _dot(ltri, g_hi) + _dot(ltri, g_lo)
            last = cum[ch - 1:ch, :]
            st = st_ref[p]
            qd = q * jnp.exp(cum)
            q2 = jnp.concatenate([jnp.where(lo, qd, 0.0), jnp.where(lo, 0.0, qd)], axis=0)
            o_inter = _dot(q2.astype(BF16), st.astype(BF16))
            atts = []
            for sb in range(ch // GLA_SUB):
                rs = slice(sb * GLA_SUB, (sb + 1) * GLA_SUB)
                ref_row = cum[sb * GLA_SUB:sb * GLA_SUB + 1, :]
                qs = q[rs] * jnp.exp(cum[rs] - ref_row)
                ks = k * jnp.exp(jnp.minimum(ref_row - cum, GLA_EXP_CLAMP))
                qq = jnp.concatenate([jnp.where(lo, qs, 0.0), jnp.where(lo, 0.0, qs)], axis=0)
                atts.append(_dot_nt(qq.astype(BF16), ks.astype(BF16)))
            kd = (k * jnp.exp(last - cum)).astype(BF16)
            upd = []
            for e in range(2):
                hs = slice((2 * p + e) * LANES, (2 * p + e + 1) * LANES)
                v = bv_ref[rows, hs].astype(BF16)
                att = jnp.concatenate([a[e * GLA_SUB:(e + 1) * GLA_SUB] for a in atts], axis=0)
                att = jnp.where(causal, att, 0.0)
                o = o_inter[e * ch:(e + 1) * ch] + _dot(att.astype(BF16), v)
                og = _rms(o, on_ref[...])
                gate = bog_ref[rows, hs]
                o_ref[rows, hs] = og * (gate * jax.nn.sigmoid(gate))
                upd.append(_dot_tn(kd, v))
            decay = jnp.transpose(jnp.broadcast_to(jnp.exp(last), (LANES, LANES)))
            st_ref[p] = decay * st + jnp.where(top_half, upd[0], upd[1])
        return carry

    lax.fori_loop(0, bq_ref.shape[0] // ch, chunk_body, 0)

    @pl.when(it == pl.num_programs(1) - 1)
    def _():
        for p in range(B_HEADS // 2):
            s_ref[0, 2 * p] = st_ref[p, :HEAD_DIM, :]
            s_ref[0, 2 * p + 1] = st_ref[p, HEAD_DIM:, :]


def gla_prompt(y, w2p, gate_b, out_norm, batch, seq, *, tg):
    n = batch * seq
    nt = seq // tg
    dv = B_HEADS * B_DV
    return pl.pallas_call(
        _gla_kernel,
        grid=(batch, nt),
        in_specs=[
            pl.BlockSpec((tg, 256), lambda b, t: (b * nt + t, 2)),
            pl.BlockSpec((tg, 256), lambda b, t: (b * nt + t, 3)),
            pl.BlockSpec((tg, dv), lambda b, t: (b * nt + t, 2)),
            pl.BlockSpec((tg, dv), lambda b, t: (b * nt + t, 3)),
            pl.BlockSpec((tg, LANES), lambda b, t: (b * nt + t, 16)),
            pl.BlockSpec((LANES, 256), lambda b, t: (0, 0)),
            pl.BlockSpec((1, 256), lambda b, t: (0, 0)),
            pl.BlockSpec((1, B_DV), lambda b, t: (0, 0)),
        ],
        out_specs=[
            pl.BlockSpec((tg, dv), lambda b, t: (b * nt + t, 0)),
            pl.BlockSpec((1, B_HEADS, B_DK, B_DV), lambda b, t: (b, 0, 0, 0)),
        ],
        out_shape=[jax.ShapeDtypeStruct((n, dv), F32),
                   jax.ShapeDtypeStruct((batch, B_HEADS, B_DK, B_DV), F32)],
        scratch_shapes=[pltpu.VMEM((B_HEADS // 2, LANES, LANES), F32)],
        compiler_params=_params("parallel", "arbitrary"),
        name="gla_prompt",
    )(y, y, y, y, y, w2p, gate_b.reshape(1, 256), out_norm.reshape(1, B_DV))


SEL_CHUNK = 256
INT_MIN = -2 ** 31


def _sortable_key(score):
    bits = pltpu.bitcast(score + 0.0, jnp.int32)
    return jnp.where(bits < 0, bits ^ jnp.int32(0x7FFFFFFF), bits)


def _threshold_select(key_ref, n_ch, topk, idx_bits, row_limit, write):
    r = key_ref.shape[0]
    kc = SEL_CHUNK
    lane = lax.broadcasted_iota(jnp.int32, (1, kc), 1)

    def count(hits):
        def body(c, acc):
            c0 = pl.multiple_of(c * kc, kc)
            hit = hits(key_ref[:, pl.ds(c0, kc)], c0 + lane)
            for t in range(kc // LANES):
                acc = acc + hit[:, t * LANES:(t + 1) * LANES]
            return acc
        acc = lax.fori_loop(0, n_ch, body, jnp.zeros((r, LANES), jnp.int32))
        return jnp.sum(acc, axis=1, keepdims=True)

    n_nonneg = count(lambda k, _: jnp.where(k >= 0, 1, 0))
    base = jnp.where(n_nonneg >= topk, 0, INT_MIN).astype(jnp.int32)
    n_ge = jnp.where(n_nonneg >= topk, n_nonneg, n_ch * kc)

    def bit_body(b, carry):
        base, n_ge = carry
        cand = base | jnp.left_shift(jnp.int32(1), 30 - b)
        cnt = count(lambda k, _: jnp.where(k >= cand, 1, 0))
        return jnp.where(cnt >= topk, cand, base), jnp.where(cnt >= topk, cnt, n_ge)

    tau, n_ge = lax.fori_loop(0, 31, bit_body, (base, n_ge))

    def tie_break():
        need = topk - count(lambda k, _: jnp.where(k > tau, 1, 0))

        def idx_body(b, lo):
            cand = lo | jnp.left_shift(jnp.int32(1), idx_bits - 1 - b)
            cnt = count(lambda k, col: jnp.where(k == tau, jnp.where(col < cand, 1, 0), 0))
            return jnp.where(cnt < need, cand, lo)

        return lax.fori_loop(0, idx_bits, idx_body, jnp.zeros((r, 1), jnp.int32))

    last_eq = lax.cond(jnp.max(n_ge) > topk, tie_break,
                       lambda: jnp.full((r, 1), 2 ** idx_bits, jnp.int32))

    def out_body(c, carry):
        c0 = pl.multiple_of(c * kc, kc)
        k = key_ref[:, pl.ds(c0, kc)]
        col = c0 + lane
        val = jnp.where(k > tau, 0.0, jnp.where(k == tau, jnp.where(col <= last_eq, 0.0, NEG), NEG))
        write(c0, jnp.where(col <= row_limit, val, NEG))
        return carry

    lax.fori_loop(0, n_ch, out_body, 0)


def _threshold_select_t(key_ref, n_ch, topk, idx_bits, t_idx, write):
    nq = key_ref.shape[1]
    kc = SEL_CHUNK
    rowi = lax.broadcasted_iota(jnp.int32, (kc, 1), 0)

    def count(hits):
        def body(c, acc):
            c0 = pl.multiple_of(c * kc, kc)
            hit = hits(key_ref[pl.ds(c0, kc), :], c0 + rowi)
            return acc + jnp.sum(hit.reshape(kc // 8, 8, nq), axis=0)
        acc = lax.fori_loop(0, n_ch, body, jnp.zeros((8, nq), jnp.int32))
        return jnp.sum(acc, axis=0, keepdims=True)

    n_nonneg = count(lambda k, _: jnp.where(k >= 0, 1, 0))
    base = jnp.where(n_nonneg >= topk, 0, INT_MIN).astype(jnp.int32)
    n_ge = jnp.where(n_nonneg >= topk, n_nonneg, n_ch * kc)

    def bit_body(b, carry):
        base, n_ge = carry
        cand = base | jnp.left_shift(jnp.int32(1), 30 - b)
        cnt = count(lambda k, _: jnp.where(k >= cand, 1, 0))
        return jnp.where(cnt >= topk, cand, base), jnp.where(cnt >= topk, cnt, n_ge)

    tau, n_ge = lax.fori_loop(0, 31, bit_body, (base, n_ge))

    def tie_break():
        need = topk - count(lambda k, _: jnp.where(k > tau, 1, 0))

        def idx_body(b, lo):
            cand = lo | jnp.left_shift(jnp.int32(1), idx_bits - 1 - b)
            cnt = count(lambda k, s: jnp.where(k == tau, jnp.where(s < cand, 1, 0), 0))
            return jnp.where(cnt < need, cand, lo)

        return lax.fori_loop(0, idx_bits, idx_body, jnp.zeros((1, nq), jnp.int32))

    last_eq = lax.cond(jnp.max(n_ge) > topk, tie_break,
                       lambda: jnp.full((1, nq), 2 ** idx_bits, jnp.int32))

    def out_body(c, carry):
        c0 = pl.multiple_of(c * kc, kc)
        k = key_ref[pl.ds(c0, kc), :]
        s = c0 + rowi
        val = jnp.where(k > tau, 0.0, jnp.where(k == tau, jnp.where(s <= last_eq, 0.0, NEG), NEG))
        write(c0, jnp.where(s <= t_idx, val, NEG))
        return carry

    lax.fori_loop(0, n_ch, out_body, 0)


SEL_QUERIES = 256


def _select_kernel(iq_ref, iwt_ref, ik_ref, o_ref, key_ref, qs_ref, *, topk, idx_bits):
    qb = pl.program_id(1)
    kc = SEL_CHUNK
    nq = SEL_QUERIES
    n_ch = (qb * nq + nq + kc - 1) // kc
    t_idx = qb * nq + lax.broadcasted_iota(jnp.int32, (1, nq), 1)
    rowi = lax.broadcasted_iota(jnp.int32, (kc, 1), 0)
    iwt = iwt_ref[...]
    for h in range(C_IDX_HEADS):
        qs_ref[h] = iq_ref[:, h * LANES:(h + 1) * LANES].astype(BF16)

    def score_body(c, carry):
        c0 = pl.multiple_of(c * kc, kc)
        ik = ik_ref[pl.ds(c0, kc), :].astype(BF16)
        sc = jnp.zeros((kc, nq), F32)
        for h in range(C_IDX_HEADS):
            sc = sc + jnp.maximum(_dot_nt(ik, qs_ref[h]), 0.0) * iwt[h:h + 1, :]
        key_ref[pl.ds(c0, kc), :] = jnp.where(c0 + rowi <= t_idx, _sortable_key(sc), INT_MIN)
        return carry

    lax.fori_loop(0, n_ch, score_body, 0)
    o_ref[...] = jnp.full(o_ref.shape, NEG, o_ref.dtype)

    def write(c0, val):
        o_ref[pl.ds(c0, kc), :] = val.astype(o_ref.dtype)

    _threshold_select_t(key_ref, n_ch, topk, idx_bits, t_idx, write)


def dsa_select_prompt(y2, iwt, y1, batch, seq):
    assert seq % SEL_CHUNK == 0 and seq % SEL_QUERIES == 0
    nqb = seq // SEL_QUERIES
    topk = min(C_TOPK_MAX, seq // 4)
    return pl.pallas_call(
        functools.partial(_select_kernel, topk=topk, idx_bits=max(1, (seq - 1).bit_length())),
        grid=(batch, nqb),
        in_specs=[
            pl.BlockSpec((SEL_QUERIES, C_IDX_HEADS * LANES), lambda b, i: (b * nqb + i, 0)),
            pl.BlockSpec((C_IDX_HEADS, SEL_QUERIES), lambda b, i: (0, b * nqb + i)),
            pl.BlockSpec((seq, LANES), lambda b, i: (b, 10)),
        ],
        out_specs=pl.BlockSpec((None, seq, SEL_QUERIES), lambda b, i: (b, 0, i)),
        out_shape=jax.ShapeDtypeStruct((batch, seq, seq), BF16),
        scratch_shapes=[pltpu.VMEM((seq, SEL_QUERIES), jnp.int32),
                        pltpu.VMEM((C_IDX_HEADS, SEL_QUERIES, LANES), BF16)],
        compiler_params=_params("parallel", "arbitrary"),
        name="dsa_select_prompt",
    )(y2, iwt, y1)


def _dsa_attn_kernel(q_ref, k_ref, vt_ref, mask_ref, tb_ref, o_ref):
    g = pl.program_id(1)
    i = pl.program_id(2)
    kc = SEL_CHUNK
    grp = C_HEADS // C_KV_HEADS
    lo = _lane_lt64()
    first = (g % 2) == 0
    lane_half = jnp.right_shift(lax.broadcasted_iota(jnp.int32, (1, LANES), 1), HEAD_DIM.bit_length() - 1)
    own_half = lane_half == g % 2
    nsub = q_ref.shape[0] // Q_BLOCK
    cols = grp * Q_BLOCK
    parts = []
    for u in range(nsub):
        for j in range(grp // 2):
            qp = q_ref[u * Q_BLOCK:(u + 1) * Q_BLOCK, j * LANES:(j + 1) * LANES]
            qr = pltpu.roll(qp, HEAD_DIM, 1)
            parts.append(jnp.where(own_half, jnp.where(first, qp, qr), 0.0))
            parts.append(jnp.where(own_half, jnp.where(first, qr, qp), 0.0))
    q4 = jnp.concatenate(parts, axis=0).astype(BF16)

    def body(c, carry):
        m, l, acc = carry
        c0 = pl.multiple_of(c * kc, kc)
        st = _dot_nt(k_ref[pl.ds(c0, kc), :].astype(BF16), q4)
        rows = []
        for hf in range(kc // Q_BLOCK):
            mk = mask_ref[pl.ds(c0 + hf * Q_BLOCK, Q_BLOCK), :].astype(F32)
            blocks = []
            for u in range(nsub):
                e = jnp.clip(i * nsub + u - (c * (kc // Q_BLOCK) + hf), 0, DSA_BIAS_TILES - 1)
                blocks.append(st[hf * Q_BLOCK:(hf + 1) * Q_BLOCK, u * cols:(u + 1) * cols] + tb_ref[0, e]
                              + jnp.concatenate([mk[:, u * Q_BLOCK:(u + 1) * Q_BLOCK]] * grp, axis=1))
            rows.append(jnp.concatenate(blocks, axis=1))
        s = jnp.concatenate(rows, axis=0)
        m_new = jnp.maximum(m, jnp.max(s, axis=0, keepdims=True))
        alpha = jnp.exp(m - m_new)
        p = jnp.exp(s - m_new)
        l = alpha * l + jnp.sum(p, axis=0, keepdims=True)
        acc = alpha * acc + _dot(vt_ref[:, pl.ds(c0, kc)].astype(BF16), p.astype(BF16))
        return m_new, l, acc

    init = (jnp.full((1, nsub * cols), NEG, F32), jnp.zeros((1, nsub * cols), F32),
            jnp.zeros((LANES, nsub * cols), F32))
    n_ch = (i * nsub * Q_BLOCK + nsub * Q_BLOCK + kc - 1) // kc
    _, l, acc = lax.fori_loop(0, n_ch, body, init)
    ot = acc / l
    for u in range(nsub):
        heads = [jnp.transpose(ot[:, (u * grp + h) * Q_BLOCK:(u * grp + h + 1) * Q_BLOCK]) for h in range(grp)]
        for j in range(grp // 2):
            a, b = heads[2 * j], heads[2 * j + 1]
            left = jnp.where(first, a, pltpu.roll(a, HEAD_DIM, 1))
            right = jnp.where(first, pltpu.roll(b, HEAD_DIM, 1), b)
            o_ref[u * Q_BLOCK:(u + 1) * Q_BLOCK, j * LANES:(j + 1) * LANES] = jnp.where(lo, left, right)


DSA_ATTN_QUERIES = 256


def dsa_attn_prompt(y1, vt, mask_t, table, batch, seq):
    n = batch * seq
    nq = DSA_ATTN_QUERIES
    assert seq % nq == 0
    nblk = seq // nq
    qw = (C_HEADS // C_KV_HEADS) * HEAD_DIM
    return pl.pallas_call(
        _dsa_attn_kernel,
        grid=(batch, C_KV_HEADS, nblk),
        in_specs=[
            pl.BlockSpec((nq, qw), lambda b, g, i: (b * nblk + i, g)),
            pl.BlockSpec((seq, LANES), lambda b, g, i: (b, 8 + g // 2)),
            pl.BlockSpec((None, LANES, seq), lambda b, g, i: (b, g // 2, 0)),
            pl.BlockSpec((None, seq, nq), lambda b, g, i: (b, 0, i)),
            pl.BlockSpec((1,) + table.shape[1:], lambda b, g, i: (g, 0, 0, 0)),
        ],
        out_specs=pl.BlockSpec((nq, qw), lambda b, g, i: (b * nblk + i, g)),
        out_shape=jax.ShapeDtypeStruct((n, C_HEADS * HEAD_DIM), F32),
        compiler_params=_params("parallel", "parallel", "arbitrary"),
        name="dsa_attn_prompt",
    )(y1, y1, vt, mask_t, table)


IW_SCALE = (C_IDX_HEADS ** -0.5) * (C_IDX_DIM ** -0.5)
Q_SCALE = HEAD_DIM ** -0.5
AB_Y_COLS = 2304
C_Y_COLS = 1536
C_KV_COL = C_HEADS * HEAD_DIM
C_IK_COL = C_KV_COL + C_KV_HEADS * HEAD_DIM
C_IW_COL = C_IK_COL


def _pad_heads(w, n_heads):
    d = w.shape[0]
    w = w.reshape(d, n_heads, HEAD_DIM)
    return jnp.pad(w, ((0, 0), (0, 0), (0, HEAD_DIM))).reshape(d, n_heads * LANES)


def _pad_cols(w, total):
    return jnp.pad(w, ((0, 0), (0, total - w.shape[1])))


def prep_ab(w_in, a_q_norm, a_k_norm, w2, w_out):
    hd = A_HEADS * HEAD_DIM
    w1 = w_in[:, :2 * hd]
    s1 = jnp.concatenate([jnp.tile(a_q_norm, A_HEADS) * Q_SCALE, jnp.tile(a_k_norm, A_HEADS)])
    nbq = B_HEADS * B_DK
    blr0 = 3 * hd + 2 * nbq + B_HEADS * B_DV
    wy = jnp.concatenate([w_in[:, 2 * hd:blr0], w_in[:, blr0 + B_GATE_RANK:], w_in[:, blr0:blr0 + B_GATE_RANK]],
                         axis=1)
    wy = _pad_cols(wy, AB_Y_COLS)
    sy = jnp.ones((AB_Y_COLS,), F32).at[hd:hd + nbq].set(Q_SCALE)
    w2p = jnp.pad(w2, ((0, LANES - B_GATE_RANK), (0, 0)))
    return dict(w1=w1.astype(BF16), s1=s1, wy=wy.astype(BF16), sy=sy, w2p=w2p.astype(BF16),
                wo_a=w_out[:hd].astype(BF16), wo_b=w_out[hd:].astype(BF16))


def prep_c(w_in, c_q_norm, c_k_norm, c_ik_norm, w_out):
    o = np.cumsum((0, C_HEADS * HEAD_DIM, C_KV_HEADS * HEAD_DIM, C_KV_HEADS * HEAD_DIM,
                   C_IDX_HEADS * C_IDX_DIM, C_IDX_DIM, C_IDX_HEADS))
    cq, ck, cv, iq, ik, iw = (w_in[:, o[t]:o[t + 1]] for t in range(6))
    w1 = _pad_cols(jnp.concatenate([cq, ck, ik], axis=1), C_Y_COLS)
    s1 = jnp.concatenate([jnp.tile(c_q_norm, C_HEADS) * Q_SCALE, jnp.tile(c_k_norm, C_KV_HEADS), c_ik_norm])
    s1 = jnp.pad(s1, (0, C_Y_COLS - s1.shape[0]))
    w2 = _pad_cols(jnp.concatenate([_pad_heads(iq, C_IDX_HEADS), cv, iw], axis=1), C_Y_COLS)
    s2 = jnp.ones((C_Y_COLS,), F32).at[C_IW_COL:C_IW_COL + C_IDX_HEADS].set(IW_SCALE)
    return dict(w1=w1.astype(BF16), s1=s1, w2=w2.astype(BF16), s2=s2, wo=w_out.astype(BF16))


def prompt_forward(x, p, w, tables):
    batch, seq, d = x.shape
    n = batch * seq
    tm = min(512, n)
    wp = min(A_WIN_MAX, seq)
    hd = A_HEADS * HEAD_DIM
    xf = x.reshape(n, d)
    a_k, a_v, b_s, c_k, c_v, c_ik = [], [], [], [], [], []
    for l in range(len(w["ffn"])):
        li = l // 2
        if l % 2 == 0:
            ab = w["ab"][li]
            qk = norm_proj(xf, w["g_mix"][l], ab["w1"], ab["s1"], head_norm=True, tm=tm)
            y = norm_proj(xf, w["g_mix"][l], ab["wy"], ab["sy"], head_norm=False, tm=tm)
            oa = dilated_prompt(qk, y, tables["dil"], batch, seq)
            ob, s_fin = gla_prompt(y, ab["w2p"], w["b_gate_b"][li], w["b_out_norm"][li], batch, seq,
                                   tg=min(256, seq))
            xf = out_proj([oa, ob], [ab["wo_a"], ab["wo_b"]], xf, tm=tm)
            a_k.append(qk[:, hd:].reshape(batch, seq, A_HEADS, HEAD_DIM)[:, seq - wp:])
            a_v.append(y[:, :hd].reshape(batch, seq, A_HEADS, HEAD_DIM)[:, seq - wp:])
            b_s.append(s_fin)
        else:
            c = w["c"][li]
            y1 = norm_proj(xf, w["g_mix"][l], c["w1"], c["s1"], head_norm=True, tm=tm)
            y2 = norm_proj(xf, w["g_mix"][l], c["w2"], c["s2"], head_norm=False, tm=tm)
            cv = y2[:, C_KV_COL:C_IK_COL]
            mask_t = dsa_select_prompt(y2, y2[:, C_IW_COL:C_IW_COL + C_IDX_HEADS].T, y1, batch, seq)
            oc = dsa_attn_prompt(y1, cv.reshape(batch, seq, -1).transpose(0, 2, 1), mask_t, tables["dsa"],
                                 batch, seq)
            xf = out_proj([oc], [c["wo"]], xf, tm=tm)
            c_k.append(y1[:, C_KV_COL:C_IK_COL].reshape(batch, seq, C_KV_HEADS, HEAD_DIM))
            c_v.append(cv.reshape(batch, seq, C_KV_HEADS, HEAD_DIM))
            c_ik.append(y1[:, C_IK_COL:C_IK_COL + C_IDX_DIM].reshape(batch, seq, C_IDX_DIM))
        f = w["ffn"][l]
        xf = ffn_ple(xf, p[l].reshape(n, -1), f["gf"], f["w1"], f["w3"], f["w2"], f["gp"], f["wg"], f["wp"],
                     tm=tm, tf=256)
    return (xf.reshape(batch, seq, d), jnp.stack(a_k), jnp.stack(a_v), jnp.stack(b_s),
            jnp.stack(c_k), jnp.stack(c_v), jnp.stack(c_ik))


def dilated_sample_bias(rel_bias, wb):
    delta = wb - jnp.arange(wb)
    mult = _branch_multiplicity(delta)
    bias = rel_bias[_rel_bucket(delta)][:, :A_HEADS].astype(F32).T
    t = jnp.where(mult[None] > 0, bias + jnp.log(jnp.maximum(mult, 1).astype(F32))[None], NEG)
    b0 = rel_bias[_rel_bucket(jnp.zeros((), jnp.int32))][:A_HEADS].astype(F32)
    return t[:, None, :], (b0 + math.log(len(A_BRANCHES)))[:, None, None]


def _dil_sample_kernel(q_ref, kn_ref, vn_ref, kt_ref, vt_ref, t_ref, b0_ref, o_ref):
    q = q_ref[...]
    s_new = jnp.sum(q * kn_ref[...], axis=1, keepdims=True) + b0_ref[...]
    s = jnp.sum(q * kt_ref[...], axis=1, keepdims=True) + t_ref[...]
    m = jnp.maximum(s_new, jnp.max(s, axis=2, keepdims=True))
    p = jnp.exp(s - m)
    p_new = jnp.exp(s_new - m)
    l = p_new + jnp.sum(p, axis=2, keepdims=True)
    acc = p_new * vn_ref[...] + jnp.sum(p * vt_ref[...], axis=2, keepdims=True)
    o_ref[...] = acc / l


def dilated_sample(q_col, kn_col, vn_col, cache_kt, cache_vt, li, table, b0):
    bx, nh, hd, _ = q_col.shape
    wb = cache_kt.shape[-1]
    col = pl.BlockSpec((None, nh, hd, 1), lambda b: (b, 0, 0, 0))
    cache = pl.BlockSpec((None, None, nh, hd, wb), lambda b: (li, b, 0, 0, 0))
    return pl.pallas_call(
        _dil_sample_kernel,
        grid=(bx,),
        in_specs=[col, col, col, cache, cache,
                  pl.BlockSpec(table.shape, lambda b: (0, 0, 0)),
                  pl.BlockSpec(b0.shape, lambda b: (0, 0, 0))],
        out_specs=col,
        out_shape=jax.ShapeDtypeStruct((bx, nh, hd, 1), F32),
        compiler_params=_params("parallel"),
        name="dilated_sample",
    )(q_col, kn_col, vn_col, cache_kt, cache_vt, table, b0)


def _gla_gate_kernel(blr_ref, w2_ref, gb_ref, o_ref):
    z = _dot(blr_ref[...].astype(BF16), w2_ref[...]) + gb_ref[...]
    o_ref[...] = _log_sigmoid(z) / B_GATE_TAU


def gla_gate(y, w2p, gate_b):
    n = y.shape[0]
    nk = B_HEADS * B_DK
    return pl.pallas_call(
        _gla_gate_kernel,
        grid=(1,),
        in_specs=[pl.BlockSpec((n, LANES), lambda i: (0, 16)),
                  pl.BlockSpec((LANES, nk), lambda i: (0, 0)),
                  pl.BlockSpec((1, nk), lambda i: (0, 0))],
        out_specs=pl.BlockSpec((n, nk), lambda i: (0, 0)),
        out_shape=jax.ShapeDtypeStruct((n, nk), F32),
        compiler_params=_params("arbitrary"),
        name="gla_gate",
    )(y, w2p, gate_b.reshape(1, nk))


def _gla_step_kernel(q_ref, k_ref, g_ref, v_ref, bog_ref, on_ref, s_ref, o_ref, sn_ref):
    st = jnp.exp(g_ref[...]) * s_ref[...] + k_ref[...] * v_ref[...]
    sn_ref[...] = st
    o = jnp.sum(q_ref[...] * st, axis=2, keepdims=True)
    gate = bog_ref[...]
    o_ref[...] = _rms(o, on_ref[...]) * (gate * jax.nn.sigmoid(gate))


def gla_step(q, k, g, v, bog, out_norm, state, li, *, tb):
    bx = q.shape[0]
    col = pl.BlockSpec((tb, B_HEADS, B_DK, 1), lambda i: (i, 0, 0, 0))
    rowspec = pl.BlockSpec((tb, B_HEADS, 1, B_DV), lambda i: (i, 0, 0, 0))
    return pl.pallas_call(
        _gla_step_kernel,
        grid=(bx // tb,),
        in_specs=[col, col, col, rowspec, rowspec,
                  pl.BlockSpec((1, B_DV), lambda i: (0, 0)),
                  pl.BlockSpec((None, tb, B_HEADS, B_DK, B_DV), lambda i: (li, i, 0, 0, 0))],
        out_specs=[rowspec, pl.BlockSpec((tb, B_HEADS, B_DK, B_DV), lambda i: (i, 0, 0, 0))],
        out_shape=[jax.ShapeDtypeStruct((bx, B_HEADS, 1, B_DV), F32),
                   jax.ShapeDtypeStruct((bx, B_HEADS, B_DK, B_DV), F32)],
        compiler_params=_params("parallel"),
        name="gla_step",
    )(q, k, g, v, bog, out_norm.reshape(1, B_DV), state)


def _dsa_sample_score_kernel(pt_ref, iq_ref, iw_ref, ikn_ref, *refs):
    pages, o_ref = refs[:-1], refs[-1]
    iq = iq_ref[...].astype(BF16)
    iw = iw_ref[...]
    for j, pg in enumerate(pages):
        s = _dot(iq[:, :C_IDX_DIM], pg[...].astype(BF16))
        o_ref[j:j + 1, :] = jnp.sum(jnp.maximum(s, 0.0) * iw, axis=0, keepdims=True)
    s_new = jnp.sum(iq.astype(F32) * ikn_ref[...].astype(BF16).astype(F32), axis=-1, keepdims=True)
    sc_new = jnp.sum(jnp.maximum(s_new, 0.0) * iw, axis=0, keepdims=True)
    np_ = len(pages)
    o_ref[np_:, :] = jnp.broadcast_to(sc_new, (o_ref.shape[0] - np_, LANES))


def dsa_sample_scores(page_table, iq3, iw3, ik_new3, pool_ik, li, n_rows):
    bx, n_pages = page_table.shape
    page_specs = [pl.BlockSpec((None, None, C_IDX_DIM, PAGE_SIZE),
                               functools.partial(lambda b, pt, j: (li, pt[b, j], 0, 0), j=j))
                  for j in range(n_pages)]
    return pl.pallas_call(
        _dsa_sample_score_kernel,
        grid_spec=pltpu.PrefetchScalarGridSpec(
            num_scalar_prefetch=1,
            grid=(bx,),
            in_specs=[pl.BlockSpec((None, C_IDX_HEADS, LANES), lambda b, pt: (b, 0, 0)),
                      pl.BlockSpec((None, C_IDX_HEADS, 1), lambda b, pt: (b, 0, 0)),
                      pl.BlockSpec((None, 1, LANES), lambda b, pt: (b, 0, 0))] + page_specs,
            out_specs=pl.BlockSpec((None, n_rows, LANES), lambda b, pt: (b, 0, 0)),
        ),
        out_shape=jax.ShapeDtypeStruct((bx, n_rows, LANES), F32),
        compiler_params=_params("parallel"),
        name="dsa_sample_scores",
    )(page_table, iq3, iw3, ik_new3, *([pool_ik] * n_pages))


def _select_rows_kernel(sc_ref, o_ref, key_ref, *, topk, idx_bits, n_keys):
    width = sc_ref.shape[1]
    col = lax.broadcasted_iota(jnp.int32, (1, width), 1)
    key_ref[...] = jnp.where(col < n_keys, _sortable_key(sc_ref[...]), INT_MIN)

    def write(c0, val):
        o_ref[:, pl.ds(c0, SEL_CHUNK)] = val

    _threshold_select(key_ref, width // SEL_CHUNK, topk, idx_bits, n_keys - 1, write)


def dsa_select_sample(scores, n_keys):
    bx, width = scores.shape
    assert width % SEL_CHUNK == 0
    topk = min(C_TOPK_MAX, n_keys // 4)
    return pl.pallas_call(
        functools.partial(_select_rows_kernel, topk=topk, idx_bits=max(1, (width - 1).bit_length()),
                          n_keys=n_keys),
        grid=(1,),
        in_specs=[pl.BlockSpec((bx, width), lambda i: (0, 0))],
        out_specs=pl.BlockSpec((bx, width), lambda i: (0, 0)),
        out_shape=jax.ShapeDtypeStruct((bx, width), F32),
        scratch_shapes=[pltpu.VMEM((bx, width), jnp.int32)],
        compiler_params=_params("arbitrary"),
        name="dsa_select_sample",
    )(scores)


def dsa_sample_bias(rel_bias, past):
    n_pages = past // PAGE_SIZE
    bias = rel_bias[_rel_bucket(past - jnp.arange(past))][:, :C_HEADS].astype(F32)
    b0 = rel_bias[_rel_bucket(jnp.zeros((), jnp.int32))][:C_HEADS].astype(F32)[:, None]
    return bias.reshape(n_pages, PAGE_SIZE, C_HEADS).transpose(0, 2, 1), b0


def _lane_tiling_matrix():
    return jnp.asarray(np.tile(np.eye(HEAD_DIM, dtype=np.float32), (1, C_KV_HEADS)), BF16)


def _dsa_sample_attn_kernel(pt_ref, q_ref, kn_ref, vn_ref, mask_ref, bias_ref, b0_ref, e_ref, *refs):
    n_pages = (len(refs) - 1) // 2
    k_pages, v_pages, o_ref = refs[:n_pages], refs[n_pages:2 * n_pages], refs[-1]
    kvw = C_KV_HEADS * HEAD_DIM
    grp_shift = (C_HEADS // C_KV_HEADS).bit_length() - 1
    own = (jnp.right_shift(lax.broadcasted_iota(jnp.int32, (C_HEADS, kvw), 1), HEAD_DIM.bit_length() - 1)
           == jnp.right_shift(lax.broadcasted_iota(jnp.int32, (C_HEADS, kvw), 0), grp_shift))
    q = q_ref[...]
    q_bd = jnp.where(own, _dot(q.astype(BF16), e_ref[...]), 0.0).astype(BF16)
    s_new = (jnp.sum(q * kn_ref[...], axis=-1, keepdims=True) + b0_ref[...]
             + mask_ref[n_pages:n_pages + 1, 0:1])
    m = s_new
    l = jnp.ones_like(s_new)
    w_new = jnp.ones_like(s_new)
    acc = jnp.zeros((C_HEADS, kvw), F32)
    for j in range(n_pages):
        s = _dot(q_bd, k_pages[j][...].astype(BF16)) + bias_ref[j] + mask_ref[j:j + 1, :]
        m_new = jnp.maximum(m, jnp.max(s, axis=1, keepdims=True))
        alpha = jnp.exp(m - m_new)
        p = jnp.exp(s - m_new)
        l = alpha * l + jnp.sum(p, axis=1, keepdims=True)
        acc = alpha * acc + _dot_nt(p.astype(BF16), v_pages[j][...].astype(BF16))
        w_new = w_new * alpha
        m = m_new
    acc = jnp.where(own, acc, 0.0)
    o = acc[:, :HEAD_DIM]
    for g in range(1, C_KV_HEADS):
        o = o + acc[:, g * HEAD_DIM:(g + 1) * HEAD_DIM]
    o_ref[...] = (o + w_new * vn_ref[...]) / l


def dsa_sample_attn(page_table, q3, k_new16, v_new16, mask3, bias3, b0, pool_kt, pool_vt, li):
    bx, n_pages = page_table.shape
    kvw = C_KV_HEADS * HEAD_DIM
    hspec = pl.BlockSpec((None, C_HEADS, HEAD_DIM), lambda b, pt: (b, 0, 0))
    page_specs = [pl.BlockSpec((None, None, kvw, PAGE_SIZE),
                               functools.partial(lambda b, pt, j: (li, pt[b, j], 0, 0), j=j))
                  for j in range(n_pages)]
    return pl.pallas_call(
        _dsa_sample_attn_kernel,
        grid_spec=pltpu.PrefetchScalarGridSpec(
            num_scalar_prefetch=1,
            grid=(bx,),
            in_specs=[hspec, hspec, hspec,
                      pl.BlockSpec((None,) + mask3.shape[1:], lambda b, pt: (b, 0, 0)),
                      pl.BlockSpec(bias3.shape, lambda b, pt: (0, 0, 0)),
                      pl.BlockSpec(b0.shape, lambda b, pt: (0, 0)),
                      pl.BlockSpec((HEAD_DIM, kvw), lambda b, pt: (0, 0))] + page_specs + page_specs,
            out_specs=hspec,
        ),
        out_shape=jax.ShapeDtypeStruct((bx, C_HEADS, HEAD_DIM), F32),
        compiler_params=_params("parallel"),
        name="dsa_sample_attn",
    )(page_table, q3, k_new16, v_new16, mask3, bias3, b0, _lane_tiling_matrix(),
      *([pool_kt] * n_pages), *([pool_vt] * n_pages))


def sample_forward(x, p, cache_a_k, cache_a_v, state_b, cache_c_k, cache_c_v, cache_c_ik, page_table, w,
                   rel_bias):
    bx, t_len, d = x.shape
    assert t_len == 1
    hd = A_HEADS * HEAD_DIM
    wb = cache_a_k.shape[2]
    n_pages = page_table.shape[1]
    past = n_pages * PAGE_SIZE
    n_rows = -(-(past + 1) // (2 * LANES)) * 2
    dil_bias, dil_b0 = dilated_sample_bias(rel_bias, wb)
    dsa_bias, dsa_b0 = dsa_sample_bias(rel_bias, past)
    cache_a_kt = cache_a_k.transpose(0, 1, 3, 4, 2)
    cache_a_vt = cache_a_v.transpose(0, 1, 3, 4, 2)
    pool_shape = cache_c_k.shape[:2] + (C_KV_HEADS * HEAD_DIM, PAGE_SIZE)
    pool_kt = cache_c_k.transpose(0, 1, 3, 4, 2).reshape(pool_shape)
    pool_vt = cache_c_v.transpose(0, 1, 3, 4, 2).reshape(pool_shape)
    pool_ikt = cache_c_ik.transpose(0, 1, 3, 2)
    grp = C_HEADS // C_KV_HEADS
    xf = x.reshape(bx, d)
    a_k, a_v, b_s, c_k, c_v, c_ik = [], [], [], [], [], []
    for l in range(len(w["ffn"])):
        li = l // 2
        if l % 2 == 0:
            ab = w["ab"][li]
            qk = norm_proj(xf, w["g_mix"][l], ab["w1"], ab["s1"], head_norm=True, tm=bx)
            y = norm_proj(xf, w["g_mix"][l], ab["wy"], ab["sy"], head_norm=False, tm=bx)
            v_new = y[:, :hd].reshape(bx, A_HEADS, HEAD_DIM)
            acol = lambda a: a.reshape(bx, A_HEADS, HEAD_DIM, 1)
            oa = dilated_sample(acol(qk[:, :hd]), acol(qk[:, hd:]), acol(y[:, :hd]), cache_a_kt, cache_a_vt, li,
                                dil_bias, dil_b0)
            g = gla_gate(y, ab["w2p"], w["b_gate_b"][li])
            nk = B_HEADS * B_DK
            colv = lambda a: a.reshape(bx, B_HEADS, B_DK, 1)
            rowv = lambda a: a.reshape(bx, B_HEADS, 1, B_DV)
            ob, s_fin = gla_step(colv(y[:, hd:hd + nk]), colv(y[:, hd + nk:hd + 2 * nk]), colv(g),
                                 rowv(y[:, 2 * hd:2 * hd + B_HEADS * B_DV]),
                                 rowv(y[:, 2 * hd + B_HEADS * B_DV:2 * hd + 2 * B_HEADS * B_DV]),
                                 w["b_out_norm"][li], state_b, li, tb=8)
            xf = out_proj([oa.reshape(bx, hd), ob.reshape(bx, B_HEADS * B_DV)], [ab["wo_a"], ab["wo_b"]], xf, tm=bx)
            a_k.append(qk[:, hd:].reshape(bx, 1, A_HEADS, HEAD_DIM))
            a_v.append(v_new.reshape(bx, 1, A_HEADS, HEAD_DIM))
            b_s.append(s_fin)
        else:
            c = w["c"][li]
            y1 = norm_proj(xf, w["g_mix"][l], c["w1"], c["s1"], head_norm=True, tm=bx)
            y2 = norm_proj(xf, w["g_mix"][l], c["w2"], c["s2"], head_norm=False, tm=bx)
            k0 = C_KV_COL
            k_new = y1[:, C_KV_COL:C_IK_COL].reshape(bx, C_KV_HEADS, HEAD_DIM)
            v_new = y2[:, C_KV_COL:C_IK_COL].reshape(bx, C_KV_HEADS, HEAD_DIM)
            scores = dsa_sample_scores(page_table, y2[:, :C_IDX_HEADS * LANES].reshape(bx, C_IDX_HEADS, LANES),
                                       y2[:, C_IW_COL:C_IW_COL + C_IDX_HEADS].reshape(bx, C_IDX_HEADS, 1),
                                       y1[:, C_IK_COL:C_IK_COL + LANES].reshape(bx, 1, LANES),
                                       pool_ikt, li, n_rows)
            mask = dsa_select_sample(scores.reshape(bx, n_rows * LANES), past + 1)
            oc = dsa_sample_attn(page_table, y1[:, :k0].reshape(bx, C_HEADS, HEAD_DIM),
                                 jnp.repeat(k_new, grp, axis=1), jnp.repeat(v_new, grp, axis=1),
                                 mask.reshape(bx, n_rows, LANES), dsa_bias, dsa_b0, pool_kt, pool_vt, li)
            xf = out_proj([oc.reshape(bx, k0)], [c["wo"]], xf, tm=bx)
            c_k.append(k_new.reshape(bx, 1, C_KV_HEADS, HEAD_DIM))
            c_v.append(v_new.reshape(bx, 1, C_KV_HEADS, HEAD_DIM))
            c_ik.append(y1[:, C_IK_COL:C_IK_COL + C_IDX_DIM].reshape(bx, 1, C_IDX_DIM))
        f = w["ffn"][l]
        xf = ffn_ple(xf, p[l].reshape(bx, -1), f["gf"], f["w1"], f["w3"], f["w2"], f["gp"], f["wg"], f["wp"],
                     tm=bx, tf=256)
    return (xf.reshape(bx, 1, d), jnp.stack(a_k), jnp.stack(a_v), jnp.stack(b_s),
            jnp.stack(c_k), jnp.stack(c_v), jnp.stack(c_ik))


def prep_weights(rel_bias, g_mix, w_in_ab, a_q_norm, a_k_norm, b_gate_w2, b_gate_b, b_out_norm, w_out_ab,
                 w_in_c, c_q_norm, c_k_norm, c_ik_norm, w_out_c, g_ffn, w_ff1, w_ff3, w_ff2, g_ple,
                 w_ple_gate, w_ple_proj):
    w = dict(g_mix=g_mix, b_gate_b=b_gate_b, b_out_norm=b_out_norm)
    w["ab"] = [prep_ab(w_in_ab[i], a_q_norm[i], a_k_norm[i], b_gate_w2[i], w_out_ab[i])
               for i in range(w_in_ab.shape[0])]
    w["c"] = [prep_c(w_in_c[i], c_q_norm[i], c_k_norm[i], c_ik_norm[i], w_out_c[i])
              for i in range(w_in_c.shape[0])]
    w["ffn"] = [dict(gf=g_ffn[l], w1=w_ff1[l].astype(BF16), w3=w_ff3[l].astype(BF16), w2=w_ff2[l].astype(BF16),
                     gp=g_ple[l], wg=w_ple_gate[l].astype(BF16), wp=w_ple_proj[l].astype(BF16))
                for l in range(g_ffn.shape[0])]
    tables = dict(dil=dilated_bias_table(rel_bias), dsa=dsa_bias_table(rel_bias))
    return w, tables


def kernel(x_prompt, x_sample, cache_a_k, cache_a_v, state_b, cache_c_k, cache_c_v, cache_c_ik, page_table,
           p_prompt, p_sample, rel_bias, g_mix, w_in_ab, a_q_norm, a_k_norm, b_gate_w2, b_gate_b, b_out_norm,
           w_out_ab, w_in_c, c_q_norm, c_k_norm, c_ik_norm, w_out_c, g_ffn, w_ff1, w_ff3, w_ff2, g_ple,
           w_ple_gate, w_ple_proj):
    w, tables = prep_weights(rel_bias, g_mix, w_in_ab, a_q_norm, a_k_norm, b_gate_w2, b_gate_b, b_out_norm,
                             w_out_ab, w_in_c, c_q_norm, c_k_norm, c_ik_norm, w_out_c, g_ffn, w_ff1, w_ff3,
                             w_ff2, g_ple, w_ple_gate, w_ple_proj)
    prompt = prompt_forward(x_prompt, p_prompt, w, tables)
    sample = sample_forward(x_sample, p_sample, cache_a_k, cache_a_v, state_b, cache_c_k, cache_c_v,
                            cache_c_ik, page_table, w, rel_bias)
    return (prompt[0], sample[0]) + tuple(prompt[1:]) + tuple(sample[1:])
```

```python
import functools
import math

import jax
import jax.numpy as jnp
import numpy as np
from jax import lax
from jax.experimental import pallas as pl
from jax.experimental.pallas import tpu as pltpu

F32 = jnp.float32
BF16 = jnp.bfloat16

LANES = 128
HEAD_DIM = 64
A_HEADS = 8
A_BRANCHES = ((128, 1), (512, 4), (2048, 16))
A_WIN_MAX = 2048
B_HEADS = 4
B_DK = 64
B_DV = 128
B_GATE_RANK = 16
B_GATE_TAU = 16.0
C_HEADS = 16
C_KV_HEADS = 4
C_IDX_HEADS = 8
C_IDX_DIM = 64
C_TOPK_MAX = 256
N_BUCKETS = 32
REL_MAX_DIST = A_WIN_MAX
Q_BLOCK = 128
PAGE_SIZE = 128
NORM_EPS = 1e-6
NEG = -1e30
LOG2E = math.log2(math.e)
VMEM_LIMIT = 56 * 1024 * 1024


def _params(*sem):
    return pltpu.CompilerParams(dimension_semantics=sem, vmem_limit_bytes=VMEM_LIMIT)


def _rms(x, g):
    ms = jnp.mean(x * x, axis=-1, keepdims=True)
    return x * lax.rsqrt(ms + NORM_EPS) * g


def _dot(a, b):
    return jnp.dot(a, b, preferred_element_type=F32)


def _dot_nt(a, b):
    return lax.dot_general(a, b, (((1,), (1,)), ((), ())), preferred_element_type=F32)


def _dot_tn(a, b):
    return lax.dot_general(a, b, (((0,), (0,)), ((), ())), preferred_element_type=F32)


PROJ_CHUNK = 256


def _proj_kernel(x_ref, g_ref, w_ref, cs_ref, p_ref, o_ref, *, head_norm):
    hn = _rms(x_ref[...], g_ref[...]).astype(BF16)
    for c in range(o_ref.shape[1] // PROJ_CHUNK):
        sl = slice(c * PROJ_CHUNK, (c + 1) * PROJ_CHUNK)
        y = _dot(hn, w_ref[:, sl])
        if head_norm:
            ms = _dot((y * y).astype(BF16), p_ref[...])
            y = y * lax.rsqrt(ms + NORM_EPS)
        o_ref[:, sl] = y * cs_ref[:, sl]


def _group_mean_matrix():
    r = np.arange(PROJ_CHUNK) // HEAD_DIM
    return jnp.asarray((r[:, None] == r[None, :]).astype(np.float32) / HEAD_DIM, BF16)


def norm_proj(x, g, w, colscale, *, head_norm, tm):
    n, d = x.shape
    dout = w.shape[1]
    assert n % tm == 0 and dout % PROJ_CHUNK == 0
    return pl.pallas_call(
        functools.partial(_proj_kernel, head_norm=head_norm),
        grid=(n // tm,),
        in_specs=[
            pl.BlockSpec((tm, d), lambda i: (i, 0)),
            pl.BlockSpec((1, d), lambda i: (0, 0)),
            pl.BlockSpec((d, dout), lambda i: (0, 0)),
            pl.BlockSpec((1, dout), lambda i: (0, 0)),
            pl.BlockSpec((PROJ_CHUNK, PROJ_CHUNK), lambda i: (0, 0)),
        ],
        out_specs=pl.BlockSpec((tm, dout), lambda i: (i, 0)),
        out_shape=jax.ShapeDtypeStruct((n, dout), F32),
        compiler_params=_params("parallel"),
        name="norm_proj_hn" if head_norm else "norm_proj",
    )(x, g.reshape(1, d), w, colscale.reshape(1, dout), _group_mean_matrix())


def _out_kernel(*refs, n_in):
    res_ref, o_ref = refs[2 * n_in], refs[2 * n_in + 1]
    acc = res_ref[...]
    for a_ref, w_ref in zip(refs[:n_in], refs[n_in:2 * n_in]):
        acc = acc + _dot(a_ref[...].astype(BF16), w_ref[...])
    o_ref[...] = acc


def out_proj(a_list, w_list, res, *, tm):
    n, d = res.shape
    n_in = len(a_list)
    in_specs = [pl.BlockSpec((tm, a.shape[1]), lambda i: (i, 0)) for a in a_list]
    in_specs += [pl.BlockSpec(w.shape, lambda i: (0, 0)) for w in w_list]
    in_specs += [pl.BlockSpec((tm, d), lambda i: (i, 0))]
    return pl.pallas_call(
        functools.partial(_out_kernel, n_in=n_in),
        grid=(n // tm,),
        in_specs=in_specs,
        out_specs=pl.BlockSpec((tm, d), lambda i: (i, 0)),
        out_shape=jax.ShapeDtypeStruct((n, d), F32),
        compiler_params=_params("parallel"),
        name="out_proj",
    )(*a_list, *w_list, res)


def _ffn_kernel(x_ref, gf_ref, w1_ref, w3_ref, w2_ref, gp_ref, wg_ref, p_ref, wp_ref, o_ref,
                hn_ref, acc_ref):
    f = pl.program_id(1)

    @pl.when(f == 0)
    def _():
        hn_ref[...] = _rms(x_ref[...], gf_ref[...]).astype(BF16)
        acc_ref[...] = jnp.zeros_like(acc_ref)

    hn = hn_ref[...]
    h1 = _dot(hn, w1_ref[...])
    h3 = _dot(hn, w3_ref[...])
    a = h1 * jax.nn.sigmoid(h1) * h3
    acc_ref[...] += _dot(a.astype(BF16), w2_ref[...])

    @pl.when(f == pl.num_programs(1) - 1)
    def _():
        x2 = x_ref[...] + acc_ref[...]
        u = _rms(x2, gp_ref[...]).astype(BF16)
        gate = jax.nn.sigmoid(_dot(u, wg_ref[...]))
        o_ref[...] = x2 + gate * _dot(p_ref[...].astype(BF16), wp_ref[...])


FFN_ROWS = 1024


def ffn_ple(x, p, gf, w1, w3, w2, gp, wg, wp, *, tm, tf):
    n, d = x.shape
    dff = w1.shape[1]
    dple = p.shape[1]
    assert n % tm == 0 and dff % tf == 0
    return pl.pallas_call(
        _ffn_kernel,
        grid=(n // tm, dff // tf),
        in_specs=[
            pl.BlockSpec((tm, d), lambda i, f: (i, 0)),
            pl.BlockSpec((1, d), lambda i, f: (0, 0)),
            pl.BlockSpec((d, tf), lambda i, f: (0, f)),
            pl.BlockSpec((d, tf), lambda i, f: (0, f)),
            pl.BlockSpec((tf, d), lambda i, f: (f, 0)),
            pl.BlockSpec((1, d), lambda i, f: (0, 0)),
            pl.BlockSpec((d, d), lambda i, f: (0, 0)),
            pl.BlockSpec((tm, dple), lambda i, f: (i, 0)),
            pl.BlockSpec((dple, d), lambda i, f: (0, 0)),
        ],
        out_specs=pl.BlockSpec((tm, d), lambda i, f: (i, 0)),
        out_shape=jax.ShapeDtypeStruct((n, d), F32),
        scratch_shapes=[pltpu.VMEM((tm, d), BF16), pltpu.VMEM((tm, d), F32)],
        compiler_params=_params("parallel", "arbitrary"),
        name="ffn_ple",
    )(x, gf.reshape(1, d), w1, w3, w2, gp.reshape(1, d), wg, p, wp)


def _rel_bucket(dist):
    n = jnp.maximum(dist, 0)
    exact = N_BUCKETS // 2
    nf = jnp.maximum(n, exact).astype(F32)
    large = exact + (jnp.log(nf / exact) / math.log(REL_MAX_DIST / exact)
                     * (N_BUCKETS - exact)).astype(jnp.int32)
    return jnp.where(n < exact, n, jnp.minimum(large, N_BUCKETS - 1))


def _branch_multiplicity(delta):
    mult = jnp.zeros(delta.shape, jnp.int32)
    for w, d in A_BRANCHES:
        mult = mult + ((delta >= 0) & (delta <= w) & (delta % d == 0)).astype(jnp.int32)
    return mult


DIL_WIN_CHUNKS = A_WIN_MAX // Q_BLOCK + 1


def _toeplitz(value_of_delta, base, width):
    period = width + Q_BLOCK - 1
    x = jnp.concatenate([jnp.arange(width), jnp.arange(-(Q_BLOCK - 1), 0)])
    v = value_of_delta(base - x)
    t = jnp.tile(v, (1, Q_BLOCK))[:, :Q_BLOCK * (period - 1)]
    return t.reshape(v.shape[0], Q_BLOCK, period - 1)[:, :, :width]


def dilated_bias_table(rel_bias):
    def value(delta):
        mult = _branch_multiplicity(delta)
        bias = rel_bias[_rel_bucket(delta)][:, :A_HEADS].astype(F32).T
        return jnp.where(mult[None] > 0, bias + jnp.log(jnp.maximum(mult, 1).astype(F32))[None], NEG)

    t = _toeplitz(value, A_WIN_MAX, DIL_WIN_CHUNKS * Q_BLOCK)
    t = t.reshape(A_HEADS // 2, 2 * Q_BLOCK, DIL_WIN_CHUNKS * Q_BLOCK)
    return jnp.pad(t * LOG2E, ((0, 0), (0, 0), (0, Q_BLOCK)), constant_values=NEG)


DSA_BIAS_TILES = 14


def dsa_bias_table(rel_bias):
    def value(delta):
        return rel_bias[_rel_bucket(delta)][:, :C_HEADS].astype(F32).T

    last = DSA_BIAS_TILES - 1
    t = _toeplitz(value, last * Q_BLOCK, DSA_BIAS_TILES * Q_BLOCK)
    grp = C_HEADS // C_KV_HEADS
    t = t.reshape(C_KV_HEADS, grp, Q_BLOCK, DSA_BIAS_TILES, Q_BLOCK)[:, :, :, ::-1, :]
    return (t * LOG2E).transpose(0, 3, 4, 1, 2).reshape(C_KV_HEADS, DSA_BIAS_TILES, Q_BLOCK, grp * Q_BLOCK)


def _lane_lt64():
    return lax.broadcasted_iota(jnp.int32, (1, LANES), 1) < HEAD_DIM


def _dil_kernel(q_ref, k_ref, v_ref, t_ref, o_ref, s_ref, p_ref):
    i = pl.program_id(2)
    nw = s_ref.shape[1] // Q_BLOCK
    last = DIL_WIN_CHUNKS - 1
    lo = _lane_lt64()
    q = q_ref[...] * LOG2E
    q2 = jnp.concatenate([jnp.where(lo, q, 0.0), jnp.where(lo, 0.0, q)], axis=0).astype(BF16)
    w0 = jnp.maximum(i - (nw - 1), 0)
    rows = pl.ds(pl.multiple_of(w0 * Q_BLOCK, Q_BLOCK), nw * Q_BLOCK)
    s_ref[...] = _dot_nt(q2, k_ref[rows, :].astype(BF16))
    m = jnp.full((2 * Q_BLOCK, LANES), NEG, F32)
    for w in range(nw):
        c = last - i + w0 + w
        tcol = pl.multiple_of(jnp.where(c <= last, c, last + 1) * Q_BLOCK, Q_BLOCK)
        cols = slice(w * Q_BLOCK, (w + 1) * Q_BLOCK)
        s = s_ref[:, cols] + t_ref[0, :, pl.ds(tcol, Q_BLOCK)]
        s_ref[:, cols] = s
        m = jnp.maximum(m, s)
    m = jnp.max(m, axis=1, keepdims=True)
    l = jnp.zeros((2 * Q_BLOCK, LANES), F32)
    for w in range(nw):
        cols = slice(w * Q_BLOCK, (w + 1) * Q_BLOCK)
        p = jnp.exp2(s_ref[:, cols] - m)
        l = l + p
        p_ref[:, cols] = p.astype(BF16)
    o = _dot(p_ref[...], v_ref[rows, :].astype(BF16)) / jnp.sum(l, axis=1, keepdims=True)
    o_ref[...] = jnp.where(lo, o[:Q_BLOCK], o[Q_BLOCK:])


def dilated_prompt(qk, yv, table, batch, seq):
    n = batch * seq
    nblk = seq // Q_BLOCK
    npair = A_HEADS // 2
    win = min(DIL_WIN_CHUNKS, nblk) * Q_BLOCK
    return pl.pallas_call(
        _dil_kernel,
        grid=(batch, npair, nblk),
        in_specs=[
            pl.BlockSpec((Q_BLOCK, LANES), lambda b, p, i: (b * nblk + i, p)),
            pl.BlockSpec((seq, LANES), lambda b, p, i: (b, npair + p)),
            pl.BlockSpec((seq, LANES), lambda b, p, i: (b, p)),
            pl.BlockSpec((1,) + table.shape[1:], lambda b, p, i: (p, 0, 0)),
        ],
        out_specs=pl.BlockSpec((Q_BLOCK, LANES), lambda b, p, i: (b * nblk + i, p)),
        out_shape=jax.ShapeDtypeStruct((n, A_HEADS * HEAD_DIM), F32),
        scratch_shapes=[pltpu.VMEM((2 * Q_BLOCK, win), F32), pltpu.VMEM((2 * Q_BLOCK, win), BF16)],
        compiler_params=_params("parallel", "parallel", "arbitrary"),
        name="dilated_prompt",
    )(qk, qk, yv, table)


GLA_CHUNK = 64
GLA_SUB = 16
GLA_EXP_CLAMP = 60.0


def _log_sigmoid(z):
    return jnp.minimum(z, 0.0) - jnp.log1p(jnp.exp(-jnp.abs(z)))


def _gla_kernel(bq_ref, bk_ref, bv_ref, bog_ref, blr_ref, w2_ref, gb_ref, on_ref, o_ref, s_ref, st_ref):
    it = pl.program_id(1)

    @pl.when(it == 0)
    def _():
        st_ref[...] = jnp.zeros_like(st_ref)

    lo = _lane_lt64()
    ch = GLA_CHUNK
    row = lax.broadcasted_iota(jnp.int32, (ch, ch), 0)
    col = lax.broadcasted_iota(jnp.int32, (ch, ch), 1)
    causal = row >= col
    ltri = jnp.where(causal, 1.0, 0.0).astype(BF16)
    top_half = lax.broadcasted_iota(jnp.int32, (LANES, LANES), 0) < HEAD_DIM

    def chunk_body(c, carry):
        rows = pl.ds(pl.multiple_of(c * ch, ch), ch)
        z = _dot(blr_ref[rows, :].astype(BF16), w2_ref[...]) + gb_ref[...]
        gg = _log_sigmoid(z) / B_GATE_TAU
        for p in range(B_HEADS // 2):
            sl = slice(p * LANES, (p + 1) * LANES)
            q = bq_ref[rows, sl]
            k = bk_ref[rows, sl]
            g = gg[:, sl]
            g_hi = g.astype(BF16)
            g_lo = (g - g_hi.astype(F32)).astype(BF16)
            cum = _dot(ltri, g_hi) + _dot(ltri, g_lo)
            last = cum[ch - 1:ch, :]
            st = st_ref[p]
            qd = q * jnp.exp(cum)
            q2 = jnp.concatenate([jnp.where(lo, qd, 0.0), jnp.where(lo, 0.0, qd)], axis=0)
            o_inter = _dot(q2.astype(BF16), st.astype(BF16))
            atts = []
            for sb in range(ch // GLA_SUB):
                rs = slice(sb * GLA_SUB, (sb + 1) * GLA_SUB)
                ref_row = cum[sb * GLA_SUB:sb * GLA_SUB + 1, :]
                qs = q[rs] * jnp.exp(cum[rs] - ref_row)
                ks = k * jnp.exp(jnp.minimum(ref_row - cum, GLA_EXP_CLAMP))
                qq = jnp.concatenate([jnp.where(lo, qs, 0.0), jnp.where(lo, 0.0, qs)], axis=0)
                atts.append(_dot_nt(qq.astype(BF16), ks.astype(BF16)))
            kd = (k * jnp.exp(last - cum)).astype(BF16)
            upd = []
            for e in range(2):
                hs = slice((2 * p + e) * LANES, (2 * p + e + 1) * LANES)
                v = bv_ref[rows, hs].astype(BF16)
                att = jnp.concatenate([a[e * GLA_SUB:(e + 1) * GLA_SUB] for a in atts], axis=0)
                att = jnp.where(causal, att, 0.0)
                o = o_inter[e * ch:(e + 1) * ch] + _dot(att.astype(BF16), v)
                og = _rms(o, on_ref[...])
                gate = bog_ref[rows, hs]
                o_ref[rows, hs] = og * (gate * jax.nn.sigmoid(gate))
                upd.append(_dot_tn(kd, v))
            decay = jnp.transpose(jnp.broadcast_to(jnp.exp(last), (LANES, LANES)))
            st_ref[p] = decay * st + jnp.where(top_half, upd[0], upd[1])
        return carry

    lax.fori_loop(0, bq_ref.shape[0] // ch, chunk_body, 0)

    @pl.when(it == pl.num_programs(1) - 1)
    def _():
        for p in range(B_HEADS // 2):
            s_ref[0, 2 * p] = st_ref[p, :HEAD_DIM, :]
            s_ref[0, 2 * p + 1] = st_ref[p, HEAD_DIM:, :]


def gla_prompt(y, w2p, gate_b, out_norm, batch, seq, *, tg):
    n = batch * seq
    nt = seq // tg
    dv = B_HEADS * B_DV
    return pl.pallas_call(
        _gla_kernel,
        grid=(batch, nt),
        in_specs=[
            pl.BlockSpec((tg, 256), lambda b, t: (b * nt + t, 2)),
            pl.BlockSpec((tg, 256), lambda b, t: (b * nt + t, 3)),
            pl.BlockSpec((tg, dv), lambda b, t: (b * nt + t, 2)),
            pl.BlockSpec((tg, dv), lambda b, t: (b * nt + t, 3)),
            pl.BlockSpec((tg, LANES), lambda b, t: (b * nt + t, 16)),
            pl.BlockSpec((LANES, 256), lambda b, t: (0, 0)),
            pl.BlockSpec((1, 256), lambda b, t: (0, 0)),
            pl.BlockSpec((1, B_DV), lambda b, t: (0, 0)),
        ],
        out_specs=[
            pl.BlockSpec((tg, dv), lambda b, t: (b * nt + t, 0)),
            pl.BlockSpec((1, B_HEADS, B_DK, B_DV), lambda b, t: (b, 0, 0, 0)),
        ],
        out_shape=[jax.ShapeDtypeStruct((n, dv), F32),
                   jax.ShapeDtypeStruct((batch, B_HEADS, B_DK, B_DV), F32)],
        scratch_shapes=[pltpu.VMEM((B_HEADS // 2, LANES, LANES), F32)],
        compiler_params=_params("parallel", "arbitrary"),
        name="gla_prompt",
    )(y, y, y, y, y, w2p, gate_b.reshape(1, 256), out_norm.reshape(1, B_DV))


SEL_CHUNK = 256
INT_MIN = -2 ** 31


def _sortable_key(score):
    bits = pltpu.bitcast(score + 0.0, jnp.int32)
    return jnp.where(bits < 0, bits ^ jnp.int32(0x7FFFFFFF), bits)


def _threshold_select(key_ref, n_ch, topk, idx_bits, row_limit, write):
    r = key_ref.shape[0]
    kc = SEL_CHUNK
    lane = lax.broadcasted_iota(jnp.int32, (1, kc), 1)

    def count(hits):
        def body(c, acc):
            c0 = pl.multiple_of(c * kc, kc)
            hit = hits(key_ref[:, pl.ds(c0, kc)], c0 + lane)
            for t in range(kc // LANES):
                acc = acc + hit[:, t * LANES:(t + 1) * LANES]
            return acc
        acc = lax.fori_loop(0, n_ch, body, jnp.zeros((r, LANES), jnp.int32))
        return jnp.sum(acc, axis=1, keepdims=True)

    n_nonneg = count(lambda k, _: jnp.where(k >= 0, 1, 0))
    base = jnp.where(n_nonneg >= topk, 0, INT_MIN).astype(jnp.int32)
    n_ge = jnp.where(n_nonneg >= topk, n_nonneg, n_ch * kc)

    def bit_body(b, carry):
        base, n_ge = carry
        cand = base | jnp.left_shift(jnp.int32(1), 30 - b)
        cnt = count(lambda k, _: jnp.where(k >= cand, 1, 0))
        return jnp.where(cnt >= topk, cand, base), jnp.where(cnt >= topk, cnt, n_ge)

    tau, n_ge = lax.fori_loop(0, 31, bit_body, (base, n_ge))

    def tie_break():
        need = topk - count(lambda k, _: jnp.where(k > tau, 1, 0))

        def idx_body(b, lo):
            cand = lo | jnp.left_shift(jnp.int32(1), idx_bits - 1 - b)
            cnt = count(lambda k, col: jnp.where(k == tau, jnp.where(col < cand, 1, 0), 0))
            return jnp.where(cnt < need, cand, lo)

        return lax.fori_loop(0, idx_bits, idx_body, jnp.zeros((r, 1), jnp.int32))

    last_eq = lax.cond(jnp.max(n_ge) > topk, tie_break,
                       lambda: jnp.full((r, 1), 2 ** idx_bits, jnp.int32))

    def out_body(c, carry):
        c0 = pl.multiple_of(c * kc, kc)
        k = key_ref[:, pl.ds(c0, kc)]
        col = c0 + lane
        val = jnp.where(k > tau, 0.0, jnp.where(k == tau, jnp.where(col <= last_eq, 0.0, NEG), NEG))
        write(c0, jnp.where(col <= row_limit, val, NEG))
        return carry

    lax.fori_loop(0, n_ch, out_body, 0)


def _threshold_select_t(key_ref, n_ch, topk, idx_bits, t_idx, write):
    nq = key_ref.shape[1]
    kc = SEL_CHUNK
    rowi = lax.broadcasted_iota(jnp.int32, (kc, 1), 0)

    def count(hits):
        def body(c, acc):
            c0 = pl.multiple_of(c * kc, kc)
            hit = hits(key_ref[pl.ds(c0, kc), :], c0 + rowi)
            return acc + jnp.sum(hit.reshape(kc // 8, 8, nq), axis=0)
        acc = lax.fori_loop(0, n_ch, body, jnp.zeros((8, nq), jnp.int32))
        return jnp.sum(acc, axis=0, keepdims=True)

    n_nonneg = count(lambda k, _: jnp.where(k >= 0, 1, 0))
    base = jnp.where(n_nonneg >= topk, 0, INT_MIN).astype(jnp.int32)
    n_ge = jnp.where(n_nonneg >= topk, n_nonneg, n_ch * kc)

    def bit_body(b, carry):
        base, n_ge = carry
        cand = base | jnp.left_shift(jnp.int32(1), 30 - b)
        cnt = count(lambda k, _: jnp.where(k >= cand, 1, 0))
        return jnp.where(cnt >= topk, cand, base), jnp.where(cnt >= topk, cnt, n_ge)

    tau, n_ge = lax.fori_loop(0, 31, bit_body, (base, n_ge))

    def tie_break():
        need = topk - count(lambda k, _: jnp.where(k > tau, 1, 0))

        def idx_body(b, lo):
            cand = lo | jnp.left_shift(jnp.int32(1), idx_bits - 1 - b)
            cnt = count(lambda k, s: jnp.where(k == tau, jnp.where(s < cand, 1, 0), 0))
            return jnp.where(cnt < need, cand, lo)

        return lax.fori_loop(0, idx_bits, idx_body, jnp.zeros((1, nq), jnp.int32))

    last_eq = lax.cond(jnp.max(n_ge) > topk, tie_break,
                       lambda: jnp.full((1, nq), 2 ** idx_bits, jnp.int32))

    def out_body(c, carry):
        c0 = pl.multiple_of(c * kc, kc)
        k = key_ref[pl.ds(c0, kc), :]
        s = c0 + rowi
        val = jnp.where(k > tau, 0.0, jnp.where(k == tau, jnp.where(s <= last_eq, 0.0, NEG), NEG))
        write(c0, jnp.where(s <= t_idx, val, NEG))
        return carry

    lax.fori_loop(0, n_ch, out_body, 0)


SEL_QUERIES = 256


def _select_kernel(iq_ref, iwt_ref, ik_ref, o_ref, key_ref, qs_ref, *, topk, idx_bits):
    qb = pl.program_id(1)
    kc = SEL_CHUNK
    nq = SEL_QUERIES
    n_ch = (qb * nq + nq + kc - 1) // kc
    t_idx = qb * nq + lax.broadcasted_iota(jnp.int32, (1, nq), 1)
    rowi = lax.broadcasted_iota(jnp.int32, (kc, 1), 0)
    iwt = iwt_ref[...]
    for h in range(C_IDX_HEADS):
        qs_ref[h] = iq_ref[:, h * LANES:(h + 1) * LANES].astype(BF16)

    def score_body(c, carry):
        c0 = pl.multiple_of(c * kc, kc)
        ik = ik_ref[pl.ds(c0, kc), :].astype(BF16)
        sc = jnp.zeros((kc, nq), F32)
        for h in range(C_IDX_HEADS):
            sc = sc + jnp.maximum(_dot_nt(ik, qs_ref[h]), 0.0) * iwt[h:h + 1, :]
        key_ref[pl.ds(c0, kc), :] = jnp.where(c0 + rowi <= t_idx, _sortable_key(sc), INT_MIN)
        return carry

    lax.fori_loop(0, n_ch, score_body, 0)
    o_ref[...] = jnp.full(o_ref.shape, NEG, o_ref.dtype)

    def write(c0, val):
        o_ref[pl.ds(c0, kc), :] = val.astype(o_ref.dtype)

    _threshold_select_t(key_ref, n_ch, topk, idx_bits, t_idx, write)


def dsa_select_prompt(y2, iwt, y1, batch, seq):
    assert seq % SEL_CHUNK == 0 and seq % SEL_QUERIES == 0
    nqb = seq // SEL_QUERIES
    topk = min(C_TOPK_MAX, seq // 4)
    return pl.pallas_call(
        functools.partial(_select_kernel, topk=topk, idx_bits=max(1, (seq - 1).bit_length())),
        grid=(batch, nqb),
        in_specs=[
            pl.BlockSpec((SEL_QUERIES, C_IDX_HEADS * LANES), lambda b, i: (b * nqb + i, 0)),
            pl.BlockSpec((C_IDX_HEADS, SEL_QUERIES), lambda b, i: (0, b * nqb + i)),
            pl.BlockSpec((seq, LANES), lambda b, i: (b, 10)),
        ],
        out_specs=pl.BlockSpec((None, seq, SEL_QUERIES), lambda b, i: (b, 0, i)),
        out_shape=jax.ShapeDtypeStruct((batch, seq, seq), BF16),
        scratch_shapes=[pltpu.VMEM((seq, SEL_QUERIES), jnp.int32),
                        pltpu.VMEM((C_IDX_HEADS, SEL_QUERIES, LANES), BF16)],
        compiler_params=_params("parallel", "arbitrary"),
        name="dsa_select_prompt",
    )(y2, iwt, y1)


def _dsa_attn_kernel(q_ref, k_ref, vt_ref, mask_ref, tb_ref, o_ref):
    g = pl.program_id(1)
    i = pl.program_id(2)
    kc = SEL_CHUNK
    grp = C_HEADS // C_KV_HEADS
    lo = _lane_lt64()
    first = (g % 2) == 0
    lane_half = jnp.right_shift(lax.broadcasted_iota(jnp.int32, (1, LANES), 1), HEAD_DIM.bit_length() - 1)
    own_half = lane_half == g % 2
    nsub = q_ref.shape[0] // Q_BLOCK
    cols = grp * Q_BLOCK
    parts = []
    for u in range(nsub):
        for j in range(grp // 2):
            qp = q_ref[u * Q_BLOCK:(u + 1) * Q_BLOCK, j * LANES:(j + 1) * LANES]
            qr = pltpu.roll(qp, HEAD_DIM, 1)
            parts.append(jnp.where(own_half, jnp.where(first, qp, qr), 0.0))
            parts.append(jnp.where(own_half, jnp.where(first, qr, qp), 0.0))
    q4 = (jnp.concatenate(parts, axis=0) * LOG2E).astype(BF16)

    def body(c, carry):
        m, l, acc = carry
        c0 = pl.multiple_of(c * kc, kc)
        st = _dot_nt(k_ref[pl.ds(c0, kc), :].astype(BF16), q4)
        rows = []
        for hf in range(kc // Q_BLOCK):
            mk = mask_ref[pl.ds(c0 + hf * Q_BLOCK, Q_BLOCK), :].astype(F32)
            blocks = []
            for u in range(nsub):
                e = jnp.clip(i * nsub + u - (c * (kc // Q_BLOCK) + hf), 0, DSA_BIAS_TILES - 1)
                blocks.append(st[hf * Q_BLOCK:(hf + 1) * Q_BLOCK, u * cols:(u + 1) * cols] + tb_ref[0, e]
                              + jnp.concatenate([mk[:, u * Q_BLOCK:(u + 1) * Q_BLOCK]] * grp, axis=1))
            rows.append(jnp.concatenate(blocks, axis=1))
        s = jnp.concatenate(rows, axis=0)
        m_new = jnp.maximum(m, jnp.max(s, axis=0, keepdims=True))
        alpha = jnp.exp2(m - m_new)
        p = jnp.exp2(s - m_new)
        l = alpha * l + jnp.sum(p, axis=0, keepdims=True)
        acc = alpha * acc + _dot(vt_ref[:, pl.ds(c0, kc)].astype(BF16), p.astype(BF16))
        return m_new, l, acc

    init = (jnp.full((1, nsub * cols), NEG, F32), jnp.zeros((1, nsub * cols), F32),
            jnp.zeros((LANES, nsub * cols), F32))
    n_ch = (i * nsub * Q_BLOCK + nsub * Q_BLOCK + kc - 1) // kc
    _, l, acc = lax.fori_loop(0, n_ch, body, init)
    ot = acc / l
    for u in range(nsub):
        heads = [jnp.transpose(ot[:, (u * grp + h) * Q_BLOCK:(u * grp + h + 1) * Q_BLOCK]) for h in range(grp)]
        for j in range(grp // 2):
            a, b = heads[2 * j], heads[2 * j + 1]
            left = jnp.where(first, a, pltpu.roll(a, HEAD_DIM, 1))
            right = jnp.where(first, pltpu.roll(b, HEAD_DIM, 1), b)
            o_ref[u * Q_BLOCK:(u + 1) * Q_BLOCK, j * LANES:(j + 1) * LANES] = jnp.where(lo, left, right)


DSA_ATTN_QUERIES = 512


def dsa_attn_prompt(y1, vt, layer, mask_t, table, batch, seq):
    n = batch * seq
    nq = DSA_ATTN_QUERIES
    assert seq % nq == 0
    nblk = seq // nq
    qw = (C_HEADS // C_KV_HEADS) * HEAD_DIM
    return pl.pallas_call(
        _dsa_attn_kernel,
        grid=(batch, C_KV_HEADS, nblk),
        in_specs=[
            pl.BlockSpec((nq, qw), lambda b, g, i: (b * nblk + i, g)),
            pl.BlockSpec((seq, LANES), lambda b, g, i: (b, 8 + g // 2)),
            pl.BlockSpec((None, None, LANES, seq), lambda b, g, i: (layer, b, g // 2, 0)),
            pl.BlockSpec((None, seq, nq), lambda b, g, i: (b, 0, i)),
            pl.BlockSpec((1,) + table.shape[1:], lambda b, g, i: (g, 0, 0, 0)),
        ],
        out_specs=pl.BlockSpec((nq, qw), lambda b, g, i: (b * nblk + i, g)),
        out_shape=jax.ShapeDtypeStruct((n, C_HEADS * HEAD_DIM), F32),
        compiler_params=_params("parallel", "parallel", "arbitrary"),
        name="dsa_attn_prompt",
    )(y1, y1, vt, mask_t, table)


IW_SCALE = (C_IDX_HEADS ** -0.5) * (C_IDX_DIM ** -0.5)
Q_SCALE = HEAD_DIM ** -0.5
AB_Y_COLS = 2304
C_Y_COLS = 1536
C_KV_COL = C_HEADS * HEAD_DIM
C_IK_COL = C_KV_COL + C_KV_HEADS * HEAD_DIM
C_IW_COL = C_IK_COL


def _pad_heads(w, n_heads):
    d = w.shape[0]
    w = w.reshape(d, n_heads, HEAD_DIM)
    return jnp.pad(w, ((0, 0), (0, 0), (0, HEAD_DIM))).reshape(d, n_heads * LANES)


def _pad_cols(w, total):
    return jnp.pad(w, ((0, 0), (0, total - w.shape[1])))


def prep_ab(w_in, a_q_norm, a_k_norm, w2, w_out):
    hd = A_HEADS * HEAD_DIM
    w1 = w_in[:, :2 * hd]
    s1 = jnp.concatenate([jnp.tile(a_q_norm, A_HEADS) * Q_SCALE, jnp.tile(a_k_norm, A_HEADS)])
    nbq = B_HEADS * B_DK
    blr0 = 3 * hd + 2 * nbq + B_HEADS * B_DV
    wy = jnp.concatenate([w_in[:, 2 * hd:blr0], w_in[:, blr0 + B_GATE_RANK:], w_in[:, blr0:blr0 + B_GATE_RANK]],
                         axis=1)
    wy = _pad_cols(wy, AB_Y_COLS)
    sy = jnp.ones((AB_Y_COLS,), F32).at[hd:hd + nbq].set(Q_SCALE)
    w2p = jnp.pad(w2, ((0, LANES - B_GATE_RANK), (0, 0)))
    return dict(w1=w1.astype(BF16), s1=s1, wy=wy.astype(BF16), sy=sy, w2p=w2p.astype(BF16),
                wo_a=w_out[:hd].astype(BF16), wo_b=w_out[hd:].astype(BF16))


def prep_c(w_in, c_q_norm, c_k_norm, c_ik_norm, w_out):
    o = np.cumsum((0, C_HEADS * HEAD_DIM, C_KV_HEADS * HEAD_DIM, C_KV_HEADS * HEAD_DIM,
                   C_IDX_HEADS * C_IDX_DIM, C_IDX_DIM, C_IDX_HEADS))
    cq, ck, cv, iq, ik, iw = (w_in[:, o[t]:o[t + 1]] for t in range(6))
    w1 = _pad_cols(jnp.concatenate([cq, ck, ik], axis=1), C_Y_COLS)
    s1 = jnp.concatenate([jnp.tile(c_q_norm, C_HEADS) * Q_SCALE, jnp.tile(c_k_norm, C_KV_HEADS), c_ik_norm])
    s1 = jnp.pad(s1, (0, C_Y_COLS - s1.shape[0]))
    w2 = _pad_cols(jnp.concatenate([_pad_heads(iq, C_IDX_HEADS), cv, iw], axis=1), C_Y_COLS)
    s2 = jnp.ones((C_Y_COLS,), F32).at[C_IW_COL:C_IW_COL + C_IDX_HEADS].set(IW_SCALE)
    return dict(w1=w1.astype(BF16), s1=s1, w2=w2.astype(BF16), s2=s2, wo=w_out.astype(BF16))


CACHE_POS_TILE = 512


def _to_cache_kernel(*refs, n_feat):
    src_ref, o_ref = refs[0], refs[-1]
    o_ref[...] = jnp.transpose(src_ref[...])[:n_feat]


def to_cache(src, col_block, width, n_feat, batch, seq, pos0, layer, n_layers, dst=None):
    n_pos = seq - pos0
    tl = min(CACHE_POS_TILE, n_pos)
    assert n_pos % tl == 0 and pos0 % tl == 0 and seq % tl == 0
    in_specs = [pl.BlockSpec((tl, width), lambda b, i: ((b * seq + pos0) // tl + i, col_block))]
    args = [src]
    aliases = {}
    if dst is not None:
        in_specs.append(pl.BlockSpec(memory_space=pl.ANY))
        args.append(dst)
        aliases = {1: 0}
    return pl.pallas_call(
        functools.partial(_to_cache_kernel, n_feat=n_feat),
        grid=(batch, n_pos // tl),
        in_specs=in_specs,
        out_specs=pl.BlockSpec((None, None, n_feat, tl), lambda b, i: (layer, b, 0, i)),
        out_shape=jax.ShapeDtypeStruct((n_layers, batch, n_feat, n_pos), F32),
        input_output_aliases=aliases,
        compiler_params=_params("parallel", "parallel"),
        name="to_cache",
    )(*args)


def _cache_result(buf, n_heads):
    nl, b, f, s = buf.shape
    if n_heads is None:
        return buf.transpose(0, 1, 3, 2)
    return buf.reshape(nl, b, n_heads, f // n_heads, s).transpose(0, 1, 4, 2, 3)


def prompt_forward(x, p, w, tables):
    batch, seq, d = x.shape
    n = batch * seq
    tm = min(512, n)
    wp = min(A_WIN_MAX, seq)
    hd = A_HEADS * HEAD_DIM
    n_ab, n_c = len(w["ab"]), len(w["c"])
    xf = x.reshape(n, d)
    a_k = a_v = c_k = c_v = c_ik = None
    b_s = []
    for l in range(len(w["ffn"])):
        li = l // 2
        if l % 2 == 0:
            ab = w["ab"][li]
            qk = norm_proj(xf, w["g_mix"][l], ab["w1"], ab["s1"], head_norm=True, tm=tm)
            y = norm_proj(xf, w["g_mix"][l], ab["wy"], ab["sy"], head_norm=False, tm=tm)
            oa = dilated_prompt(qk, y, tables["dil"], batch, seq)
            ob, s_fin = gla_prompt(y, ab["w2p"], w["b_gate_b"][li], w["b_out_norm"][li], batch, seq,
                                   tg=min(256, seq))
            xf = out_proj([oa, ob], [ab["wo_a"], ab["wo_b"]], xf, tm=tm)
            a_k = to_cache(qk, 1, hd, hd, batch, seq, seq - wp, li, n_ab, a_k)
            a_v = to_cache(y, 0, hd, hd, batch, seq, seq - wp, li, n_ab, a_v)
            b_s.append(s_fin)
        else:
            c = w["c"][li]
            kvw = C_KV_HEADS * HEAD_DIM
            y1 = norm_proj(xf, w["g_mix"][l], c["w1"], c["s1"], head_norm=True, tm=tm)
            y2 = norm_proj(xf, w["g_mix"][l], c["w2"], c["s2"], head_norm=False, tm=tm)
            c_v = to_cache(y2, C_KV_COL // kvw, kvw, kvw, batch, seq, 0, li, n_c, c_v)
            mask_t = dsa_select_prompt(y2, y2[:, C_IW_COL:C_IW_COL + C_IDX_HEADS].T, y1, batch, seq)
            oc = dsa_attn_prompt(y1, c_v, li, mask_t, tables["dsa"], batch, seq)
            xf = out_proj([oc], [c["wo"]], xf, tm=tm)
            c_k = to_cache(y1, C_KV_COL // kvw, kvw, kvw, batch, seq, 0, li, n_c, c_k)
            c_ik = to_cache(y1, C_IK_COL // LANES, LANES, C_IDX_DIM, batch, seq, 0, li, n_c, c_ik)
        f = w["ffn"][l]
        xf = ffn_ple(xf, p[l].reshape(n, -1), f["gf"], f["w1"], f["w3"], f["w2"], f["gp"], f["wg"], f["wp"],
                     tm=FFN_ROWS if n % FFN_ROWS == 0 else tm, tf=256)
    return (xf.reshape(batch, seq, d), _cache_result(a_k, A_HEADS), _cache_result(a_v, A_HEADS), jnp.stack(b_s),
            _cache_result(c_k, C_KV_HEADS), _cache_result(c_v, C_KV_HEADS), _cache_result(c_ik, None))


def dilated_sample_bias(rel_bias, wb):
    delta = wb - jnp.arange(wb)
    mult = _branch_multiplicity(delta)
    bias = rel_bias[_rel_bucket(delta)][:, :A_HEADS].astype(F32).T
    t = jnp.where(mult[None] > 0, bias + jnp.log(jnp.maximum(mult, 1).astype(F32))[None], NEG)
    b0 = rel_bias[_rel_bucket(jnp.zeros((), jnp.int32))][:A_HEADS].astype(F32)
    return t[:, None, :], (b0 + math.log(len(A_BRANCHES)))[:, None, None]


def _dil_sample_kernel(q_ref, kn_ref, vn_ref, kt_ref, vt_ref, t_ref, b0_ref, o_ref):
    q = q_ref[...]
    s_new = jnp.sum(q * kn_ref[...], axis=1, keepdims=True) + b0_ref[...]
    s = jnp.sum(q * kt_ref[...], axis=1, keepdims=True) + t_ref[...]
    m = jnp.maximum(s_new, jnp.max(s, axis=2, keepdims=True))
    p = jnp.exp(s - m)
    p_new = jnp.exp(s_new - m)
    l = p_new + jnp.sum(p, axis=2, keepdims=True)
    acc = p_new * vn_ref[...] + jnp.sum(p * vt_ref[...], axis=2, keepdims=True)
    o_ref[...] = acc / l


def dilated_sample(q_col, kn_col, vn_col, cache_kt, cache_vt, li, table, b0):
    bx, nh, hd, _ = q_col.shape
    wb = cache_kt.shape[-1]
    col = pl.BlockSpec((None, nh, hd, 1), lambda b: (b, 0, 0, 0))
    cache = pl.BlockSpec((None, None, nh, hd, wb), lambda b: (li, b, 0, 0, 0))
    return pl.pallas_call(
        _dil_sample_kernel,
        grid=(bx,),
        in_specs=[col, col, col, cache, cache,
                  pl.BlockSpec(table.shape, lambda b: (0, 0, 0)),
                  pl.BlockSpec(b0.shape, lambda b: (0, 0, 0))],
        out_specs=col,
        out_shape=jax.ShapeDtypeStruct((bx, nh, hd, 1), F32),
        compiler_params=_params("parallel"),
        name="dilated_sample",
    )(q_col, kn_col, vn_col, cache_kt, cache_vt, table, b0)


def _gla_gate_kernel(blr_ref, w2_ref, gb_ref, o_ref):
    z = _dot(blr_ref[...].astype(BF16), w2_ref[...]) + gb_ref[...]
    o_ref[...] = _log_sigmoid(z) / B_GATE_TAU


def gla_gate(y, w2p, gate_b):
    n = y.shape[0]
    nk = B_HEADS * B_DK
    return pl.pallas_call(
        _gla_gate_kernel,
        grid=(1,),
        in_specs=[pl.BlockSpec((n, LANES), lambda i: (0, 16)),
                  pl.BlockSpec((LANES, nk), lambda i: (0, 0)),
                  pl.BlockSpec((1, nk), lambda i: (0, 0))],
        out_specs=pl.BlockSpec((n, nk), lambda i: (0, 0)),
        out_shape=jax.ShapeDtypeStruct((n, nk), F32),
        compiler_params=_params("arbitrary"),
        name="gla_gate",
    )(y, w2p, gate_b.reshape(1, nk))


def _gla_step_kernel(q_ref, k_ref, g_ref, v_ref, bog_ref, on_ref, s_ref, o_ref, sn_ref):
    st = jnp.exp(g_ref[...]) * s_ref[...] + k_ref[...] * v_ref[...]
    sn_ref[...] = st
    o = jnp.sum(q_ref[...] * st, axis=2, keepdims=True)
    gate = bog_ref[...]
    o_ref[...] = _rms(o, on_ref[...]) * (gate * jax.nn.sigmoid(gate))


def gla_step(q, k, g, v, bog, out_norm, state, li, *, tb):
    bx = q.shape[0]
    col = pl.BlockSpec((tb, B_HEADS, B_DK, 1), lambda i: (i, 0, 0, 0))
    rowspec = pl.BlockSpec((tb, B_HEADS, 1, B_DV), lambda i: (i, 0, 0, 0))
    return pl.pallas_call(
        _gla_step_kernel,
        grid=(bx // tb,),
        in_specs=[col, col, col, rowspec, rowspec,
                  pl.BlockSpec((1, B_DV), lambda i: (0, 0)),
                  pl.BlockSpec((None, tb, B_HEADS, B_DK, B_DV), lambda i: (li, i, 0, 0, 0))],
        out_specs=[rowspec, pl.BlockSpec((tb, B_HEADS, B_DK, B_DV), lambda i: (i, 0, 0, 0))],
        out_shape=[jax.ShapeDtypeStruct((bx, B_HEADS, 1, B_DV), F32),
                   jax.ShapeDtypeStruct((bx, B_HEADS, B_DK, B_DV), F32)],
        compiler_params=_params("parallel"),
        name="gla_step",
    )(q, k, g, v, bog, out_norm.reshape(1, B_DV), state)


def _dsa_sample_score_kernel(pt_ref, iq_ref, iw_ref, ikn_ref, *refs):
    pages, o_ref = refs[:-1], refs[-1]
    iq = iq_ref[...].astype(BF16)
    iw = iw_ref[...]
    for j, pg in enumerate(pages):
        s = _dot(iq[:, :C_IDX_DIM], pg[...].astype(BF16))
        o_ref[j:j + 1, :] = jnp.sum(jnp.maximum(s, 0.0) * iw, axis=0, keepdims=True)
    s_new = jnp.sum(iq.astype(F32) * ikn_ref[...].astype(BF16).astype(F32), axis=-1, keepdims=True)
    sc_new = jnp.sum(jnp.maximum(s_new, 0.0) * iw, axis=0, keepdims=True)
    np_ = len(pages)
    o_ref[np_:, :] = jnp.broadcast_to(sc_new, (o_ref.shape[0] - np_, LANES))


def dsa_sample_scores(page_table, iq3, iw3, ik_new3, pool_ik, li, n_rows):
    bx, n_pages = page_table.shape
    page_specs = [pl.BlockSpec((None, None, C_IDX_DIM, PAGE_SIZE),
                               functools.partial(lambda b, pt, j: (li, pt[b, j], 0, 0), j=j))
                  for j in range(n_pages)]
    return pl.pallas_call(
        _dsa_sample_score_kernel,
        grid_spec=pltpu.PrefetchScalarGridSpec(
            num_scalar_prefetch=1,
            grid=(bx,),
            in_specs=[pl.BlockSpec((None, C_IDX_HEADS, LANES), lambda b, pt: (b, 0, 0)),
                      pl.BlockSpec((None, C_IDX_HEADS, 1), lambda b, pt: (b, 0, 0)),
                      pl.BlockSpec((None, 1, LANES), lambda b, pt: (b, 0, 0))] + page_specs,
            out_specs=pl.BlockSpec((None, n_rows, LANES), lambda b, pt: (b, 0, 0)),
        ),
        out_shape=jax.ShapeDtypeStruct((bx, n_rows, LANES), F32),
        compiler_params=_params("parallel"),
        name="dsa_sample_scores",
    )(page_table, iq3, iw3, ik_new3, *([pool_ik] * n_pages))


def _select_rows_kernel(sc_ref, o_ref, key_ref, *, topk, idx_bits, n_keys):
    width = sc_ref.shape[1]
    col = lax.broadcasted_iota(jnp.int32, (1, width), 1)
    key_ref[...] = jnp.where(col < n_keys, _sortable_key(sc_ref[...]), INT_MIN)

    def write(c0, val):
        o_ref[:, pl.ds(c0, SEL_CHUNK)] = val

    _threshold_select(key_ref, width // SEL_CHUNK, topk, idx_bits, n_keys - 1, write)


def dsa_select_sample(scores, n_keys):
    bx, width = scores.shape
    assert width % SEL_CHUNK == 0
    topk = min(C_TOPK_MAX, n_keys // 4)
    return pl.pallas_call(
        functools.partial(_select_rows_kernel, topk=topk, idx_bits=max(1, (width - 1).bit_length()),
                          n_keys=n_keys),
        grid=(1,),
        in_specs=[pl.BlockSpec((bx, width), lambda i: (0, 0))],
        out_specs=pl.BlockSpec((bx, width), lambda i: (0, 0)),
        out_shape=jax.ShapeDtypeStruct((bx, width), F32),
        scratch_shapes=[pltpu.VMEM((bx, width), jnp.int32)],
        compiler_params=_params("arbitrary"),
        name="dsa_select_sample",
    )(scores)


def dsa_sample_bias(rel_bias, past):
    n_pages = past // PAGE_SIZE
    bias = rel_bias[_rel_bucket(past - jnp.arange(past))][:, :C_HEADS].astype(F32)
    b0 = rel_bias[_rel_bucket(jnp.zeros((), jnp.int32))][:C_HEADS].astype(F32)[:, None]
    return bias.reshape(n_pages, PAGE_SIZE, C_HEADS).transpose(0, 2, 1), b0


def _lane_tiling_matrix():
    return jnp.asarray(np.tile(np.eye(HEAD_DIM, dtype=np.float32), (1, C_KV_HEADS)), BF16)


def _dsa_sample_attn_kernel(pt_ref, q_ref, kn_ref, vn_ref, mask_ref, bias_ref, b0_ref, e_ref, *refs):
    n_pages = (len(refs) - 1) // 2
    k_pages, v_pages, o_ref = refs[:n_pages], refs[n_pages:2 * n_pages], refs[-1]
    kvw = C_KV_HEADS * HEAD_DIM
    grp_shift = (C_HEADS // C_KV_HEADS).bit_length() - 1
    own = (jnp.right_shift(lax.broadcasted_iota(jnp.int32, (C_HEADS, kvw), 1), HEAD_DIM.bit_length() - 1)
           == jnp.right_shift(lax.broadcasted_iota(jnp.int32, (C_HEADS, kvw), 0), grp_shift))
    q = q_ref[...]
    q_bd = jnp.where(own, _dot(q.astype(BF16), e_ref[...]), 0.0).astype(BF16)
    s_new = (jnp.sum(q * kn_ref[...], axis=-1, keepdims=True) + b0_ref[...]
             + mask_ref[n_pages:n_pages + 1, 0:1])
    scores = [_dot(q_bd, k_pages[j][...].astype(BF16)) + bias_ref[j] + mask_ref[j:j + 1, :]
              for j in range(n_pages)]
    m = s_new
    for s in scores:
        m = jnp.maximum(m, jnp.max(s, axis=1, keepdims=True))
    p_new = jnp.exp(s_new - m)
    l = p_new
    acc = jnp.zeros((C_HEADS, kvw), F32)
    for j, s in enumerate(scores):
        p = jnp.exp(s - m)
        l = l + jnp.sum(p, axis=1, keepdims=True)
        acc = acc + _dot_nt(p.astype(BF16), v_pages[j][...].astype(BF16))
    acc = jnp.where(own, acc, 0.0)
    o = acc[:, :HEAD_DIM]
    for g in range(1, C_KV_HEADS):
        o = o + acc[:, g * HEAD_DIM:(g + 1) * HEAD_DIM]
    o_ref[...] = (o + p_new * vn_ref[...]) / l


def dsa_sample_attn(page_table, q3, k_new16, v_new16, mask3, bias3, b0, pool_kt, pool_vt, li):
    bx, n_pages = page_table.shape
    kvw = C_KV_HEADS * HEAD_DIM
    hspec = pl.BlockSpec((None, C_HEADS, HEAD_DIM), lambda b, pt: (b, 0, 0))
    page_specs = [pl.BlockSpec((None, None, kvw, PAGE_SIZE),
                               functools.partial(lambda b, pt, j: (li, pt[b, j], 0, 0), j=j))
                  for j in range(n_pages)]
    return pl.pallas_call(
        _dsa_sample_attn_kernel,
        grid_spec=pltpu.PrefetchScalarGridSpec(
            num_scalar_prefetch=1,
            grid=(bx,),
            in_specs=[hspec, hspec, hspec,
                      pl.BlockSpec((None,) + mask3.shape[1:], lambda b, pt: (b, 0, 0)),
                      pl.BlockSpec(bias3.shape, lambda b, pt: (0, 0, 0)),
                      pl.BlockSpec(b0.shape, lambda b, pt: (0, 0)),
                      pl.BlockSpec((HEAD_DIM, kvw), lambda b, pt: (0, 0))] + page_specs + page_specs,
            out_specs=hspec,
        ),
        out_shape=jax.ShapeDtypeStruct((bx, C_HEADS, HEAD_DIM), F32),
        compiler_params=_params("parallel"),
        name="dsa_sample_attn",
    )(page_table, q3, k_new16, v_new16, mask3, bias3, b0, _lane_tiling_matrix(),
      *([pool_kt] * n_pages), *([pool_vt] * n_pages))


def sample_forward(x, p, cache_a_k, cache_a_v, state_b, cache_c_k, cache_c_v, cache_c_ik, page_table, w,
                   rel_bias):
    bx, t_len, d = x.shape
    assert t_len == 1
    hd = A_HEADS * HEAD_DIM
    wb = cache_a_k.shape[2]
    n_pages = page_table.shape[1]
    past = n_pages * PAGE_SIZE
    n_rows = -(-(past + 1) // (2 * LANES)) * 2
    dil_bias, dil_b0 = dilated_sample_bias(rel_bias, wb)
    dsa_bias, dsa_b0 = dsa_sample_bias(rel_bias, past)
    cache_a_kt = cache_a_k.transpose(0, 1, 3, 4, 2)
    cache_a_vt = cache_a_v.transpose(0, 1, 3, 4, 2)
    pool_shape = cache_c_k.shape[:2] + (C_KV_HEADS * HEAD_DIM, PAGE_SIZE)
    pool_kt = cache_c_k.transpose(0, 1, 3, 4, 2).reshape(pool_shape)
    pool_vt = cache_c_v.transpose(0, 1, 3, 4, 2).reshape(pool_shape)
    pool_ikt = cache_c_ik.transpose(0, 1, 3, 2)
    grp = C_HEADS // C_KV_HEADS
    xf = x.reshape(bx, d)
    a_k, a_v, b_s, c_k, c_v, c_ik = [], [], [], [], [], []
    for l in range(len(w["ffn"])):
        li = l // 2
        if l % 2 == 0:
            ab = w["ab"][li]
            qk = norm_proj(xf, w["g_mix"][l], ab["w1"], ab["s1"], head_norm=True, tm=bx)
            y = norm_proj(xf, w["g_mix"][l], ab["wy"], ab["sy"], head_norm=False, tm=bx)
            v_new = y[:, :hd].reshape(bx, A_HEADS, HEAD_DIM)
            acol = lambda a: a.reshape(bx, A_HEADS, HEAD_DIM, 1)
            oa = dilated_sample(acol(qk[:, :hd]), acol(qk[:, hd:]), acol(y[:, :hd]), cache_a_kt, cache_a_vt, li,
                                dil_bias, dil_b0)
            g = gla_gate(y, ab["w2p"], w["b_gate_b"][li])
            nk = B_HEADS * B_DK
            colv = lambda a: a.reshape(bx, B_HEADS, B_DK, 1)
            rowv = lambda a: a.reshape(bx, B_HEADS, 1, B_DV)
            ob, s_fin = gla_step(colv(y[:, hd:hd + nk]), colv(y[:, hd + nk:hd + 2 * nk]), colv(g),
                                 rowv(y[:, 2 * hd:2 * hd + B_HEADS * B_DV]),
                                 rowv(y[:, 2 * hd + B_HEADS * B_DV:2 * hd + 2 * B_HEADS * B_DV]),
                                 w["b_out_norm"][li], state_b, li, tb=8)
            xf = out_proj([oa.reshape(bx, hd), ob.reshape(bx, B_HEADS * B_DV)], [ab["wo_a"], ab["wo_b"]], xf, tm=bx)
            a_k.append(qk[:, hd:].reshape(bx, 1, A_HEADS, HEAD_DIM))
            a_v.append(v_new.reshape(bx, 1, A_HEADS, HEAD_DIM))
            b_s.append(s_fin)
        else:
            c = w["c"][li]
            y1 = norm_proj(xf, w["g_mix"][l], c["w1"], c["s1"], head_norm=True, tm=bx)
            y2 = norm_proj(xf, w["g_mix"][l], c["w2"], c["s2"], head_norm=False, tm=bx)
            k0 = C_KV_COL
            k_new = y1[:, C_KV_COL:C_IK_COL].reshape(bx, C_KV_HEADS, HEAD_DIM)
            v_new = y2[:, C_KV_COL:C_IK_COL].reshape(bx, C_KV_HEADS, HEAD_DIM)
            scores = dsa_sample_scores(page_table, y2[:, :C_IDX_HEADS * LANES].reshape(bx, C_IDX_HEADS, LANES),
                                       y2[:, C_IW_COL:C_IW_COL + C_IDX_HEADS].reshape(bx, C_IDX_HEADS, 1),
                                       y1[:, C_IK_COL:C_IK_COL + LANES].reshape(bx, 1, LANES),
                                       pool_ikt, li, n_rows)
            mask = dsa_select_sample(scores.reshape(bx, n_rows * LANES), past + 1)
            oc = dsa_sample_attn(page_table, y1[:, :k0].reshape(bx, C_HEADS, HEAD_DIM),
                                 jnp.repeat(k_new, grp, axis=1), jnp.repeat(v_new, grp, axis=1),
                                 mask.reshape(bx, n_rows, LANES), dsa_bias, dsa_b0, pool_kt, pool_vt, li)
            xf = out_proj([oc.reshape(bx, k0)], [c["wo"]], xf, tm=bx)
            c_k.append(k_new.reshape(bx, 1, C_KV_HEADS, HEAD_DIM))
            c_v.append(v_new.reshape(bx, 1, C_KV_HEADS, HEAD_DIM))
            c_ik.append(y1[:, C_IK_COL:C_IK_COL + C_IDX_DIM].reshape(bx, 1, C_IDX_DIM))
        f = w["ffn"][l]
        xf = ffn_ple(xf, p[l].reshape(bx, -1), f["gf"], f["w1"], f["w3"], f["w2"], f["gp"], f["wg"], f["wp"],
                     tm=bx, tf=256)
    return (xf.reshape(bx, 1, d), jnp.stack(a_k), jnp.stack(a_v), jnp.stack(b_s),
            jnp.stack(c_k), jnp.stack(c_v), jnp.stack(c_ik))


def prep_weights(rel_bias, g_mix, w_in_ab, a_q_norm, a_k_norm, b_gate_w2, b_gate_b, b_out_norm, w_out_ab,
                 w_in_c, c_q_norm, c_k_norm, c_ik_norm, w_out_c, g_ffn, w_ff1, w_ff3, w_ff2, g_ple,
                 w_ple_gate, w_ple_proj):
    w = dict(g_mix=g_mix, b_gate_b=b_gate_b, b_out_norm=b_out_norm)
    w["ab"] = [prep_ab(w_in_ab[i], a_q_norm[i], a_k_norm[i], b_gate_w2[i], w_out_ab[i])
               for i in range(w_in_ab.shape[0])]
    w["c"] = [prep_c(w_in_c[i], c_q_norm[i], c_k_norm[i], c_ik_norm[i], w_out_c[i])
              for i in range(w_in_c.shape[0])]
    w["ffn"] = [dict(gf=g_ffn[l], w1=w_ff1[l].astype(BF16), w3=w_ff3[l].astype(BF16), w2=w_ff2[l].astype(BF16),
                     gp=g_ple[l], wg=w_ple_gate[l].astype(BF16), wp=w_ple_proj[l].astype(BF16))
                for l in range(g_ffn.shape[0])]
    tables = dict(dil=dilated_bias_table(rel_bias), dsa=dsa_bias_table(rel_bias))
    return w, tables


def kernel(x_prompt, x_sample, cache_a_k, cache_a_v, state_b, cache_c_k, cache_c_v, cache_c_ik, page_table,
           p_prompt, p_sample, rel_bias, g_mix, w_in_ab, a_q_norm, a_k_norm, b_gate_w2, b_gate_b, b_out_norm,
           w_out_ab, w_in_c, c_q_norm, c_k_norm, c_ik_norm, w_out_c, g_ffn, w_ff1, w_ff3, w_ff2, g_ple,
           w_ple_gate, w_ple_proj):
    w, tables = prep_weights(rel_bias, g_mix, w_in_ab, a_q_norm, a_k_norm, b_gate_w2, b_gate_b, b_out_norm,
                             w_out_ab, w_in_c, c_q_norm, c_k_norm, c_ik_norm, w_out_c, g_ffn, w_ff1, w_ff3,
                             w_ff2, g_ple, w_ple_gate, w_ple_proj)
    prompt = prompt_forward(x_prompt, p_prompt, w, tables)
    sample = sample_forward(x_sample, p_sample, cache_a_k, cache_a_v, state_b, cache_c_k, cache_c_v,
                            cache_c_ik, page_table, w, rel_bias)
    return (prompt[0], sample[0]) + tuple(prompt[1:]) + tuple(sample[1:])
```

```python
import functools
import math

import jax
import jax.numpy as jnp
import numpy as np
from jax import lax
from jax.experimental import pallas as pl
from jax.experimental.pallas import tpu as pltpu

F32 = jnp.float32
BF16 = jnp.bfloat16

LANES = 128
HEAD_DIM = 64
A_HEADS = 8
A_BRANCHES = ((128, 1), (512, 4), (2048, 16))
A_WIN_MAX = 2048
B_HEADS = 4
B_DK = 64
B_DV = 128
B_GATE_RANK = 16
B_GATE_TAU = 16.0
C_HEADS = 16
C_KV_HEADS = 4
C_IDX_HEADS = 8
C_IDX_DIM = 64
C_TOPK_MAX = 256
N_BUCKETS = 32
REL_MAX_DIST = A_WIN_MAX
Q_BLOCK = 128
PAGE_SIZE = 128
NORM_EPS = 1e-6
NEG = -1e30
LOG2E = math.log2(math.e)
VMEM_LIMIT = 56 * 1024 * 1024


def _params(*sem):
    return pltpu.CompilerParams(dimension_semantics=sem, vmem_limit_bytes=VMEM_LIMIT)


def _rms(x, g):
    ms = jnp.mean(x * x, axis=-1, keepdims=True)
    return x * lax.rsqrt(ms + NORM_EPS) * g


def _dot(a, b):
    return jnp.dot(a, b, preferred_element_type=F32)


def _dot_nt(a, b):
    return lax.dot_general(a, b, (((1,), (1,)), ((), ())), preferred_element_type=F32)


def _dot_tn(a, b):
    return lax.dot_general(a, b, (((0,), (0,)), ((), ())), preferred_element_type=F32)


PROJ_CHUNK = 256


def _proj_kernel(x_ref, g_ref, w_ref, cs_ref, p_ref, o_ref, *, head_norm):
    hn = _rms(x_ref[...], g_ref[...]).astype(BF16)
    for c in range(o_ref.shape[1] // PROJ_CHUNK):
        sl = slice(c * PROJ_CHUNK, (c + 1) * PROJ_CHUNK)
        y = _dot(hn, w_ref[:, sl])
        if head_norm:
            ms = _dot((y * y).astype(BF16), p_ref[...])
            y = y * lax.rsqrt(ms + NORM_EPS)
        o_ref[:, sl] = y * cs_ref[:, sl]


def _group_mean_matrix():
    r = np.arange(PROJ_CHUNK) // HEAD_DIM
    return jnp.asarray((r[:, None] == r[None, :]).astype(np.float32) / HEAD_DIM, BF16)


def norm_proj(x, g, w, colscale, *, head_norm, tm):
    n, d = x.shape
    dout = w.shape[1]
    assert n % tm == 0 and dout % PROJ_CHUNK == 0
    return pl.pallas_call(
        functools.partial(_proj_kernel, head_norm=head_norm),
        grid=(n // tm,),
        in_specs=[
            pl.BlockSpec((tm, d), lambda i: (i, 0)),
            pl.BlockSpec((1, d), lambda i: (0, 0)),
            pl.BlockSpec((d, dout), lambda i: (0, 0)),
            pl.BlockSpec((1, dout), lambda i: (0, 0)),
            pl.BlockSpec((PROJ_CHUNK, PROJ_CHUNK), lambda i: (0, 0)),
        ],
        out_specs=pl.BlockSpec((tm, dout), lambda i: (i, 0)),
        out_shape=jax.ShapeDtypeStruct((n, dout), F32),
        compiler_params=_params("parallel"),
        name="norm_proj_hn" if head_norm else "norm_proj",
    )(x, g.reshape(1, d), w, colscale.reshape(1, dout), _group_mean_matrix())


def _out_kernel(*refs, n_in):
    res_ref, o_ref = refs[2 * n_in], refs[2 * n_in + 1]
    acc = res_ref[...]
    for a_ref, w_ref in zip(refs[:n_in], refs[n_in:2 * n_in]):
        acc = acc + _dot(a_ref[...].astype(BF16), w_ref[...])
    o_ref[...] = acc


def out_proj(a_list, w_list, res, *, tm):
    n, d = res.shape
    n_in = len(a_list)
    in_specs = [pl.BlockSpec((tm, a.shape[1]), lambda i: (i, 0)) for a in a_list]
    in_specs += [pl.BlockSpec(w.shape, lambda i: (0, 0)) for w in w_list]
    in_specs += [pl.BlockSpec((tm, d), lambda i: (i, 0))]
    return pl.pallas_call(
        functools.partial(_out_kernel, n_in=n_in),
        grid=(n // tm,),
        in_specs=in_specs,
        out_specs=pl.BlockSpec((tm, d), lambda i: (i, 0)),
        out_shape=jax.ShapeDtypeStruct((n, d), F32),
        compiler_params=_params("parallel"),
        name="out_proj",
    )(*a_list, *w_list, res)


def _ffn_kernel(x_ref, gf_ref, w1_ref, w3_ref, w2_ref, gp_ref, wg_ref, p_ref, wp_ref, o_ref,
                hn_ref, acc_ref):
    f = pl.program_id(1)

    @pl.when(f == 0)
    def _():
        hn_ref[...] = _rms(x_ref[...], gf_ref[...]).astype(BF16)
        acc_ref[...] = jnp.zeros_like(acc_ref)

    hn = hn_ref[...]
    h1 = _dot(hn, w1_ref[...])
    h3 = _dot(hn, w3_ref[...])
    a = h1 * jax.nn.sigmoid(h1) * h3
    acc_ref[...] += _dot(a.astype(BF16), w2_ref[...])

    @pl.when(f == pl.num_programs(1) - 1)
    def _():
        x2 = x_ref[...] + acc_ref[...]
        u = _rms(x2, gp_ref[...]).astype(BF16)
        gate = jax.nn.sigmoid(_dot(u, wg_ref[...]))
        o_ref[...] = x2 + gate * _dot(p_ref[...].astype(BF16), wp_ref[...])


FFN_ROWS = 1024


def ffn_ple(x, p, gf, w1, w3, w2, gp, wg, wp, *, tm, tf):
    n, d = x.shape
    dff = w1.shape[1]
    dple = p.shape[1]
    assert n % tm == 0 and dff % tf == 0
    return pl.pallas_call(
        _ffn_kernel,
        grid=(n // tm, dff // tf),
        in_specs=[
            pl.BlockSpec((tm, d), lambda i, f: (i, 0)),
            pl.BlockSpec((1, d), lambda i, f: (0, 0)),
            pl.BlockSpec((d, tf), lambda i, f: (0, f)),
            pl.BlockSpec((d, tf), lambda i, f: (0, f)),
            pl.BlockSpec((tf, d), lambda i, f: (f, 0)),
            pl.BlockSpec((1, d), lambda i, f: (0, 0)),
            pl.BlockSpec((d, d), lambda i, f: (0, 0)),
            pl.BlockSpec((tm, dple), lambda i, f: (i, 0)),
            pl.BlockSpec((dple, d), lambda i, f: (0, 0)),
        ],
        out_specs=pl.BlockSpec((tm, d), lambda i, f: (i, 0)),
        out_shape=jax.ShapeDtypeStruct((n, d), F32),
        scratch_shapes=[pltpu.VMEM((tm, d), BF16), pltpu.VMEM((tm, d), F32)],
        compiler_params=_params("parallel", "arbitrary"),
        name="ffn_ple",
    )(x, gf.reshape(1, d), w1, w3, w2, gp.reshape(1, d), wg, p, wp)


def _rel_bucket(dist):
    n = jnp.maximum(dist, 0)
    exact = N_BUCKETS // 2
    nf = jnp.maximum(n, exact).astype(F32)
    large = exact + (jnp.log(nf / exact) / math.log(REL_MAX_DIST / exact)
                     * (N_BUCKETS - exact)).astype(jnp.int32)
    return jnp.where(n < exact, n, jnp.minimum(large, N_BUCKETS - 1))


def _branch_multiplicity(delta):
    mult = jnp.zeros(delta.shape, jnp.int32)
    for w, d in A_BRANCHES:
        mult = mult + ((delta >= 0) & (delta <= w) & (delta % d == 0)).astype(jnp.int32)
    return mult


DIL_WIN_CHUNKS = A_WIN_MAX // Q_BLOCK + 1


def _toeplitz(value_of_delta, base, width):
    period = width + Q_BLOCK - 1
    x = jnp.concatenate([jnp.arange(width), jnp.arange(-(Q_BLOCK - 1), 0)])
    v = value_of_delta(base - x)
    t = jnp.tile(v, (1, Q_BLOCK))[:, :Q_BLOCK * (period - 1)]
    return t.reshape(v.shape[0], Q_BLOCK, period - 1)[:, :, :width]


def dilated_bias_table(rel_bias):
    def value(delta):
        mult = _branch_multiplicity(delta)
        bias = rel_bias[_rel_bucket(delta)][:, :A_HEADS].astype(F32).T
        return jnp.where(mult[None] > 0, bias + jnp.log(jnp.maximum(mult, 1).astype(F32))[None], NEG)

    t = _toeplitz(value, A_WIN_MAX, DIL_WIN_CHUNKS * Q_BLOCK)
    t = t.reshape(A_HEADS // 2, 2 * Q_BLOCK, DIL_WIN_CHUNKS * Q_BLOCK)
    return jnp.pad(t * LOG2E, ((0, 0), (0, 0), (0, Q_BLOCK)), constant_values=NEG)


DSA_BIAS_TILES = 14


def dsa_bias_table(rel_bias):
    def value(delta):
        return rel_bias[_rel_bucket(delta)][:, :C_HEADS].astype(F32).T

    last = DSA_BIAS_TILES - 1
    t = _toeplitz(value, last * Q_BLOCK, DSA_BIAS_TILES * Q_BLOCK)
    grp = C_HEADS // C_KV_HEADS
    t = t.reshape(C_KV_HEADS, grp, Q_BLOCK, DSA_BIAS_TILES, Q_BLOCK)[:, :, :, ::-1, :]
    return (t * LOG2E).transpose(0, 3, 4, 1, 2).reshape(C_KV_HEADS, DSA_BIAS_TILES, Q_BLOCK, grp * Q_BLOCK)


def _lane_lt64():
    return lax.broadcasted_iota(jnp.int32, (1, LANES), 1) < HEAD_DIM


DIL_SUB_BLOCKS = 4


def _dil_kernel(q_ref, k_ref, v_ref, t_ref, o_ref, s_ref, p_ref):
    nw = s_ref.shape[2] // Q_BLOCK
    last = DIL_WIN_CHUNKS - 1
    lo = _lane_lt64()
    for u in range(DIL_SUB_BLOCKS):
        i = pl.program_id(2) * DIL_SUB_BLOCKS + u
        qrows = slice(u * Q_BLOCK, (u + 1) * Q_BLOCK)
        q = q_ref[qrows, :] * LOG2E
        q2 = jnp.concatenate([jnp.where(lo, q, 0.0), jnp.where(lo, 0.0, q)], axis=0).astype(BF16)
        w0 = jnp.maximum(i - (nw - 1), 0)
        rows = pl.ds(pl.multiple_of(w0 * Q_BLOCK, Q_BLOCK), nw * Q_BLOCK)
        s_ref[u] = _dot_nt(q2, k_ref[rows, :].astype(BF16))
        m = jnp.full((2 * Q_BLOCK, LANES), NEG, F32)
        for w in range(nw):
            c = last - i + w0 + w
            tcol = pl.multiple_of(jnp.where(c <= last, c, last + 1) * Q_BLOCK, Q_BLOCK)
            cols = slice(w * Q_BLOCK, (w + 1) * Q_BLOCK)
            s = s_ref[u, :, cols] + t_ref[0, :, pl.ds(tcol, Q_BLOCK)]
            s_ref[u, :, cols] = s
            m = jnp.maximum(m, s)
        m = jnp.max(m, axis=1, keepdims=True)
        l = jnp.zeros((2 * Q_BLOCK, LANES), F32)
        for w in range(nw):
            cols = slice(w * Q_BLOCK, (w + 1) * Q_BLOCK)
            p = jnp.exp2(s_ref[u, :, cols] - m)
            l = l + p
            p_ref[u, :, cols] = p.astype(BF16)
        o = _dot(p_ref[u], v_ref[rows, :].astype(BF16)) / jnp.sum(l, axis=1, keepdims=True)
        o_ref[qrows, :] = jnp.where(lo, o[:Q_BLOCK], o[Q_BLOCK:])


def dilated_prompt(qk, yv, table, batch, seq):
    n = batch * seq
    nsub = DIL_SUB_BLOCKS
    assert seq % (nsub * Q_BLOCK) == 0
    nstep = seq // (nsub * Q_BLOCK)
    npair = A_HEADS // 2
    win = min(DIL_WIN_CHUNKS, seq // Q_BLOCK) * Q_BLOCK
    return pl.pallas_call(
        _dil_kernel,
        grid=(batch, npair, nstep),
        in_specs=[
            pl.BlockSpec((nsub * Q_BLOCK, LANES), lambda b, p, i: (b * nstep + i, p)),
            pl.BlockSpec((seq, LANES), lambda b, p, i: (b, npair + p)),
            pl.BlockSpec((seq, LANES), lambda b, p, i: (b, p)),
            pl.BlockSpec((1,) + table.shape[1:], lambda b, p, i: (p, 0, 0)),
        ],
        out_specs=pl.BlockSpec((nsub * Q_BLOCK, LANES), lambda b, p, i: (b * nstep + i, p)),
        out_shape=jax.ShapeDtypeStruct((n, A_HEADS * HEAD_DIM), F32),
        scratch_shapes=[pltpu.VMEM((nsub, 2 * Q_BLOCK, win), F32), pltpu.VMEM((nsub, 2 * Q_BLOCK, win), BF16)],
        compiler_params=_params("parallel", "parallel", "arbitrary"),
        name="dilated_prompt",
    )(qk, qk, yv, table)


GLA_CHUNK = 64
GLA_SUB = 16
GLA_EXP_CLAMP = 60.0


def _log_sigmoid(z):
    return jnp.minimum(z, 0.0) - jnp.log1p(jnp.exp(-jnp.abs(z)))


def _gla_kernel(bq_ref, bk_ref, bv_ref, bog_ref, blr_ref, w2_ref, gb_ref, on_ref, o_ref, s_ref, st_ref):
    it = pl.program_id(1)

    @pl.when(it == 0)
    def _():
        st_ref[...] = jnp.zeros_like(st_ref)

    lo = _lane_lt64()
    ch = GLA_CHUNK
    row = lax.broadcasted_iota(jnp.int32, (ch, ch), 0)
    col = lax.broadcasted_iota(jnp.int32, (ch, ch), 1)
    causal = row >= col
    ltri = jnp.where(causal, 1.0, 0.0).astype(BF16)
    top_half = lax.broadcasted_iota(jnp.int32, (LANES, LANES), 0) < HEAD_DIM

    def chunk_body(c, carry):
        rows = pl.ds(pl.multiple_of(c * ch, ch), ch)
        for bi, p in [(bi, p) for bi in range(bq_ref.shape[0]) for p in range(B_HEADS // 2)]:
            sl = slice(p * LANES, (p + 1) * LANES)
            z = _dot(blr_ref[bi, rows, :].astype(BF16), w2_ref[:, sl]) + gb_ref[:, sl]
            g = _log_sigmoid(z) / B_GATE_TAU
            q = bq_ref[bi, rows, sl]
            k = bk_ref[bi, rows, sl]
            g_hi = g.astype(BF16)
            g_lo = (g - g_hi.astype(F32)).astype(BF16)
            cum = _dot(ltri, g_hi) + _dot(ltri, g_lo)
            last = cum[ch - 1:ch, :]
            st = st_ref[bi, p]
            qd = q * jnp.exp(cum)
            q2 = jnp.concatenate([jnp.where(lo, qd, 0.0), jnp.where(lo, 0.0, qd)], axis=0)
            o_inter = _dot(q2.astype(BF16), st.astype(BF16))
            atts = []
            for sb in range(ch // GLA_SUB):
                rs = slice(sb * GLA_SUB, (sb + 1) * GLA_SUB)
                ref_row = cum[sb * GLA_SUB:sb * GLA_SUB + 1, :]
                qs = q[rs] * jnp.exp(cum[rs] - ref_row)
                ks = k * jnp.exp(jnp.minimum(ref_row - cum, GLA_EXP_CLAMP))
                qq = jnp.concatenate([jnp.where(lo, qs, 0.0), jnp.where(lo, 0.0, qs)], axis=0)
                atts.append(_dot_nt(qq.astype(BF16), ks.astype(BF16)))
            kd = (k * jnp.exp(last - cum)).astype(BF16)
            upd = []
            for e in range(2):
                hs = slice((2 * p + e) * LANES, (2 * p + e + 1) * LANES)
                v = bv_ref[bi, rows, hs].astype(BF16)
                att = jnp.concatenate([a[e * GLA_SUB:(e + 1) * GLA_SUB] for a in atts], axis=0)
                att = jnp.where(causal, att, 0.0)
                o = o_inter[e * ch:(e + 1) * ch] + _dot(att.astype(BF16), v)
                og = _rms(o, on_ref[...])
                gate = bog_ref[bi, rows, hs]
                o_ref[bi, rows, hs] = og * (gate * jax.nn.sigmoid(gate))
                upd.append(_dot_tn(kd, v))
            decay = jnp.transpose(jnp.broadcast_to(jnp.exp(last), (LANES, LANES)))
            st_ref[bi, p] = decay * st + jnp.where(top_half, upd[0], upd[1])
        return carry

    lax.fori_loop(0, bq_ref.shape[1] // ch, chunk_body, 0)

    @pl.when(it == pl.num_programs(1) - 1)
    def _():
        for bi in range(s_ref.shape[0]):
            for p in range(B_HEADS // 2):
                s_ref[bi, 2 * p] = st_ref[bi, p, :HEAD_DIM, :]
                s_ref[bi, 2 * p + 1] = st_ref[bi, p, HEAD_DIM:, :]


GLA_BATCH = 4


def gla_prompt(y, w2p, gate_b, out_norm, batch, seq, *, tg):
    nb = GLA_BATCH if batch % GLA_BATCH == 0 else 1
    nt = seq // tg
    dv = B_HEADS * B_DV
    y3 = y.reshape(batch, seq, y.shape[1])
    o, s_fin = pl.pallas_call(
        _gla_kernel,
        grid=(batch // nb, nt),
        in_specs=[
            pl.BlockSpec((nb, tg, 256), lambda b, t: (b, t, 2)),
            pl.BlockSpec((nb, tg, 256), lambda b, t: (b, t, 3)),
            pl.BlockSpec((nb, tg, dv), lambda b, t: (b, t, 2)),
            pl.BlockSpec((nb, tg, dv), lambda b, t: (b, t, 3)),
            pl.BlockSpec((nb, tg, LANES), lambda b, t: (b, t, 16)),
            pl.BlockSpec((LANES, 256), lambda b, t: (0, 0)),
            pl.BlockSpec((1, 256), lambda b, t: (0, 0)),
            pl.BlockSpec((1, B_DV), lambda b, t: (0, 0)),
        ],
        out_specs=[
            pl.BlockSpec((nb, tg, dv), lambda b, t: (b, t, 0)),
            pl.BlockSpec((nb, B_HEADS, B_DK, B_DV), lambda b, t: (b, 0, 0, 0)),
        ],
        out_shape=[jax.ShapeDtypeStruct((batch, seq, dv), F32),
                   jax.ShapeDtypeStruct((batch, B_HEADS, B_DK, B_DV), F32)],
        scratch_shapes=[pltpu.VMEM((nb, B_HEADS // 2, LANES, LANES), F32)],
        compiler_params=_params("parallel", "arbitrary"),
        name="gla_prompt",
    )(y3, y3, y3, y3, y3, w2p, gate_b.reshape(1, 256), out_norm.reshape(1, B_DV))
    return o.reshape(batch * seq, dv), s_fin


SEL_CHUNK = 256
INT_MIN = -2 ** 31


def _sortable_key(score):
    bits = pltpu.bitcast(score + 0.0, jnp.int32)
    return jnp.where(bits < 0, bits ^ jnp.int32(0x7FFFFFFF), bits)


def _threshold_select(key_ref, n_ch, topk, idx_bits, row_limit, write):
    r = key_ref.shape[0]
    kc = SEL_CHUNK
    lane = lax.broadcasted_iota(jnp.int32, (1, kc), 1)

    def count(hits):
        def body(c, acc):
            c0 = pl.multiple_of(c * kc, kc)
            hit = hits(key_ref[:, pl.ds(c0, kc)], c0 + lane)
            for t in range(kc // LANES):
                acc = acc + hit[:, t * LANES:(t + 1) * LANES]
            return acc
        acc = lax.fori_loop(0, n_ch, body, jnp.zeros((r, LANES), jnp.int32))
        return jnp.sum(acc, axis=1, keepdims=True)

    n_nonneg = count(lambda k, _: jnp.where(k >= 0, 1, 0))
    base = jnp.where(n_nonneg >= topk, 0, INT_MIN).astype(jnp.int32)
    n_ge = jnp.where(n_nonneg >= topk, n_nonneg, n_ch * kc)

    def bit_body(b, carry):
        base, n_ge = carry
        cand = base | jnp.left_shift(jnp.int32(1), 30 - b)
        cnt = count(lambda k, _: jnp.where(k >= cand, 1, 0))
        return jnp.where(cnt >= topk, cand, base), jnp.where(cnt >= topk, cnt, n_ge)

    tau, n_ge = lax.fori_loop(0, 31, bit_body, (base, n_ge))

    def tie_break():
        need = topk - count(lambda k, _: jnp.where(k > tau, 1, 0))

        def idx_body(b, lo):
            cand = lo | jnp.left_shift(jnp.int32(1), idx_bits - 1 - b)
            cnt = count(lambda k, col: jnp.where(k == tau, jnp.where(col < cand, 1, 0), 0))
            return jnp.where(cnt < need, cand, lo)

        return lax.fori_loop(0, idx_bits, idx_body, jnp.zeros((r, 1), jnp.int32))

    last_eq = lax.cond(jnp.max(n_ge) > topk, tie_break,
                       lambda: jnp.full((r, 1), 2 ** idx_bits, jnp.int32))

    def out_body(c, carry):
        c0 = pl.multiple_of(c * kc, kc)
        k = key_ref[:, pl.ds(c0, kc)]
        col = c0 + lane
        val = jnp.where(k > tau, 0.0, jnp.where(k == tau, jnp.where(col <= last_eq, 0.0, NEG), NEG))
        write(c0, jnp.where(col <= row_limit, val, NEG))
        return carry

    lax.fori_loop(0, n_ch, out_body, 0)


def _threshold_select_t(key_ref, n_ch, topk, idx_bits, t_idx, write):
    nq = key_ref.shape[1]
    kc = SEL_CHUNK
    rowi = lax.broadcasted_iota(jnp.int32, (kc, 1), 0)

    def count(hits):
        def body(c, acc):
            c0 = pl.multiple_of(c * kc, kc)
            hit = hits(key_ref[pl.ds(c0, kc), :], c0 + rowi)
            return acc + jnp.sum(hit.reshape(kc // 8, 8, nq), axis=0)
        acc = lax.fori_loop(0, n_ch, body, jnp.zeros((8, nq), jnp.int32))
        return jnp.sum(acc, axis=0, keepdims=True)

    n_nonneg = count(lambda k, _: jnp.where(k >= 0, 1, 0))
    base = jnp.where(n_nonneg >= topk, 0, INT_MIN).astype(jnp.int32)
    n_ge = jnp.where(n_nonneg >= topk, n_nonneg, n_ch * kc)

    def bit_body(b, carry):
        base, n_ge = carry
        cand = base | jnp.left_shift(jnp.int32(1), 30 - b)
        cnt = count(lambda k, _: jnp.where(k >= cand, 1, 0))
        return jnp.where(cnt >= topk, cand, base), jnp.where(cnt >= topk, cnt, n_ge)

    tau, n_ge = lax.fori_loop(0, 31, bit_body, (base, n_ge))

    def tie_break():
        need = topk - count(lambda k, _: jnp.where(k > tau, 1, 0))

        def idx_body(b, lo):
            cand = lo | jnp.left_shift(jnp.int32(1), idx_bits - 1 - b)
            cnt = count(lambda k, s: jnp.where(k == tau, jnp.where(s < cand, 1, 0), 0))
            return jnp.where(cnt < need, cand, lo)

        return lax.fori_loop(0, idx_bits, idx_body, jnp.zeros((1, nq), jnp.int32))

    last_eq = lax.cond(jnp.max(n_ge) > topk, tie_break,
                       lambda: jnp.full((1, nq), 2 ** idx_bits, jnp.int32))

    def out_body(c, carry):
        c0 = pl.multiple_of(c * kc, kc)
        k = key_ref[pl.ds(c0, kc), :]
        s = c0 + rowi
        val = jnp.where(k > tau, 0.0, jnp.where(k == tau, jnp.where(s <= last_eq, 0.0, NEG), NEG))
        write(c0, jnp.where(s <= t_idx, val, NEG))
        return carry

    lax.fori_loop(0, n_ch, out_body, 0)


SEL_QUERIES = 256


def _select_kernel(iq_ref, iwt_ref, ik_ref, o_ref, key_ref, qs_ref, *, topk, idx_bits):
    qb = pl.program_id(1)
    kc = SEL_CHUNK
    nq = SEL_QUERIES
    n_ch = (qb * nq + nq + kc - 1) // kc
    t_idx = qb * nq + lax.broadcasted_iota(jnp.int32, (1, nq), 1)
    rowi = lax.broadcasted_iota(jnp.int32, (kc, 1), 0)
    iwt = iwt_ref[...]
    for h in range(C_IDX_HEADS):
        qs_ref[h] = iq_ref[:, h * LANES:(h + 1) * LANES].astype(BF16)

    def score_body(c, carry):
        c0 = pl.multiple_of(c * kc, kc)
        ik = ik_ref[pl.ds(c0, kc), :].astype(BF16)
        sc = jnp.zeros((kc, nq), F32)
        for h in range(C_IDX_HEADS):
            sc = sc + jnp.maximum(_dot_nt(ik, qs_ref[h]), 0.0) * iwt[h:h + 1, :]
        key_ref[pl.ds(c0, kc), :] = jnp.where(c0 + rowi <= t_idx, _sortable_key(sc), INT_MIN)
        return carry

    lax.fori_loop(0, n_ch, score_body, 0)
    o_ref[...] = jnp.full(o_ref.shape, NEG, o_ref.dtype)

    def write(c0, val):
        o_ref[pl.ds(c0, kc), :] = val.astype(o_ref.dtype)

    _threshold_select_t(key_ref, n_ch, topk, idx_bits, t_idx, write)


def dsa_select_prompt(y2, iwt, y1, batch, seq):
    assert seq % SEL_CHUNK == 0 and seq % SEL_QUERIES == 0
    nqb = seq // SEL_QUERIES
    topk = min(C_TOPK_MAX, seq // 4)
    return pl.pallas_call(
        functools.partial(_select_kernel, topk=topk, idx_bits=max(1, (seq - 1).bit_length())),
        grid=(batch, nqb),
        in_specs=[
            pl.BlockSpec((SEL_QUERIES, C_IDX_HEADS * LANES), lambda b, i: (b * nqb + i, 0)),
            pl.BlockSpec((C_IDX_HEADS, SEL_QUERIES), lambda b, i: (0, b * nqb + i)),
            pl.BlockSpec((seq, LANES), lambda b, i: (b, 10)),
        ],
        out_specs=pl.BlockSpec((None, seq, SEL_QUERIES), lambda b, i: (b, 0, i)),
        out_shape=jax.ShapeDtypeStruct((batch, seq, seq), BF16),
        scratch_shapes=[pltpu.VMEM((seq, SEL_QUERIES), jnp.int32),
                        pltpu.VMEM((C_IDX_HEADS, SEL_QUERIES, LANES), BF16)],
        compiler_params=_params("parallel", "arbitrary"),
        name="dsa_select_prompt",
    )(y2, iwt, y1)


def _dsa_attn_kernel(q_ref, k_ref, vt_ref, mask_ref, tb_ref, o_ref, st_ref, p_ref, acc_ref):
    g = pl.program_id(1)
    i = pl.program_id(2)
    kc = SEL_CHUNK
    grp = C_HEADS // C_KV_HEADS
    lo = _lane_lt64()
    first = (g % 2) == 0
    lane_half = jnp.right_shift(lax.broadcasted_iota(jnp.int32, (1, LANES), 1), HEAD_DIM.bit_length() - 1)
    own_half = lane_half == g % 2
    nsub = q_ref.shape[0] // Q_BLOCK
    cols = grp * Q_BLOCK
    parts = []
    for u in range(nsub):
        for j in range(grp // 2):
            qp = q_ref[u * Q_BLOCK:(u + 1) * Q_BLOCK, j * LANES:(j + 1) * LANES]
            qr = pltpu.roll(qp, HEAD_DIM, 1)
            parts.append(jnp.where(own_half, jnp.where(first, qp, qr), 0.0))
            parts.append(jnp.where(own_half, jnp.where(first, qr, qp), 0.0))
    q4 = (jnp.concatenate(parts, axis=0) * LOG2E).astype(BF16)

    n_ch = (i * nsub * Q_BLOCK + nsub * Q_BLOCK + kc - 1) // kc

    acc_ref[...] = jnp.zeros_like(acc_ref)

    def body(c, carry):
        m, l = carry
        c0 = pl.multiple_of(c * kc, kc)
        st_ref[...] = _dot_nt(k_ref[pl.ds(c0, kc), :].astype(BF16), q4)
        m_out, l_out, alphas = [], [], []
        for u in range(nsub):
            for h in range(grp):
                lanes = slice((u * grp + h) * Q_BLOCK, (u * grp + h + 1) * Q_BLOCK)
                halves = []
                for hf in range(kc // Q_BLOCK):
                    e = jnp.clip(i * nsub + u - (c * (kc // Q_BLOCK) + hf), 0, DSA_BIAS_TILES - 1)
                    mk = mask_ref[pl.ds(c0 + hf * Q_BLOCK, Q_BLOCK), u * Q_BLOCK:(u + 1) * Q_BLOCK]
                    halves.append(st_ref[hf * Q_BLOCK:(hf + 1) * Q_BLOCK, lanes]
                                  + tb_ref[0, e, :, h * Q_BLOCK:(h + 1) * Q_BLOCK] + mk.astype(F32))
                s = jnp.concatenate(halves, axis=0)
                m_new = jnp.maximum(m[:, lanes], jnp.max(s, axis=0, keepdims=True))
                alpha = jnp.exp2(m[:, lanes] - m_new)
                p = jnp.exp2(s - m_new)
                p_ref[:, lanes] = p.astype(BF16)
                m_out.append(m_new)
                l_out.append(alpha * l[:, lanes] + jnp.sum(p, axis=0, keepdims=True))
                alphas.append(alpha)
        pv = _dot(vt_ref[:, pl.ds(c0, kc)].astype(BF16), p_ref[...])
        for t, alpha in enumerate(alphas):
            lanes = slice(t * Q_BLOCK, (t + 1) * Q_BLOCK)
            acc_ref[:, lanes] = alpha * acc_ref[:, lanes] + pv[:, lanes]
        return jnp.concatenate(m_out, axis=1), jnp.concatenate(l_out, axis=1)

    init = (jnp.full((1, nsub * cols), NEG, F32), jnp.zeros((1, nsub * cols), F32))
    _, l = lax.fori_loop(0, n_ch, body, init)
    ot = acc_ref[...] / l
    for u in range(nsub):
        heads = [jnp.transpose(ot[:, (u * grp + h) * Q_BLOCK:(u * grp + h + 1) * Q_BLOCK]) for h in range(grp)]
        for j in range(grp // 2):
            a, b = heads[2 * j], heads[2 * j + 1]
            left = jnp.where(first, a, pltpu.roll(a, HEAD_DIM, 1))
            right = jnp.where(first, pltpu.roll(b, HEAD_DIM, 1), b)
            o_ref[u * Q_BLOCK:(u + 1) * Q_BLOCK, j * LANES:(j + 1) * LANES] = jnp.where(lo, left, right)


DSA_ATTN_QUERIES = 512


def dsa_attn_prompt(y1, vt, layer, mask_t, table, batch, seq):
    n = batch * seq
    nq = DSA_ATTN_QUERIES
    assert seq % nq == 0
    nblk = seq // nq
    qw = (C_HEADS // C_KV_HEADS) * HEAD_DIM
    lanes_total = (C_HEADS // C_KV_HEADS) * nq
    return pl.pallas_call(
        _dsa_attn_kernel,
        grid=(batch, C_KV_HEADS, nblk),
        in_specs=[
            pl.BlockSpec((nq, qw), lambda b, g, i: (b * nblk + i, g)),
            pl.BlockSpec((seq, LANES), lambda b, g, i: (b, 8 + g // 2)),
            pl.BlockSpec((None, None, LANES, seq), lambda b, g, i: (layer, b, g // 2, 0)),
            pl.BlockSpec((None, seq, nq), lambda b, g, i: (b, 0, i)),
            pl.BlockSpec((1,) + table.shape[1:], lambda b, g, i: (g, 0, 0, 0)),
        ],
        out_specs=pl.BlockSpec((nq, qw), lambda b, g, i: (b * nblk + i, g)),
        out_shape=jax.ShapeDtypeStruct((n, C_HEADS * HEAD_DIM), F32),
        scratch_shapes=[pltpu.VMEM((SEL_CHUNK, lanes_total), F32), pltpu.VMEM((SEL_CHUNK, lanes_total), BF16),
                        pltpu.VMEM((LANES, lanes_total), F32)],
        compiler_params=_params("parallel", "parallel", "arbitrary"),
        name="dsa_attn_prompt",
    )(y1, y1, vt, mask_t, table)


IW_SCALE = (C_IDX_HEADS ** -0.5) * (C_IDX_DIM ** -0.5)
Q_SCALE = HEAD_DIM ** -0.5
AB_Y_COLS = 2304
C_Y_COLS = 1536
C_KV_COL = C_HEADS * HEAD_DIM
C_IK_COL = C_KV_COL + C_KV_HEADS * HEAD_DIM
C_IW_COL = C_IK_COL


def _pad_heads(w, n_heads):
    d = w.shape[0]
    w = w.reshape(d, n_heads, HEAD_DIM)
    return jnp.pad(w, ((0, 0), (0, 0), (0, HEAD_DIM))).reshape(d, n_heads * LANES)


def _pad_cols(w, total):
    return jnp.pad(w, ((0, 0), (0, total - w.shape[1])))


def prep_ab(w_in, a_q_norm, a_k_norm, w2, w_out):
    hd = A_HEADS * HEAD_DIM
    w1 = w_in[:, :2 * hd]
    s1 = jnp.concatenate([jnp.tile(a_q_norm, A_HEADS) * Q_SCALE, jnp.tile(a_k_norm, A_HEADS)])
    nbq = B_HEADS * B_DK
    blr0 = 3 * hd + 2 * nbq + B_HEADS * B_DV
    wy = jnp.concatenate([w_in[:, 2 * hd:blr0], w_in[:, blr0 + B_GATE_RANK:], w_in[:, blr0:blr0 + B_GATE_RANK]],
                         axis=1)
    wy = _pad_cols(wy, AB_Y_COLS)
    sy = jnp.ones((AB_Y_COLS,), F32).at[hd:hd + nbq].set(Q_SCALE)
    w2p = jnp.pad(w2, ((0, LANES - B_GATE_RANK), (0, 0)))
    return dict(w1=w1.astype(BF16), s1=s1, wy=wy.astype(BF16), sy=sy, w2p=w2p.astype(BF16),
                wo_a=w_out[:hd].astype(BF16), wo_b=w_out[hd:].astype(BF16))


def prep_c(w_in, c_q_norm, c_k_norm, c_ik_norm, w_out):
    o = np.cumsum((0, C_HEADS * HEAD_DIM, C_KV_HEADS * HEAD_DIM, C_KV_HEADS * HEAD_DIM,
                   C_IDX_HEADS * C_IDX_DIM, C_IDX_DIM, C_IDX_HEADS))
    cq, ck, cv, iq, ik, iw = (w_in[:, o[t]:o[t + 1]] for t in range(6))
    w1 = _pad_cols(jnp.concatenate([cq, ck, ik], axis=1), C_Y_COLS)
    s1 = jnp.concatenate([jnp.tile(c_q_norm, C_HEADS) * Q_SCALE, jnp.tile(c_k_norm, C_KV_HEADS), c_ik_norm])
    s1 = jnp.pad(s1, (0, C_Y_COLS - s1.shape[0]))
    w2 = _pad_cols(jnp.concatenate([_pad_heads(iq, C_IDX_HEADS), cv, iw], axis=1), C_Y_COLS)
    s2 = jnp.ones((C_Y_COLS,), F32).at[C_IW_COL:C_IW_COL + C_IDX_HEADS].set(IW_SCALE)
    return dict(w1=w1.astype(BF16), s1=s1, w2=w2.astype(BF16), s2=s2, wo=w_out.astype(BF16))


CACHE_POS_TILE = 512


def _to_cache_kernel(*refs, n_feat):
    src_ref, o_ref = refs[0], refs[-1]
    o_ref[...] = jnp.transpose(src_ref[...])[:n_feat]


def to_cache(src, col_block, width, n_feat, batch, seq, pos0, layer, n_layers, dst=None):
    n_pos = seq - pos0
    tl = min(CACHE_POS_TILE, n_pos)
    assert n_pos % tl == 0 and pos0 % tl == 0 and seq % tl == 0
    in_specs = [pl.BlockSpec((tl, width), lambda b, i: ((b * seq + pos0) // tl + i, col_block))]
    args = [src]
    aliases = {}
    if dst is not None:
        in_specs.append(pl.BlockSpec(memory_space=pl.ANY))
        args.append(dst)
        aliases = {1: 0}
    return pl.pallas_call(
        functools.partial(_to_cache_kernel, n_feat=n_feat),
        grid=(batch, n_pos // tl),
        in_specs=in_specs,
        out_specs=pl.BlockSpec((None, None, n_feat, tl), lambda b, i: (layer, b, 0, i)),
        out_shape=jax.ShapeDtypeStruct((n_layers, batch, n_feat, n_pos), F32),
        input_output_aliases=aliases,
        compiler_params=_params("parallel", "parallel"),
        name="to_cache",
    )(*args)


def _cache_result(buf, n_heads):
    nl, b, f, s = buf.shape
    if n_heads is None:
        return buf.transpose(0, 1, 3, 2)
    return buf.reshape(nl, b, n_heads, f // n_heads, s).transpose(0, 1, 4, 2, 3)


def prompt_forward(x, p, w, tables):
    batch, seq, d = x.shape
    n = batch * seq
    tm = min(512, n)
    wp = min(A_WIN_MAX, seq)
    hd = A_HEADS * HEAD_DIM
    n_ab, n_c = len(w["ab"]), len(w["c"])
    xf = x.reshape(n, d)
    a_k = a_v = c_k = c_v = c_ik = None
    b_s = []
    for l in range(len(w["ffn"])):
        li = l // 2
        if l % 2 == 0:
            ab = w["ab"][li]
            qk = norm_proj(xf, w["g_mix"][l], ab["w1"], ab["s1"], head_norm=True, tm=tm)
            y = norm_proj(xf, w["g_mix"][l], ab["wy"], ab["sy"], head_norm=False, tm=tm)
            oa = dilated_prompt(qk, y, tables["dil"], batch, seq)
            ob, s_fin = gla_prompt(y, ab["w2p"], w["b_gate_b"][li], w["b_out_norm"][li], batch, seq,
                                   tg=min(256, seq))
            xf = out_proj([oa, ob], [ab["wo_a"], ab["wo_b"]], xf, tm=tm)
            a_k = to_cache(qk, 1, hd, hd, batch, seq, seq - wp, li, n_ab, a_k)
            a_v = to_cache(y, 0, hd, hd, batch, seq, seq - wp, li, n_ab, a_v)
            b_s.append(s_fin)
        else:
            c = w["c"][li]
            kvw = C_KV_HEADS * HEAD_DIM
            y1 = norm_proj(xf, w["g_mix"][l], c["w1"], c["s1"], head_norm=True, tm=tm)
            y2 = norm_proj(xf, w["g_mix"][l], c["w2"], c["s2"], head_norm=False, tm=tm)
            c_v = to_cache(y2, C_KV_COL // kvw, kvw, kvw, batch, seq, 0, li, n_c, c_v)
            mask_t = dsa_select_prompt(y2, y2[:, C_IW_COL:C_IW_COL + C_IDX_HEADS].T, y1, batch, seq)
            oc = dsa_attn_prompt(y1, c_v, li, mask_t, tables["dsa"], batch, seq)
            xf = out_proj([oc], [c["wo"]], xf, tm=tm)
            c_k = to_cache(y1, C_KV_COL // kvw, kvw, kvw, batch, seq, 0, li, n_c, c_k)
            c_ik = to_cache(y1, C_IK_COL // LANES, LANES, C_IDX_DIM, batch, seq, 0, li, n_c, c_ik)
        f = w["ffn"][l]
        xf = ffn_ple(xf, p[l].reshape(n, -1), f["gf"], f["w1"], f["w3"], f["w2"], f["gp"], f["wg"], f["wp"],
                     tm=FFN_ROWS if n % FFN_ROWS == 0 else tm, tf=256)
    return (xf.reshape(batch, seq, d), _cache_result(a_k, A_HEADS), _cache_result(a_v, A_HEADS), jnp.stack(b_s),
            _cache_result(c_k, C_KV_HEADS), _cache_result(c_v, C_KV_HEADS), _cache_result(c_ik, None))


def dilated_sample_bias(rel_bias, wb):
    delta = wb - jnp.arange(wb)
    mult = _branch_multiplicity(delta)
    bias = rel_bias[_rel_bucket(delta)][:, :A_HEADS].astype(F32).T
    t = jnp.where(mult[None] > 0, bias + jnp.log(jnp.maximum(mult, 1).astype(F32))[None], NEG)
    b0 = rel_bias[_rel_bucket(jnp.zeros((), jnp.int32))][:A_HEADS].astype(F32)
    return t[:, None, :], (b0 + math.log(len(A_BRANCHES)))[:, None, None]


def _dil_sample_kernel(q_ref, kn_ref, vn_ref, kt_ref, vt_ref, t_ref, b0_ref, o_ref):
    q = q_ref[...]
    s_new = jnp.sum(q * kn_ref[...], axis=1, keepdims=True) + b0_ref[...]
    s = jnp.sum(q * kt_ref[...], axis=1, keepdims=True) + t_ref[...]
    m = jnp.maximum(s_new, jnp.max(s, axis=2, keepdims=True))
    p = jnp.exp(s - m)
    p_new = jnp.exp(s_new - m)
    l = p_new + jnp.sum(p, axis=2, keepdims=True)
    acc = p_new * vn_ref[...] + jnp.sum(p * vt_ref[...], axis=2, keepdims=True)
    o_ref[...] = acc / l


def dilated_sample(q_col, kn_col, vn_col, cache_kt, cache_vt, li, table, b0):
    bx, nh, hd, _ = q_col.shape
    wb = cache_kt.shape[-1]
    col = pl.BlockSpec((None, nh, hd, 1), lambda b: (b, 0, 0, 0))
    cache = pl.BlockSpec((None, None, nh, hd, wb), lambda b: (li, b, 0, 0, 0))
    return pl.pallas_call(
        _dil_sample_kernel,
        grid=(bx,),
        in_specs=[col, col, col, cache, cache,
                  pl.BlockSpec(table.shape, lambda b: (0, 0, 0)),
                  pl.BlockSpec(b0.shape, lambda b: (0, 0, 0))],
        out_specs=col,
        out_shape=jax.ShapeDtypeStruct((bx, nh, hd, 1), F32),
        compiler_params=_params("parallel"),
        name="dilated_sample",
    )(q_col, kn_col, vn_col, cache_kt, cache_vt, table, b0)


def _gla_gate_kernel(blr_ref, w2_ref, gb_ref, o_ref):
    z = _dot(blr_ref[...].astype(BF16), w2_ref[...]) + gb_ref[...]
    o_ref[...] = _log_sigmoid(z) / B_GATE_TAU


def gla_gate(y, w2p, gate_b):
    n = y.shape[0]
    nk = B_HEADS * B_DK
    return pl.pallas_call(
        _gla_gate_kernel,
        grid=(1,),
        in_specs=[pl.BlockSpec((n, LANES), lambda i: (0, 16)),
                  pl.BlockSpec((LANES, nk), lambda i: (0, 0)),
                  pl.BlockSpec((1, nk), lambda i: (0, 0))],
        out_specs=pl.BlockSpec((n, nk), lambda i: (0, 0)),
        out_shape=jax.ShapeDtypeStruct((n, nk), F32),
        compiler_params=_params("arbitrary"),
        name="gla_gate",
    )(y, w2p, gate_b.reshape(1, nk))


def _gla_step_kernel(q_ref, k_ref, g_ref, v_ref, bog_ref, on_ref, s_ref, o_ref, sn_ref):
    st = jnp.exp(g_ref[...]) * s_ref[...] + k_ref[...] * v_ref[...]
    sn_ref[...] = st
    o = jnp.sum(q_ref[...] * st, axis=2, keepdims=True)
    gate = bog_ref[...]
    o_ref[...] = _rms(o, on_ref[...]) * (gate * jax.nn.sigmoid(gate))


def gla_step(q, k, g, v, bog, out_norm, state, li, *, tb):
    bx = q.shape[0]
    col = pl.BlockSpec((tb, B_HEADS, B_DK, 1), lambda i: (i, 0, 0, 0))
    rowspec = pl.BlockSpec((tb, B_HEADS, 1, B_DV), lambda i: (i, 0, 0, 0))
    return pl.pallas_call(
        _gla_step_kernel,
        grid=(bx // tb,),
        in_specs=[col, col, col, rowspec, rowspec,
                  pl.BlockSpec((1, B_DV), lambda i: (0, 0)),
                  pl.BlockSpec((None, tb, B_HEADS, B_DK, B_DV), lambda i: (li, i, 0, 0, 0))],
        out_specs=[rowspec, pl.BlockSpec((tb, B_HEADS, B_DK, B_DV), lambda i: (i, 0, 0, 0))],
        out_shape=[jax.ShapeDtypeStruct((bx, B_HEADS, 1, B_DV), F32),
                   jax.ShapeDtypeStruct((bx, B_HEADS, B_DK, B_DV), F32)],
        compiler_params=_params("parallel"),
        name="gla_step",
    )(q, k, g, v, bog, out_norm.reshape(1, B_DV), state)


def _dsa_sample_score_kernel(pt_ref, iq_ref, iw_ref, ikn_ref, *refs):
    pages, o_ref = refs[:-1], refs[-1]
    iq = iq_ref[...].astype(BF16)
    iw = iw_ref[...]
    for j, pg in enumerate(pages):
        s = _dot(iq[:, :C_IDX_DIM], pg[...].astype(BF16))
        o_ref[j:j + 1, :] = jnp.sum(jnp.maximum(s, 0.0) * iw, axis=0, keepdims=True)
    s_new = jnp.sum(iq.astype(F32) * ikn_ref[...].astype(BF16).astype(F32), axis=-1, keepdims=True)
    sc_new = jnp.sum(jnp.maximum(s_new, 0.0) * iw, axis=0, keepdims=True)
    np_ = len(pages)
    o_ref[np_:, :] = jnp.broadcast_to(sc_new, (o_ref.shape[0] - np_, LANES))


def dsa_sample_scores(page_table, iq3, iw3, ik_new3, pool_ik, li, n_rows):
    bx, n_pages = page_table.shape
    page_specs = [pl.BlockSpec((None, None, C_IDX_DIM, PAGE_SIZE),
                               functools.partial(lambda b, pt, j: (li, pt[b, j], 0, 0), j=j))
                  for j in range(n_pages)]
    return pl.pallas_call(
        _dsa_sample_score_kernel,
        grid_spec=pltpu.PrefetchScalarGridSpec(
            num_scalar_prefetch=1,
            grid=(bx,),
            in_specs=[pl.BlockSpec((None, C_IDX_HEADS, LANES), lambda b, pt: (b, 0, 0)),
                      pl.BlockSpec((None, C_IDX_HEADS, 1), lambda b, pt: (b, 0, 0)),
                      pl.BlockSpec((None, 1, LANES), lambda b, pt: (b, 0, 0))] + page_specs,
            out_specs=pl.BlockSpec((None, n_rows, LANES), lambda b, pt: (b, 0, 0)),
        ),
        out_shape=jax.ShapeDtypeStruct((bx, n_rows, LANES), F32),
        compiler_params=_params("parallel"),
        name="dsa_sample_scores",
    )(page_table, iq3, iw3, ik_new3, *([pool_ik] * n_pages))


def _select_rows_kernel(sc_ref, o_ref, key_ref, *, topk, idx_bits, n_keys):
    width = sc_ref.shape[1]
    col = lax.broadcasted_iota(jnp.int32, (1, width), 1)
    key_ref[...] = jnp.where(col < n_keys, _sortable_key(sc_ref[...]), INT_MIN)

    def write(c0, val):
        o_ref[:, pl.ds(c0, SEL_CHUNK)] = val

    _threshold_select(key_ref, width // SEL_CHUNK, topk, idx_bits, n_keys - 1, write)


def dsa_select_sample(scores, n_keys):
    bx, width = scores.shape
    assert width % SEL_CHUNK == 0
    topk = min(C_TOPK_MAX, n_keys // 4)
    return pl.pallas_call(
        functools.partial(_select_rows_kernel, topk=topk, idx_bits=max(1, (width - 1).bit_length()),
                          n_keys=n_keys),
        grid=(1,),
        in_specs=[pl.BlockSpec((bx, width), lambda i: (0, 0))],
        out_specs=pl.BlockSpec((bx, width), lambda i: (0, 0)),
        out_shape=jax.ShapeDtypeStruct((bx, width), F32),
        scratch_shapes=[pltpu.VMEM((bx, width), jnp.int32)],
        compiler_params=_params("arbitrary"),
        name="dsa_select_sample",
    )(scores)


def dsa_sample_bias(rel_bias, past):
    n_pages = past // PAGE_SIZE
    bias = rel_bias[_rel_bucket(past - jnp.arange(past))][:, :C_HEADS].astype(F32)
    b0 = rel_bias[_rel_bucket(jnp.zeros((), jnp.int32))][:C_HEADS].astype(F32)[:, None]
    return bias.reshape(n_pages, PAGE_SIZE, C_HEADS).transpose(0, 2, 1), b0


def _lane_tiling_matrix():
    return jnp.asarray(np.tile(np.eye(HEAD_DIM, dtype=np.float32), (1, C_KV_HEADS)), BF16)


def _dsa_sample_attn_kernel(pt_ref, q_ref, kn_ref, vn_ref, mask_ref, bias_ref, b0_ref, e_ref, *refs):
    n_pages = (len(refs) - 1) // 2
    k_pages, v_pages, o_ref = refs[:n_pages], refs[n_pages:2 * n_pages], refs[-1]
    kvw = C_KV_HEADS * HEAD_DIM
    grp_shift = (C_HEADS // C_KV_HEADS).bit_length() - 1
    own = (jnp.right_shift(lax.broadcasted_iota(jnp.int32, (C_HEADS, kvw), 1), HEAD_DIM.bit_length() - 1)
           == jnp.right_shift(lax.broadcasted_iota(jnp.int32, (C_HEADS, kvw), 0), grp_shift))
    q = q_ref[...]
    q_bd = jnp.where(own, _dot(q.astype(BF16), e_ref[...]), 0.0).astype(BF16)
    s_new = (jnp.sum(q * kn_ref[...], axis=-1, keepdims=True) + b0_ref[...]
             + mask_ref[n_pages:n_pages + 1, 0:1])
    scores = [_dot(q_bd, k_pages[j][...].astype(BF16)) + bias_ref[j] + mask_ref[j:j + 1, :]
              for j in range(n_pages)]
    m = s_new
    for s in scores:
        m = jnp.maximum(m, jnp.max(s, axis=1, keepdims=True))
    p_new = jnp.exp(s_new - m)
    l = p_new
    acc = jnp.zeros((C_HEADS, kvw), F32)
    for j, s in enumerate(scores):
        p = jnp.exp(s - m)
        l = l + jnp.sum(p, axis=1, keepdims=True)
        acc = acc + _dot_nt(p.astype(BF16), v_pages[j][...].astype(BF16))
    acc = jnp.where(own, acc, 0.0)
    o = acc[:, :HEAD_DIM]
    for g in range(1, C_KV_HEADS):
        o = o + acc[:, g * HEAD_DIM:(g + 1) * HEAD_DIM]
    o_ref[...] = (o + p_new * vn_ref[...]) / l


def dsa_sample_attn(page_table, q3, k_new16, v_new16, mask3, bias3, b0, pool_kt, pool_vt, li):
    bx, n_pages = page_table.shape
    kvw = C_KV_HEADS * HEAD_DIM
    hspec = pl.BlockSpec((None, C_HEADS, HEAD_DIM), lambda b, pt: (b, 0, 0))
    page_specs = [pl.BlockSpec((None, None, kvw, PAGE_SIZE),
                               functools.partial(lambda b, pt, j: (li, pt[b, j], 0, 0), j=j))
                  for j in range(n_pages)]
    return pl.pallas_call(
        _dsa_sample_attn_kernel,
        grid_spec=pltpu.PrefetchScalarGridSpec(
            num_scalar_prefetch=1,
            grid=(bx,),
            in_specs=[hspec, hspec, hspec,
                      pl.BlockSpec((None,) + mask3.shape[1:], lambda b, pt: (b, 0, 0)),
                      pl.BlockSpec(bias3.shape, lambda b, pt: (0, 0, 0)),
                      pl.BlockSpec(b0.shape, lambda b, pt: (0, 0)),
                      pl.BlockSpec((HEAD_DIM, kvw), lambda b, pt: (0, 0))] + page_specs + page_specs,
            out_specs=hspec,
        ),
        out_shape=jax.ShapeDtypeStruct((bx, C_HEADS, HEAD_DIM), F32),
        compiler_params=_params("parallel"),
        name="dsa_sample_attn",
    )(page_table, q3, k_new16, v_new16, mask3, bias3, b0, _lane_tiling_matrix(),
      *([pool_kt] * n_pages), *([pool_vt] * n_pages))


def sample_forward(x, p, cache_a_k, cache_a_v, state_b, cache_c_k, cache_c_v, cache_c_ik, page_table, w,
                   rel_bias):
    bx, t_len, d = x.shape
    assert t_len == 1
    hd = A_HEADS * HEAD_DIM
    wb = cache_a_k.shape[2]
    n_pages = page_table.shape[1]
    past = n_pages * PAGE_SIZE
    n_rows = -(-(past + 1) // (2 * LANES)) * 2
    dil_bias, dil_b0 = dilated_sample_bias(rel_bias, wb)
    dsa_bias, dsa_b0 = dsa_sample_bias(rel_bias, past)
    cache_a_kt = cache_a_k.transpose(0, 1, 3, 4, 2)
    cache_a_vt = cache_a_v.transpose(0, 1, 3, 4, 2)
    pool_shape = cache_c_k.shape[:2] + (C_KV_HEADS * HEAD_DIM, PAGE_SIZE)
    pool_kt = cache_c_k.transpose(0, 1, 3, 4, 2).reshape(pool_shape)
    pool_vt = cache_c_v.transpose(0, 1, 3, 4, 2).reshape(pool_shape)
    pool_ikt = cache_c_ik.transpose(0, 1, 3, 2)
    grp = C_HEADS // C_KV_HEADS
    xf = x.reshape(bx, d)
    a_k, a_v, b_s, c_k, c_v, c_ik = [], [], [], [], [], []
    for l in range(len(w["ffn"])):
        li = l // 2
        if l % 2 == 0:
            ab = w["ab"][li]
            qk = norm_proj(xf, w["g_mix"][l], ab["w1"], ab["s1"], head_norm=True, tm=bx)
            y = norm_proj(xf, w["g_mix"][l], ab["wy"], ab["sy"], head_norm=False, tm=bx)
            v_new = y[:, :hd].reshape(bx, A_HEADS, HEAD_DIM)
            acol = lambda a: a.reshape(bx, A_HEADS, HEAD_DIM, 1)
            oa = dilated_sample(acol(qk[:, :hd]), acol(qk[:, hd:]), acol(y[:, :hd]), cache_a_kt, cache_a_vt, li,
                                dil_bias, dil_b0)
            g = gla_gate(y, ab["w2p"], w["b_gate_b"][li])
            nk = B_HEADS * B_DK
            colv = lambda a: a.reshape(bx, B_HEADS, B_DK, 1)
            rowv = lambda a: a.reshape(bx, B_HEADS, 1, B_DV)
            ob, s_fin = gla_step(colv(y[:, hd:hd + nk]), colv(y[:, hd + nk:hd + 2 * nk]), colv(g),
                                 rowv(y[:, 2 * hd:2 * hd + B_HEADS * B_DV]),
                                 rowv(y[:, 2 * hd + B_HEADS * B_DV:2 * hd + 2 * B_HEADS * B_DV]),
                                 w["b_out_norm"][li], state_b, li, tb=8)
            xf = out_proj([oa.reshape(bx, hd), ob.reshape(bx, B_HEADS * B_DV)], [ab["wo_a"], ab["wo_b"]], xf, tm=bx)
            a_k.append(qk[:, hd:].reshape(bx, 1, A_HEADS, HEAD_DIM))
            a_v.append(v_new.reshape(bx, 1, A_HEADS, HEAD_DIM))
            b_s.append(s_fin)
        else:
            c = w["c"][li]
            y1 = norm_proj(xf, w["g_mix"][l], c["w1"], c["s1"], head_norm=True, tm=bx)
            y2 = norm_proj(xf, w["g_mix"][l], c["w2"], c["s2"], head_norm=False, tm=bx)
            k0 = C_KV_COL
            k_new = y1[:, C_KV_COL:C_IK_COL].reshape(bx, C_KV_HEADS, HEAD_DIM)
            v_new = y2[:, C_KV_COL:C_IK_COL].reshape(bx, C_KV_HEADS, HEAD_DIM)
            scores = dsa_sample_scores(page_table, y2[:, :C_IDX_HEADS * LANES].reshape(bx, C_IDX_HEADS, LANES),
                                       y2[:, C_IW_COL:C_IW_COL + C_IDX_HEADS].reshape(bx, C_IDX_HEADS, 1),
                                       y1[:, C_IK_COL:C_IK_COL + LANES].reshape(bx, 1, LANES),
                                       pool_ikt, li, n_rows)
            mask = dsa_select_sample(scores.reshape(bx, n_rows * LANES), past + 1)
            oc = dsa_sample_attn(page_table, y1[:, :k0].reshape(bx, C_HEADS, HEAD_DIM),
                                 jnp.repeat(k_new, grp, axis=1), jnp.repeat(v_new, grp, axis=1),
                                 mask.reshape(bx, n_rows, LANES), dsa_bias, dsa_b0, pool_kt, pool_vt, li)
            xf = out_proj([oc.reshape(bx, k0)], [c["wo"]], xf, tm=bx)
            c_k.append(k_new.reshape(bx, 1, C_KV_HEADS, HEAD_DIM))
            c_v.append(v_new.reshape(bx, 1, C_KV_HEADS, HEAD_DIM))
            c_ik.append(y1[:, C_IK_COL:C_IK_COL + C_IDX_DIM].reshape(bx, 1, C_IDX_DIM))
        f = w["ffn"][l]
        xf = ffn_ple(xf, p[l].reshape(bx, -1), f["gf"], f["w1"], f["w3"], f["w2"], f["gp"], f["wg"], f["wp"],
                     tm=bx, tf=256)
    return (xf.reshape(bx, 1, d), jnp.stack(a_k), jnp.stack(a_v), jnp.stack(b_s),
            jnp.stack(c_k), jnp.stack(c_v), jnp.stack(c_ik))


def prep_weights(rel_bias, g_mix, w_in_ab, a_q_norm, a_k_norm, b_gate_w2, b_gate_b, b_out_norm, w_out_ab,
                 w_in_c, c_q_norm, c_k_norm, c_ik_norm, w_out_c, g_ffn, w_ff1, w_ff3, w_ff2, g_ple,
                 w_ple_gate, w_ple_proj):
    w = dict(g_mix=g_mix, b_gate_b=b_gate_b, b_out_norm=b_out_norm)
    w["ab"] = [prep_ab(w_in_ab[i], a_q_norm[i], a_k_norm[i], b_gate_w2[i], w_out_ab[i])
               for i in range(w_in_ab.shape[0])]
    w["c"] = [prep_c(w_in_c[i], c_q_norm[i], c_k_norm[i], c_ik_norm[i], w_out_c[i])
              for i in range(w_in_c.shape[0])]
    w["ffn"] = [dict(gf=g_ffn[l], w1=w_ff1[l].astype(BF16), w3=w_ff3[l].astype(BF16), w2=w_ff2[l].astype(BF16),
                     gp=g_ple[l], wg=w_ple_gate[l].astype(BF16), wp=w_ple_proj[l].astype(BF16))
                for l in range(g_ffn.shape[0])]
    tables = dict(dil=dilated_bias_table(rel_bias), dsa=dsa_bias_table(rel_bias))
    return w, tables


def kernel(x_prompt, x_sample, cache_a_k, cache_a_v, state_b, cache_c_k, cache_c_v, cache_c_ik, page_table,
           p_prompt, p_sample, rel_bias, g_mix, w_in_ab, a_q_norm, a_k_norm, b_gate_w2, b_gate_b, b_out_norm,
           w_out_ab, w_in_c, c_q_norm, c_k_norm, c_ik_norm, w_out_c, g_ffn, w_ff1, w_ff3, w_ff2, g_ple,
           w_ple_gate, w_ple_proj):
    w, tables = prep_weights(rel_bias, g_mix, w_in_ab, a_q_norm, a_k_norm, b_gate_w2, b_gate_b, b_out_norm,
                             w_out_ab, w_in_c, c_q_norm, c_k_norm, c_ik_norm, w_out_c, g_ffn, w_ff1, w_ff3,
                             w_ff2, g_ple, w_ple_gate, w_ple_proj)
    prompt = prompt_forward(x_prompt, p_prompt, w, tables)
    sample = sample_forward(x_sample, p_sample, cache_a_k, cache_a_v, state_b, cache_c_k, cache_c_v,
                            cache_c_ik, page_table, w, rel_bias)
    return (prompt[0], sample[0]) + tuple(prompt[1:]) + tuple(sample[1:])
```

```python
import functools
import math

import jax
import jax.numpy as jnp
import numpy as np
from jax import lax
from jax.experimental import pallas as pl
from jax.experimental.pallas import tpu as pltpu

F32 = jnp.float32
BF16 = jnp.bfloat16

LANES = 128
HEAD_DIM = 64
A_HEADS = 8
A_BRANCHES = ((128, 1), (512, 4), (2048, 16))
A_WIN_MAX = 2048
B_HEADS = 4
B_DK = 64
B_DV = 128
B_GATE_RANK = 16
B_GATE_TAU = 16.0
C_HEADS = 16
C_KV_HEADS = 4
C_IDX_HEADS = 8
C_IDX_DIM = 64
C_TOPK_MAX = 256
N_BUCKETS = 32
REL_MAX_DIST = A_WIN_MAX
Q_BLOCK = 128
PAGE_SIZE = 128
NORM_EPS = 1e-6
NEG = -1e30
LOG2E = math.log2(math.e)
VMEM_LIMIT = 56 * 1024 * 1024


def _params(*sem):
    return pltpu.CompilerParams(dimension_semantics=sem, vmem_limit_bytes=VMEM_LIMIT)


def _rms(x, g):
    ms = jnp.mean(x * x, axis=-1, keepdims=True)
    return x * lax.rsqrt(ms + NORM_EPS) * g


def _dot(a, b):
    return jnp.dot(a, b, preferred_element_type=F32)


def _dot_nt(a, b):
    return lax.dot_general(a, b, (((1,), (1,)), ((), ())), preferred_element_type=F32)


def _dot_tn(a, b):
    return lax.dot_general(a, b, (((0,), (0,)), ((), ())), preferred_element_type=F32)


PROJ_CHUNK = 256


def _proj_kernel(x_ref, g_ref, w_ref, cs_ref, p_ref, o_ref, *, head_norm):
    hn = _rms(x_ref[...], g_ref[...]).astype(BF16)
    for c in range(o_ref.shape[1] // PROJ_CHUNK):
        sl = slice(c * PROJ_CHUNK, (c + 1) * PROJ_CHUNK)
        y = _dot(hn, w_ref[:, sl])
        if head_norm:
            ms = _dot((y * y).astype(BF16), p_ref[...])
            y = y * lax.rsqrt(ms + NORM_EPS)
        o_ref[:, sl] = y * cs_ref[:, sl]


def _group_mean_matrix():
    r = np.arange(PROJ_CHUNK) // HEAD_DIM
    return jnp.asarray((r[:, None] == r[None, :]).astype(np.float32) / HEAD_DIM, BF16)


def norm_proj(x, g, w, colscale, *, head_norm, tm):
    n, d = x.shape
    dout = w.shape[1]
    assert n % tm == 0 and dout % PROJ_CHUNK == 0
    return pl.pallas_call(
        functools.partial(_proj_kernel, head_norm=head_norm),
        grid=(n // tm,),
        in_specs=[
            pl.BlockSpec((tm, d), lambda i: (i, 0)),
            pl.BlockSpec((1, d), lambda i: (0, 0)),
            pl.BlockSpec((d, dout), lambda i: (0, 0)),
            pl.BlockSpec((1, dout), lambda i: (0, 0)),
            pl.BlockSpec((PROJ_CHUNK, PROJ_CHUNK), lambda i: (0, 0)),
        ],
        out_specs=pl.BlockSpec((tm, dout), lambda i: (i, 0)),
        out_shape=jax.ShapeDtypeStruct((n, dout), F32),
        compiler_params=_params("parallel"),
        name="norm_proj_hn" if head_norm else "norm_proj",
    )(x, g.reshape(1, d), w, colscale.reshape(1, dout), _group_mean_matrix())


def _out_kernel(*refs, n_in):
    res_ref, o_ref = refs[2 * n_in], refs[2 * n_in + 1]
    acc = res_ref[...]
    for a_ref, w_ref in zip(refs[:n_in], refs[n_in:2 * n_in]):
        acc = acc + _dot(a_ref[...].astype(BF16), w_ref[...])
    o_ref[...] = acc


def out_proj(a_list, w_list, res, *, tm):
    n, d = res.shape
    n_in = len(a_list)
    in_specs = [pl.BlockSpec((tm, a.shape[1]), lambda i: (i, 0)) for a in a_list]
    in_specs += [pl.BlockSpec(w.shape, lambda i: (0, 0)) for w in w_list]
    in_specs += [pl.BlockSpec((tm, d), lambda i: (i, 0))]
    return pl.pallas_call(
        functools.partial(_out_kernel, n_in=n_in),
        grid=(n // tm,),
        in_specs=in_specs,
        out_specs=pl.BlockSpec((tm, d), lambda i: (i, 0)),
        out_shape=jax.ShapeDtypeStruct((n, d), F32),
        compiler_params=_params("parallel"),
        name="out_proj",
    )(*a_list, *w_list, res)


def _ffn_kernel(x_ref, gf_ref, w1_ref, w3_ref, w2_ref, gp_ref, wg_ref, p_ref, wp_ref, o_ref,
                hn_ref, acc_ref):
    f = pl.program_id(1)

    @pl.when(f == 0)
    def _():
        hn_ref[...] = _rms(x_ref[...], gf_ref[...]).astype(BF16)
        acc_ref[...] = jnp.zeros_like(acc_ref)

    hn = hn_ref[...]
    h1 = _dot(hn, w1_ref[...])
    h3 = _dot(hn, w3_ref[...])
    a = h1 * jax.nn.sigmoid(h1) * h3
    acc_ref[...] += _dot(a.astype(BF16), w2_ref[...])

    @pl.when(f == pl.num_programs(1) - 1)
    def _():
        x2 = x_ref[...] + acc_ref[...]
        u = _rms(x2, gp_ref[...]).astype(BF16)
        gate = jax.nn.sigmoid(_dot(u, wg_ref[...]))
        o_ref[...] = x2 + gate * _dot(p_ref[...].astype(BF16), wp_ref[...])


FFN_ROWS = 1024


def ffn_ple(x, p, layer, gf, w1, w3, w2, gp, wg, wp, *, tm, tf):
    n, d = x.shape
    dff = w1.shape[1]
    dple = p.shape[2]
    assert n % tm == 0 and dff % tf == 0
    return pl.pallas_call(
        _ffn_kernel,
        grid=(n // tm, dff // tf),
        in_specs=[
            pl.BlockSpec((tm, d), lambda i, f: (i, 0)),
            pl.BlockSpec((1, d), lambda i, f: (0, 0)),
            pl.BlockSpec((d, tf), lambda i, f: (0, f)),
            pl.BlockSpec((d, tf), lambda i, f: (0, f)),
            pl.BlockSpec((tf, d), lambda i, f: (f, 0)),
            pl.BlockSpec((1, d), lambda i, f: (0, 0)),
            pl.BlockSpec((d, d), lambda i, f: (0, 0)),
            pl.BlockSpec((None, tm, dple), lambda i, f: (layer, i, 0)),
            pl.BlockSpec((dple, d), lambda i, f: (0, 0)),
        ],
        out_specs=pl.BlockSpec((tm, d), lambda i, f: (i, 0)),
        out_shape=jax.ShapeDtypeStruct((n, d), F32),
        scratch_shapes=[pltpu.VMEM((tm, d), BF16), pltpu.VMEM((tm, d), F32)],
        compiler_params=_params("parallel", "arbitrary"),
        name="ffn_ple",
    )(x, gf.reshape(1, d), w1, w3, w2, gp.reshape(1, d), wg, p, wp)


def _rel_bucket(dist):
    n = jnp.maximum(dist, 0)
    exact = N_BUCKETS // 2
    nf = jnp.maximum(n, exact).astype(F32)
    large = exact + (jnp.log(nf / exact) / math.log(REL_MAX_DIST / exact)
                     * (N_BUCKETS - exact)).astype(jnp.int32)
    return jnp.where(n < exact, n, jnp.minimum(large, N_BUCKETS - 1))


def _branch_multiplicity(delta):
    mult = jnp.zeros(delta.shape, jnp.int32)
    for w, d in A_BRANCHES:
        mult = mult + ((delta >= 0) & (delta <= w) & (delta % d == 0)).astype(jnp.int32)
    return mult


DIL_WIN_CHUNKS = A_WIN_MAX // Q_BLOCK + 1


def _toeplitz(value_of_delta, base, width):
    period = width + Q_BLOCK - 1
    x = jnp.concatenate([jnp.arange(width), jnp.arange(-(Q_BLOCK - 1), 0)])
    v = value_of_delta(base - x)
    t = jnp.tile(v, (1, Q_BLOCK))[:, :Q_BLOCK * (period - 1)]
    return t.reshape(v.shape[0], Q_BLOCK, period - 1)[:, :, :width]


def dilated_bias_table(rel_bias):
    def value(delta):
        mult = _branch_multiplicity(delta)
        bias = rel_bias[_rel_bucket(delta)][:, :A_HEADS].astype(F32).T
        return jnp.where(mult[None] > 0, bias + jnp.log(jnp.maximum(mult, 1).astype(F32))[None], NEG)

    t = _toeplitz(value, A_WIN_MAX, DIL_WIN_CHUNKS * Q_BLOCK)
    t = t.reshape(A_HEADS // 2, 2 * Q_BLOCK, DIL_WIN_CHUNKS * Q_BLOCK)
    return jnp.pad(t * LOG2E, ((0, 0), (0, 0), (0, Q_BLOCK)), constant_values=NEG)


DSA_BIAS_TILES = 14


def dsa_bias_table(rel_bias):
    def value(delta):
        return rel_bias[_rel_bucket(delta)][:, :C_HEADS].astype(F32).T

    last = DSA_BIAS_TILES - 1
    t = _toeplitz(value, last * Q_BLOCK, DSA_BIAS_TILES * Q_BLOCK)
    grp = C_HEADS // C_KV_HEADS
    t = t.reshape(C_KV_HEADS, grp, Q_BLOCK, DSA_BIAS_TILES, Q_BLOCK)[:, :, :, ::-1, :]
    return (t * LOG2E).transpose(0, 3, 4, 1, 2).reshape(C_KV_HEADS, DSA_BIAS_TILES, Q_BLOCK, grp * Q_BLOCK)


def _lane_lt64():
    return lax.broadcasted_iota(jnp.int32, (1, LANES), 1) < HEAD_DIM


DIL_SUB_BLOCKS = 4


def _dil_kernel(q_ref, k_ref, v_ref, t_ref, o_ref, s_ref, p_ref):
    nw = s_ref.shape[2] // Q_BLOCK
    last = DIL_WIN_CHUNKS - 1
    lo = _lane_lt64()
    for u in range(DIL_SUB_BLOCKS):
        i = pl.program_id(2) * DIL_SUB_BLOCKS + u
        qrows = slice(u * Q_BLOCK, (u + 1) * Q_BLOCK)
        q = q_ref[qrows, :] * LOG2E
        q2 = jnp.concatenate([jnp.where(lo, q, 0.0), jnp.where(lo, 0.0, q)], axis=0).astype(BF16)
        w0 = jnp.maximum(i - (nw - 1), 0)
        rows = pl.ds(pl.multiple_of(w0 * Q_BLOCK, Q_BLOCK), nw * Q_BLOCK)
        s_ref[u] = _dot_nt(q2, k_ref[rows, :].astype(BF16))
        m = jnp.full((2 * Q_BLOCK, LANES), NEG, F32)
        for w in range(nw):
            c = last - i + w0 + w
            tcol = pl.multiple_of(jnp.where(c <= last, c, last + 1) * Q_BLOCK, Q_BLOCK)
            cols = slice(w * Q_BLOCK, (w + 1) * Q_BLOCK)
            s = s_ref[u, :, cols] + t_ref[0, :, pl.ds(tcol, Q_BLOCK)]
            s_ref[u, :, cols] = s
            m = jnp.maximum(m, s)
        m = jnp.max(m, axis=1, keepdims=True)
        l = jnp.zeros((2 * Q_BLOCK, LANES), F32)
        for w in range(nw):
            cols = slice(w * Q_BLOCK, (w + 1) * Q_BLOCK)
            p = jnp.exp2(s_ref[u, :, cols] - m)
            l = l + p
            p_ref[u, :, cols] = p.astype(BF16)
        o = _dot(p_ref[u], v_ref[rows, :].astype(BF16)) / jnp.sum(l, axis=1, keepdims=True)
        o_ref[qrows, :] = jnp.where(lo, o[:Q_BLOCK], o[Q_BLOCK:])


def dilated_prompt(qk, yv, table, batch, seq):
    n = batch * seq
    nsub = DIL_SUB_BLOCKS
    assert seq % (nsub * Q_BLOCK) == 0
    nstep = seq // (nsub * Q_BLOCK)
    npair = A_HEADS // 2
    win = min(DIL_WIN_CHUNKS, seq // Q_BLOCK) * Q_BLOCK
    return pl.pallas_call(
        _dil_kernel,
        grid=(batch, npair, nstep),
        in_specs=[
            pl.BlockSpec((nsub * Q_BLOCK, LANES), lambda b, p, i: (b * nstep + i, p)),
            pl.BlockSpec((seq, LANES), lambda b, p, i: (b, npair + p)),
            pl.BlockSpec((seq, LANES), lambda b, p, i: (b, p)),
            pl.BlockSpec((1,) + table.shape[1:], lambda b, p, i: (p, 0, 0)),
        ],
        out_specs=pl.BlockSpec((nsub * Q_BLOCK, LANES), lambda b, p, i: (b * nstep + i, p)),
        out_shape=jax.ShapeDtypeStruct((n, A_HEADS * HEAD_DIM), F32),
        scratch_shapes=[pltpu.VMEM((nsub, 2 * Q_BLOCK, win), F32), pltpu.VMEM((nsub, 2 * Q_BLOCK, win), BF16)],
        compiler_params=_params("parallel", "parallel", "arbitrary"),
        name="dilated_prompt",
    )(qk, qk, yv, table)


GLA_CHUNK = 64
GLA_SUB = 16
GLA_EXP_CLAMP = 60.0


def _log_sigmoid(z):
    return jnp.minimum(z, 0.0) - jnp.log1p(jnp.exp(-jnp.abs(z)))


def _gla_kernel(bq_ref, bk_ref, bv_ref, bog_ref, blr_ref, w2_ref, gb_ref, on_ref, o_ref, s_ref, st_ref):
    it = pl.program_id(1)

    @pl.when(it == 0)
    def _():
        st_ref[...] = jnp.zeros_like(st_ref)

    lo = _lane_lt64()
    ch = GLA_CHUNK
    row = lax.broadcasted_iota(jnp.int32, (ch, ch), 0)
    col = lax.broadcasted_iota(jnp.int32, (ch, ch), 1)
    causal = row >= col
    ltri = jnp.where(causal, 1.0, 0.0).astype(BF16)
    top_half = lax.broadcasted_iota(jnp.int32, (LANES, LANES), 0) < HEAD_DIM

    def chunk_body(c, carry):
        rows = pl.ds(pl.multiple_of(c * ch, ch), ch)
        for bi, p in [(bi, p) for bi in range(bq_ref.shape[0]) for p in range(B_HEADS // 2)]:
            sl = slice(p * LANES, (p + 1) * LANES)
            z = _dot(blr_ref[bi, rows, :].astype(BF16), w2_ref[:, sl]) + gb_ref[:, sl]
            g = _log_sigmoid(z) / B_GATE_TAU
            q = bq_ref[bi, rows, sl]
            k = bk_ref[bi, rows, sl]
            g_hi = g.astype(BF16)
            g_lo = (g - g_hi.astype(F32)).astype(BF16)
            cum = _dot(ltri, g_hi) + _dot(ltri, g_lo)
            last = cum[ch - 1:ch, :]
            st = st_ref[bi, p]
            qd = q * jnp.exp(cum)
            q2 = jnp.concatenate([jnp.where(lo, qd, 0.0), jnp.where(lo, 0.0, qd)], axis=0)
            o_inter = _dot(q2.astype(BF16), st.astype(BF16))
            atts = []
            for sb in range(ch // GLA_SUB):
                rs = slice(sb * GLA_SUB, (sb + 1) * GLA_SUB)
                ref_row = cum[sb * GLA_SUB:sb * GLA_SUB + 1, :]
                qs = q[rs] * jnp.exp(cum[rs] - ref_row)
                ks = k * jnp.exp(jnp.minimum(ref_row - cum, GLA_EXP_CLAMP))
                qq = jnp.concatenate([jnp.where(lo, qs, 0.0), jnp.where(lo, 0.0, qs)], axis=0)
                atts.append(_dot_nt(qq.astype(BF16), ks.astype(BF16)))
            kd = (k * jnp.exp(last - cum)).astype(BF16)
            upd = []
            for e in range(2):
                hs = slice((2 * p + e) * LANES, (2 * p + e + 1) * LANES)
                v = bv_ref[bi, rows, hs].astype(BF16)
                att = jnp.concatenate([a[e * GLA_SUB:(e + 1) * GLA_SUB] for a in atts], axis=0)
                att = jnp.where(causal, att, 0.0)
                o = o_inter[e * ch:(e + 1) * ch] + _dot(att.astype(BF16), v)
                og = _rms(o, on_ref[...])
                gate = bog_ref[bi, rows, hs]
                o_ref[bi, rows, hs] = og * (gate * jax.nn.sigmoid(gate))
                upd.append(_dot_tn(kd, v))
            decay = jnp.transpose(jnp.broadcast_to(jnp.exp(last), (LANES, LANES)))
            st_ref[bi, p] = decay * st + jnp.where(top_half, upd[0], upd[1])
        return carry

    lax.fori_loop(0, bq_ref.shape[1] // ch, chunk_body, 0)

    @pl.when(it == pl.num_programs(1) - 1)
    def _():
        for bi in range(s_ref.shape[0]):
            for p in range(B_HEADS // 2):
                s_ref[bi, 2 * p] = st_ref[bi, p, :HEAD_DIM, :]
                s_ref[bi, 2 * p + 1] = st_ref[bi, p, HEAD_DIM:, :]


GLA_BATCH = 4


def gla_prompt(y, w2p, gate_b, out_norm, batch, seq, *, tg):
    nb = GLA_BATCH if batch % GLA_BATCH == 0 else 1
    nt = seq // tg
    dv = B_HEADS * B_DV
    y3 = y.reshape(batch, seq, y.shape[1])
    o, s_fin = pl.pallas_call(
        _gla_kernel,
        grid=(batch // nb, nt),
        in_specs=[
            pl.BlockSpec((nb, tg, 256), lambda b, t: (b, t, 2)),
            pl.BlockSpec((nb, tg, 256), lambda b, t: (b, t, 3)),
            pl.BlockSpec((nb, tg, dv), lambda b, t: (b, t, 2)),
            pl.BlockSpec((nb, tg, dv), lambda b, t: (b, t, 3)),
            pl.BlockSpec((nb, tg, LANES), lambda b, t: (b, t, 16)),
            pl.BlockSpec((LANES, 256), lambda b, t: (0, 0)),
            pl.BlockSpec((1, 256), lambda b, t: (0, 0)),
            pl.BlockSpec((1, B_DV), lambda b, t: (0, 0)),
        ],
        out_specs=[
            pl.BlockSpec((nb, tg, dv), lambda b, t: (b, t, 0)),
            pl.BlockSpec((nb, B_HEADS, B_DK, B_DV), lambda b, t: (b, 0, 0, 0)),
        ],
        out_shape=[jax.ShapeDtypeStruct((batch, seq, dv), F32),
                   jax.ShapeDtypeStruct((batch, B_HEADS, B_DK, B_DV), F32)],
        scratch_shapes=[pltpu.VMEM((nb, B_HEADS // 2, LANES, LANES), F32)],
        compiler_params=_params("parallel", "arbitrary"),
        name="gla_prompt",
    )(y3, y3, y3, y3, y3, w2p, gate_b.reshape(1, 256), out_norm.reshape(1, B_DV))
    return o.reshape(batch * seq, dv), s_fin


SEL_CHUNK = 256
INT_MIN = -2 ** 31


def _sortable_key(score):
    bits = pltpu.bitcast(score + 0.0, jnp.int32)
    return jnp.where(bits < 0, bits ^ jnp.int32(0x7FFFFFFF), bits)


def _threshold_select(key_ref, n_ch, topk, idx_bits, row_limit, write):
    r = key_ref.shape[0]
    kc = SEL_CHUNK
    lane = lax.broadcasted_iota(jnp.int32, (1, kc), 1)

    def count(hits):
        def body(c, acc):
            c0 = pl.multiple_of(c * kc, kc)
            hit = hits(key_ref[:, pl.ds(c0, kc)], c0 + lane)
            for t in range(kc // LANES):
                acc = acc + hit[:, t * LANES:(t + 1) * LANES]
            return acc
        acc = lax.fori_loop(0, n_ch, body, jnp.zeros((r, LANES), jnp.int32))
        return jnp.sum(acc, axis=1, keepdims=True)

    n_nonneg = count(lambda k, _: jnp.where(k >= 0, 1, 0))
    base = jnp.where(n_nonneg >= topk, 0, INT_MIN).astype(jnp.int32)
    n_ge = jnp.where(n_nonneg >= topk, n_nonneg, n_ch * kc)

    def bit_body(b, carry):
        base, n_ge = carry
        cand = base | jnp.left_shift(jnp.int32(1), 30 - b)
        cnt = count(lambda k, _: jnp.where(k >= cand, 1, 0))
        return jnp.where(cnt >= topk, cand, base), jnp.where(cnt >= topk, cnt, n_ge)

    tau, n_ge = lax.fori_loop(0, 31, bit_body, (base, n_ge))

    def tie_break():
        need = topk - count(lambda k, _: jnp.where(k > tau, 1, 0))

        def idx_body(b, lo):
            cand = lo | jnp.left_shift(jnp.int32(1), idx_bits - 1 - b)
            cnt = count(lambda k, col: jnp.where(k == tau, jnp.where(col < cand, 1, 0), 0))
            return jnp.where(cnt < need, cand, lo)

        return lax.fori_loop(0, idx_bits, idx_body, jnp.zeros((r, 1), jnp.int32))

    last_eq = lax.cond(jnp.max(n_ge) > topk, tie_break,
                       lambda: jnp.full((r, 1), 2 ** idx_bits, jnp.int32))

    def out_body(c, carry):
        c0 = pl.multiple_of(c * kc, kc)
        k = key_ref[:, pl.ds(c0, kc)]
        col = c0 + lane
        val = jnp.where(k > tau, 0.0, jnp.where(k == tau, jnp.where(col <= last_eq, 0.0, NEG), NEG))
        write(c0, jnp.where(col <= row_limit, val, NEG))
        return carry

    lax.fori_loop(0, n_ch, out_body, 0)


def _threshold_select_t(key_ref, n_ch, topk, idx_bits, t_idx, write):
    nq = key_ref.shape[1]
    kc = SEL_CHUNK
    rowi = lax.broadcasted_iota(jnp.int32, (kc, 1), 0)

    def count(hits):
        def body(c, acc):
            c0 = pl.multiple_of(c * kc, kc)
            hit = hits(key_ref[pl.ds(c0, kc), :], c0 + rowi)
            return acc + jnp.sum(hit.reshape(kc // 8, 8, nq), axis=0)
        acc = lax.fori_loop(0, n_ch, body, jnp.zeros((8, nq), jnp.int32))
        return jnp.sum(acc, axis=0, keepdims=True)

    n_nonneg = count(lambda k, _: jnp.where(k >= 0, 1, 0))
    base = jnp.where(n_nonneg >= topk, 0, INT_MIN).astype(jnp.int32)
    n_ge = jnp.where(n_nonneg >= topk, n_nonneg, n_ch * kc)

    def bit_body(b, carry):
        base, n_ge = carry
        cand = base | jnp.left_shift(jnp.int32(1), 30 - b)
        cnt = count(lambda k, _: jnp.where(k >= cand, 1, 0))
        return jnp.where(cnt >= topk, cand, base), jnp.where(cnt >= topk, cnt, n_ge)

    tau, n_ge = lax.fori_loop(0, 31, bit_body, (base, n_ge))

    def tie_break():
        need = topk - count(lambda k, _: jnp.where(k > tau, 1, 0))

        def idx_body(b, lo):
            cand = lo | jnp.left_shift(jnp.int32(1), idx_bits - 1 - b)
            cnt = count(lambda k, s: jnp.where(k == tau, jnp.where(s < cand, 1, 0), 0))
            return jnp.where(cnt < need, cand, lo)

        return lax.fori_loop(0, idx_bits, idx_body, jnp.zeros((1, nq), jnp.int32))

    last_eq = lax.cond(jnp.max(n_ge) > topk, tie_break,
                       lambda: jnp.full((1, nq), 2 ** idx_bits, jnp.int32))

    def out_body(c, carry):
        c0 = pl.multiple_of(c * kc, kc)
        k = key_ref[pl.ds(c0, kc), :]
        s = c0 + rowi
        val = jnp.where(k > tau, 0.0, jnp.where(k == tau, jnp.where(s <= last_eq, 0.0, NEG), NEG))
        write(c0, jnp.where(s <= t_idx, val, NEG))
        return carry

    lax.fori_loop(0, n_ch, out_body, 0)


SEL_QUERIES = 256


def _select_kernel(iq_ref, iwt_ref, ik_ref, o_ref, key_ref, qs_ref, *, topk, idx_bits):
    qb = pl.program_id(1)
    kc = SEL_CHUNK
    nq = SEL_QUERIES
    n_ch = (qb * nq + nq + kc - 1) // kc
    t_idx = qb * nq + lax.broadcasted_iota(jnp.int32, (1, nq), 1)
    rowi = lax.broadcasted_iota(jnp.int32, (kc, 1), 0)
    iwt = iwt_ref[...]
    for h in range(C_IDX_HEADS):
        qs_ref[h] = iq_ref[:, h * LANES:(h + 1) * LANES].astype(BF16)

    def score_body(c, carry):
        c0 = pl.multiple_of(c * kc, kc)
        ik = ik_ref[pl.ds(c0, kc), :].astype(BF16)
        sc = jnp.zeros((kc, nq), F32)
        for h in range(C_IDX_HEADS):
            sc = sc + jnp.maximum(_dot_nt(ik, qs_ref[h]), 0.0) * iwt[h:h + 1, :]
        key_ref[pl.ds(c0, kc), :] = jnp.where(c0 + rowi <= t_idx, _sortable_key(sc), INT_MIN)
        return carry

    lax.fori_loop(0, n_ch, score_body, 0)
    o_ref[...] = jnp.full(o_ref.shape, NEG, o_ref.dtype)

    def write(c0, val):
        o_ref[pl.ds(c0, kc), :] = val.astype(o_ref.dtype)

    _threshold_select_t(key_ref, n_ch, topk, idx_bits, t_idx, write)


def dsa_select_prompt(y2, iwt, y1, batch, seq):
    assert seq % SEL_CHUNK == 0 and seq % SEL_QUERIES == 0
    nqb = seq // SEL_QUERIES
    topk = min(C_TOPK_MAX, seq // 4)
    return pl.pallas_call(
        functools.partial(_select_kernel, topk=topk, idx_bits=max(1, (seq - 1).bit_length())),
        grid=(batch, nqb),
        in_specs=[
            pl.BlockSpec((SEL_QUERIES, C_IDX_HEADS * LANES), lambda b, i: (b * nqb + i, 0)),
            pl.BlockSpec((C_IDX_HEADS, SEL_QUERIES), lambda b, i: (0, b * nqb + i)),
            pl.BlockSpec((seq, LANES), lambda b, i: (b, 10)),
        ],
        out_specs=pl.BlockSpec((None, seq, SEL_QUERIES), lambda b, i: (b, 0, i)),
        out_shape=jax.ShapeDtypeStruct((batch, seq, seq), BF16),
        scratch_shapes=[pltpu.VMEM((seq, SEL_QUERIES), jnp.int32),
                        pltpu.VMEM((C_IDX_HEADS, SEL_QUERIES, LANES), BF16)],
        compiler_params=_params("parallel", "arbitrary"),
        name="dsa_select_prompt",
    )(y2, iwt, y1)


def _dsa_attn_kernel(q_ref, k_ref, vt_ref, mask_ref, tb_ref, o_ref, st_ref, p_ref, acc_ref):
    g = pl.program_id(1)
    i = pl.program_id(2)
    kc = SEL_CHUNK
    grp = C_HEADS // C_KV_HEADS
    lo = _lane_lt64()
    first = (g % 2) == 0
    lane_half = jnp.right_shift(lax.broadcasted_iota(jnp.int32, (1, LANES), 1), HEAD_DIM.bit_length() - 1)
    own_half = lane_half == g % 2
    nsub = q_ref.shape[0] // Q_BLOCK
    cols = grp * Q_BLOCK
    parts = []
    for u in range(nsub):
        for j in range(grp // 2):
            qp = q_ref[u * Q_BLOCK:(u + 1) * Q_BLOCK, j * LANES:(j + 1) * LANES]
            qr = pltpu.roll(qp, HEAD_DIM, 1)
            parts.append(jnp.where(own_half, jnp.where(first, qp, qr), 0.0))
            parts.append(jnp.where(own_half, jnp.where(first, qr, qp), 0.0))
    q4 = (jnp.concatenate(parts, axis=0) * LOG2E).astype(BF16)

    n_ch = (i * nsub * Q_BLOCK + nsub * Q_BLOCK + kc - 1) // kc

    acc_ref[...] = jnp.zeros_like(acc_ref)

    def body(c, carry):
        m, l = carry
        c0 = pl.multiple_of(c * kc, kc)
        st_ref[...] = _dot_nt(k_ref[pl.ds(c0, kc), :].astype(BF16), q4)
        m_out, l_out, alphas = [], [], []
        for u in range(nsub):
            for h in range(grp):
                lanes = slice((u * grp + h) * Q_BLOCK, (u * grp + h + 1) * Q_BLOCK)
                halves = []
                for hf in range(kc // Q_BLOCK):
                    e = jnp.clip(i * nsub + u - (c * (kc // Q_BLOCK) + hf), 0, DSA_BIAS_TILES - 1)
                    mk = mask_ref[pl.ds(c0 + hf * Q_BLOCK, Q_BLOCK), u * Q_BLOCK:(u + 1) * Q_BLOCK]
                    halves.append(st_ref[hf * Q_BLOCK:(hf + 1) * Q_BLOCK, lanes]
                                  + tb_ref[0, e, :, h * Q_BLOCK:(h + 1) * Q_BLOCK] + mk.astype(F32))
                s = jnp.concatenate(halves, axis=0)
                m_new = jnp.maximum(m[:, lanes], jnp.max(s, axis=0, keepdims=True))
                alpha = jnp.exp2(m[:, lanes] - m_new)
                p = jnp.exp2(s - m_new)
                p_ref[:, lanes] = p.astype(BF16)
                m_out.append(m_new)
                l_out.append(alpha * l[:, lanes] + jnp.sum(p, axis=0, keepdims=True))
                alphas.append(alpha)
        pv = _dot(vt_ref[:, pl.ds(c0, kc)].astype(BF16), p_ref[...])
        for t, alpha in enumerate(alphas):
            lanes = slice(t * Q_BLOCK, (t + 1) * Q_BLOCK)
            acc_ref[:, lanes] = alpha * acc_ref[:, lanes] + pv[:, lanes]
        return jnp.concatenate(m_out, axis=1), jnp.concatenate(l_out, axis=1)

    init = (jnp.full((1, nsub * cols), NEG, F32), jnp.zeros((1, nsub * cols), F32))
    _, l = lax.fori_loop(0, n_ch, body, init)
    ot = acc_ref[...] / l
    for u in range(nsub):
        heads = [jnp.transpose(ot[:, (u * grp + h) * Q_BLOCK:(u * grp + h + 1) * Q_BLOCK]) for h in range(grp)]
        for j in range(grp // 2):
            a, b = heads[2 * j], heads[2 * j + 1]
            left = jnp.where(first, a, pltpu.roll(a, HEAD_DIM, 1))
            right = jnp.where(first, pltpu.roll(b, HEAD_DIM, 1), b)
            o_ref[u * Q_BLOCK:(u + 1) * Q_BLOCK, j * LANES:(j + 1) * LANES] = jnp.where(lo, left, right)


DSA_ATTN_QUERIES = 512


def dsa_attn_prompt(y1, vt, layer, mask_t, table, batch, seq):
    n = batch * seq
    nq = DSA_ATTN_QUERIES
    assert seq % nq == 0
    nblk = seq // nq
    qw = (C_HEADS // C_KV_HEADS) * HEAD_DIM
    lanes_total = (C_HEADS // C_KV_HEADS) * nq
    return pl.pallas_call(
        _dsa_attn_kernel,
        grid=(batch, C_KV_HEADS, nblk),
        in_specs=[
            pl.BlockSpec((nq, qw), lambda b, g, i: (b * nblk + i, g)),
            pl.BlockSpec((seq, LANES), lambda b, g, i: (b, 8 + g // 2)),
            pl.BlockSpec((None, None, LANES, seq), lambda b, g, i: (layer, b, g // 2, 0)),
            pl.BlockSpec((None, seq, nq), lambda b, g, i: (b, 0, i)),
            pl.BlockSpec((1,) + table.shape[1:], lambda b, g, i: (g, 0, 0, 0)),
        ],
        out_specs=pl.BlockSpec((nq, qw), lambda b, g, i: (b * nblk + i, g)),
        out_shape=jax.ShapeDtypeStruct((n, C_HEADS * HEAD_DIM), F32),
        scratch_shapes=[pltpu.VMEM((SEL_CHUNK, lanes_total), F32), pltpu.VMEM((SEL_CHUNK, lanes_total), BF16),
                        pltpu.VMEM((LANES, lanes_total), F32)],
        compiler_params=_params("parallel", "parallel", "arbitrary"),
        name="dsa_attn_prompt",
    )(y1, y1, vt, mask_t, table)


IW_SCALE = (C_IDX_HEADS ** -0.5) * (C_IDX_DIM ** -0.5)
Q_SCALE = HEAD_DIM ** -0.5
AB_Y_COLS = 2304
C_Y_COLS = 1536
C_KV_COL = C_HEADS * HEAD_DIM
C_IK_COL = C_KV_COL + C_KV_HEADS * HEAD_DIM
C_IW_COL = C_IK_COL


def _pad_heads(w, n_heads):
    d = w.shape[0]
    w = w.reshape(d, n_heads, HEAD_DIM)
    return jnp.pad(w, ((0, 0), (0, 0), (0, HEAD_DIM))).reshape(d, n_heads * LANES)


def _pad_cols(w, total):
    return jnp.pad(w, ((0, 0), (0, total - w.shape[1])))


def prep_ab(w_in, a_q_norm, a_k_norm, w2, w_out):
    hd = A_HEADS * HEAD_DIM
    w1 = w_in[:, :2 * hd]
    s1 = jnp.concatenate([jnp.tile(a_q_norm, A_HEADS) * Q_SCALE, jnp.tile(a_k_norm, A_HEADS)])
    nbq = B_HEADS * B_DK
    blr0 = 3 * hd + 2 * nbq + B_HEADS * B_DV
    wy = jnp.concatenate([w_in[:, 2 * hd:blr0], w_in[:, blr0 + B_GATE_RANK:], w_in[:, blr0:blr0 + B_GATE_RANK]],
                         axis=1)
    wy = _pad_cols(wy, AB_Y_COLS)
    sy = jnp.ones((AB_Y_COLS,), F32).at[hd:hd + nbq].set(Q_SCALE)
    w2p = jnp.pad(w2, ((0, LANES - B_GATE_RANK), (0, 0)))
    return dict(w1=w1.astype(BF16), s1=s1, wy=wy.astype(BF16), sy=sy, w2p=w2p.astype(BF16),
                wo_a=w_out[:hd].astype(BF16), wo_b=w_out[hd:].astype(BF16))


def prep_c(w_in, c_q_norm, c_k_norm, c_ik_norm, w_out):
    o = np.cumsum((0, C_HEADS * HEAD_DIM, C_KV_HEADS * HEAD_DIM, C_KV_HEADS * HEAD_DIM,
                   C_IDX_HEADS * C_IDX_DIM, C_IDX_DIM, C_IDX_HEADS))
    cq, ck, cv, iq, ik, iw = (w_in[:, o[t]:o[t + 1]] for t in range(6))
    w1 = _pad_cols(jnp.concatenate([cq, ck, ik], axis=1), C_Y_COLS)
    s1 = jnp.concatenate([jnp.tile(c_q_norm, C_HEADS) * Q_SCALE, jnp.tile(c_k_norm, C_KV_HEADS), c_ik_norm])
    s1 = jnp.pad(s1, (0, C_Y_COLS - s1.shape[0]))
    w2 = _pad_cols(jnp.concatenate([_pad_heads(iq, C_IDX_HEADS), cv, iw], axis=1), C_Y_COLS)
    s2 = jnp.ones((C_Y_COLS,), F32).at[C_IW_COL:C_IW_COL + C_IDX_HEADS].set(IW_SCALE)
    return dict(w1=w1.astype(BF16), s1=s1, w2=w2.astype(BF16), s2=s2, wo=w_out.astype(BF16))


CACHE_POS_TILE = 512


def _to_cache_kernel(*refs, n_feat):
    src_ref, o_ref = refs[0], refs[-1]
    o_ref[...] = jnp.transpose(src_ref[...])[:n_feat]


def to_cache(src, col_block, width, n_feat, batch, seq, pos0, layer, n_layers, dst=None):
    n_pos = seq - pos0
    tl = min(CACHE_POS_TILE, n_pos)
    assert n_pos % tl == 0 and pos0 % tl == 0 and seq % tl == 0
    shape = (n_layers, batch, n_feat, n_pos)
    if dst is None:
        dst = jnp.zeros(shape, F32)
    return pl.pallas_call(
        functools.partial(_to_cache_kernel, n_feat=n_feat),
        grid=(batch, n_pos // tl),
        in_specs=[pl.BlockSpec((tl, width), lambda b, i: ((b * seq + pos0) // tl + i, col_block)),
                  pl.BlockSpec(memory_space=pl.ANY)],
        out_specs=pl.BlockSpec((None, None, n_feat, tl), lambda b, i: (layer, b, 0, i)),
        out_shape=jax.ShapeDtypeStruct(shape, F32),
        input_output_aliases={1: 0},
        compiler_params=_params("parallel", "parallel"),
        name="to_cache",
    )(src, dst)


def _cache_result(buf, n_heads):
    nl, b, f, s = buf.shape
    if n_heads is None:
        return buf.transpose(0, 1, 3, 2)
    return buf.reshape(nl, b, n_heads, f // n_heads, s).transpose(0, 1, 4, 2, 3)


def prompt_forward(x, p, w, tables):
    batch, seq, d = x.shape
    n = batch * seq
    tm = FFN_ROWS if n % FFN_ROWS == 0 else min(512, n)
    wp = min(A_WIN_MAX, seq)
    hd = A_HEADS * HEAD_DIM
    n_ab, n_c = len(w["ab"]), len(w["c"])
    xf = x.reshape(n, d)
    a_k = a_v = c_k = c_v = c_ik = None
    b_s = []
    for l in range(len(w["ffn"])):
        li = l // 2
        if l % 2 == 0:
            ab = w["ab"][li]
            qk = norm_proj(xf, w["g_mix"][l], ab["w1"], ab["s1"], head_norm=True, tm=tm)
            y = norm_proj(xf, w["g_mix"][l], ab["wy"], ab["sy"], head_norm=False, tm=tm)
            oa = dilated_prompt(qk, y, tables["dil"], batch, seq)
            ob, s_fin = gla_prompt(y, ab["w2p"], w["b_gate_b"][li], w["b_out_norm"][li], batch, seq,
                                   tg=min(256, seq))
            xf = out_proj([oa, ob], [ab["wo_a"], ab["wo_b"]], xf, tm=tm)
            a_k = to_cache(qk, 1, hd, hd, batch, seq, seq - wp, li, n_ab, a_k)
            a_v = to_cache(y, 0, hd, hd, batch, seq, seq - wp, li, n_ab, a_v)
            b_s.append(s_fin)
        else:
            c = w["c"][li]
            kvw = C_KV_HEADS * HEAD_DIM
            y1 = norm_proj(xf, w["g_mix"][l], c["w1"], c["s1"], head_norm=True, tm=tm)
            y2 = norm_proj(xf, w["g_mix"][l], c["w2"], c["s2"], head_norm=False, tm=tm)
            c_v = to_cache(y2, C_KV_COL // kvw, kvw, kvw, batch, seq, 0, li, n_c, c_v)
            mask_t = dsa_select_prompt(y2, y2[:, C_IW_COL:C_IW_COL + C_IDX_HEADS].T, y1, batch, seq)
            oc = dsa_attn_prompt(y1, c_v, li, mask_t, tables["dsa"], batch, seq)
            xf = out_proj([oc], [c["wo"]], xf, tm=tm)
            c_k = to_cache(y1, C_KV_COL // kvw, kvw, kvw, batch, seq, 0, li, n_c, c_k)
            c_ik = to_cache(y1, C_IK_COL // LANES, LANES, C_IDX_DIM, batch, seq, 0, li, n_c, c_ik)
        f = w["ffn"][l]
        xf = ffn_ple(xf, p.reshape(p.shape[0], n, -1), l, f["gf"], f["w1"], f["w3"], f["w2"], f["gp"], f["wg"], f["wp"],
                     tm=tm, tf=256)
    return (xf.reshape(batch, seq, d), _cache_result(a_k, A_HEADS), _cache_result(a_v, A_HEADS), jnp.stack(b_s),
            _cache_result(c_k, C_KV_HEADS), _cache_result(c_v, C_KV_HEADS), _cache_result(c_ik, None))


def dilated_sample_bias(rel_bias, wb):
    delta = wb - jnp.arange(wb)
    mult = _branch_multiplicity(delta)
    bias = rel_bias[_rel_bucket(delta)][:, :A_HEADS].astype(F32).T
    t = jnp.where(mult[None] > 0, bias + jnp.log(jnp.maximum(mult, 1).astype(F32))[None], NEG)
    b0 = rel_bias[_rel_bucket(jnp.zeros((), jnp.int32))][:A_HEADS].astype(F32)
    return t[:, None, :], (b0 + math.log(len(A_BRANCHES)))[:, None, None]


def _dil_sample_kernel(q_ref, kn_ref, vn_ref, kt_ref, vt_ref, t_ref, b0_ref, o_ref):
    q = q_ref[...]
    s_new = jnp.sum(q * kn_ref[...], axis=1, keepdims=True) + b0_ref[...]
    s = jnp.sum(q * kt_ref[...], axis=1, keepdims=True) + t_ref[...]
    m = jnp.maximum(s_new, jnp.max(s, axis=2, keepdims=True))
    p = jnp.exp(s - m)
    p_new = jnp.exp(s_new - m)
    l = p_new + jnp.sum(p, axis=2, keepdims=True)
    acc = p_new * vn_ref[...] + jnp.sum(p * vt_ref[...], axis=2, keepdims=True)
    o_ref[...] = acc / l


def dilated_sample(q_col, kn_col, vn_col, cache_kt, cache_vt, li, table, b0):
    bx, nh, hd, _ = q_col.shape
    wb = cache_kt.shape[-1]
    col = pl.BlockSpec((None, nh, hd, 1), lambda b: (b, 0, 0, 0))
    cache = pl.BlockSpec((None, None, nh, hd, wb), lambda b: (li, b, 0, 0, 0))
    return pl.pallas_call(
        _dil_sample_kernel,
        grid=(bx,),
        in_specs=[col, col, col, cache, cache,
                  pl.BlockSpec(table.shape, lambda b: (0, 0, 0)),
                  pl.BlockSpec(b0.shape, lambda b: (0, 0, 0))],
        out_specs=col,
        out_shape=jax.ShapeDtypeStruct((bx, nh, hd, 1), F32),
        compiler_params=_params("parallel"),
        name="dilated_sample",
    )(q_col, kn_col, vn_col, cache_kt, cache_vt, table, b0)


def _gla_gate_kernel(blr_ref, w2_ref, gb_ref, o_ref):
    z = _dot(blr_ref[...].astype(BF16), w2_ref[...]) + gb_ref[...]
    o_ref[...] = _log_sigmoid(z) / B_GATE_TAU


def gla_gate(y, w2p, gate_b):
    n = y.shape[0]
    nk = B_HEADS * B_DK
    return pl.pallas_call(
        _gla_gate_kernel,
        grid=(1,),
        in_specs=[pl.BlockSpec((n, LANES), lambda i: (0, 16)),
                  pl.BlockSpec((LANES, nk), lambda i: (0, 0)),
                  pl.BlockSpec((1, nk), lambda i: (0, 0))],
        out_specs=pl.BlockSpec((n, nk), lambda i: (0, 0)),
        out_shape=jax.ShapeDtypeStruct((n, nk), F32),
        compiler_params=_params("arbitrary"),
        name="gla_gate",
    )(y, w2p, gate_b.reshape(1, nk))


def _gla_step_kernel(q_ref, k_ref, g_ref, v_ref, bog_ref, on_ref, s_ref, o_ref, sn_ref):
    st = jnp.exp(g_ref[...]) * s_ref[...] + k_ref[...] * v_ref[...]
    sn_ref[...] = st
    o = jnp.sum(q_ref[...] * st, axis=2, keepdims=True)
    gate = bog_ref[...]
    o_ref[...] = _rms(o, on_ref[...]) * (gate * jax.nn.sigmoid(gate))


def gla_step(q, k, g, v, bog, out_norm, state, li, *, tb):
    bx = q.shape[0]
    col = pl.BlockSpec((tb, B_HEADS, B_DK, 1), lambda i: (i, 0, 0, 0))
    rowspec = pl.BlockSpec((tb, B_HEADS, 1, B_DV), lambda i: (i, 0, 0, 0))
    return pl.pallas_call(
        _gla_step_kernel,
        grid=(bx // tb,),
        in_specs=[col, col, col, rowspec, rowspec,
                  pl.BlockSpec((1, B_DV), lambda i: (0, 0)),
                  pl.BlockSpec((None, tb, B_HEADS, B_DK, B_DV), lambda i: (li, i, 0, 0, 0))],
        out_specs=[rowspec, pl.BlockSpec((tb, B_HEADS, B_DK, B_DV), lambda i: (i, 0, 0, 0))],
        out_shape=[jax.ShapeDtypeStruct((bx, B_HEADS, 1, B_DV), F32),
                   jax.ShapeDtypeStruct((bx, B_HEADS, B_DK, B_DV), F32)],
        compiler_params=_params("parallel"),
        name="gla_step",
    )(q, k, g, v, bog, out_norm.reshape(1, B_DV), state)


def _dsa_sample_score_kernel(pt_ref, iq_ref, iw_ref, ikn_ref, *refs):
    pages, o_ref = refs[:-1], refs[-1]
    iq = iq_ref[...].astype(BF16)
    iw = iw_ref[...]
    for j, pg in enumerate(pages):
        s = _dot(iq[:, :C_IDX_DIM], pg[...].astype(BF16))
        o_ref[j:j + 1, :] = jnp.sum(jnp.maximum(s, 0.0) * iw, axis=0, keepdims=True)
    s_new = jnp.sum(iq.astype(F32) * ikn_ref[...].astype(BF16).astype(F32), axis=-1, keepdims=True)
    sc_new = jnp.sum(jnp.maximum(s_new, 0.0) * iw, axis=0, keepdims=True)
    np_ = len(pages)
    o_ref[np_:, :] = jnp.broadcast_to(sc_new, (o_ref.shape[0] - np_, LANES))


def dsa_sample_scores(page_table, iq3, iw3, ik_new3, pool_ik, li, n_rows):
    bx, n_pages = page_table.shape
    page_specs = [pl.BlockSpec((None, None, C_IDX_DIM, PAGE_SIZE),
                               functools.partial(lambda b, pt, j: (li, pt[b, j], 0, 0), j=j))
                  for j in range(n_pages)]
    return pl.pallas_call(
        _dsa_sample_score_kernel,
        grid_spec=pltpu.PrefetchScalarGridSpec(
            num_scalar_prefetch=1,
            grid=(bx,),
            in_specs=[pl.BlockSpec((None, C_IDX_HEADS, LANES), lambda b, pt: (b, 0, 0)),
                      pl.BlockSpec((None, C_IDX_HEADS, 1), lambda b, pt: (b, 0, 0)),
                      pl.BlockSpec((None, 1, LANES), lambda b, pt: (b, 0, 0))] + page_specs,
            out_specs=pl.BlockSpec((None, n_rows, LANES), lambda b, pt: (b, 0, 0)),
        ),
        out_shape=jax.ShapeDtypeStruct((bx, n_rows, LANES), F32),
        compiler_params=_params("parallel"),
        name="dsa_sample_scores",
    )(page_table, iq3, iw3, ik_new3, *([pool_ik] * n_pages))


def _select_rows_kernel(sc_ref, o_ref, key_ref, *, topk, idx_bits, n_keys):
    width = sc_ref.shape[1]
    col = lax.broadcasted_iota(jnp.int32, (1, width), 1)
    key_ref[...] = jnp.where(col < n_keys, _sortable_key(sc_ref[...]), INT_MIN)

    def write(c0, val):
        o_ref[:, pl.ds(c0, SEL_CHUNK)] = val

    _threshold_select(key_ref, width // SEL_CHUNK, topk, idx_bits, n_keys - 1, write)


def dsa_select_sample(scores, n_keys):
    bx, width = scores.shape
    assert width % SEL_CHUNK == 0
    topk = min(C_TOPK_MAX, n_keys // 4)
    return pl.pallas_call(
        functools.partial(_select_rows_kernel, topk=topk, idx_bits=max(1, (width - 1).bit_length()),
                          n_keys=n_keys),
        grid=(1,),
        in_specs=[pl.BlockSpec((bx, width), lambda i: (0, 0))],
        out_specs=pl.BlockSpec((bx, width), lambda i: (0, 0)),
        out_shape=jax.ShapeDtypeStruct((bx, width), F32),
        scratch_shapes=[pltpu.VMEM((bx, width), jnp.int32)],
        compiler_params=_params("arbitrary"),
        name="dsa_select_sample",
    )(scores)


def dsa_sample_bias(rel_bias, past):
    n_pages = past // PAGE_SIZE
    bias = rel_bias[_rel_bucket(past - jnp.arange(past))][:, :C_HEADS].astype(F32)
    b0 = rel_bias[_rel_bucket(jnp.zeros((), jnp.int32))][:C_HEADS].astype(F32)[:, None]
    return bias.reshape(n_pages, PAGE_SIZE, C_HEADS).transpose(0, 2, 1), b0


def _lane_tiling_matrix():
    return jnp.asarray(np.tile(np.eye(HEAD_DIM, dtype=np.float32), (1, C_KV_HEADS)), BF16)


def _dsa_sample_attn_kernel(pt_ref, q_ref, kn_ref, vn_ref, mask_ref, bias_ref, b0_ref, e_ref, *refs):
    n_pages = (len(refs) - 1) // 2
    k_pages, v_pages, o_ref = refs[:n_pages], refs[n_pages:2 * n_pages], refs[-1]
    kvw = C_KV_HEADS * HEAD_DIM
    grp_shift = (C_HEADS // C_KV_HEADS).bit_length() - 1
    own = (jnp.right_shift(lax.broadcasted_iota(jnp.int32, (C_HEADS, kvw), 1), HEAD_DIM.bit_length() - 1)
           == jnp.right_shift(lax.broadcasted_iota(jnp.int32, (C_HEADS, kvw), 0), grp_shift))
    q = q_ref[...]
    q_bd = jnp.where(own, _dot(q.astype(BF16), e_ref[...]), 0.0).astype(BF16)
    s_new = (jnp.sum(q * kn_ref[...], axis=-1, keepdims=True) + b0_ref[...]
             + mask_ref[n_pages:n_pages + 1, 0:1])
    scores = [_dot(q_bd, k_pages[j][...].astype(BF16)) + bias_ref[j] + mask_ref[j:j + 1, :]
              for j in range(n_pages)]
    m = s_new
    for s in scores:
        m = jnp.maximum(m, jnp.max(s, axis=1, keepdims=True))
    p_new = jnp.exp(s_new - m)
    l = p_new
    acc = jnp.zeros((C_HEADS, kvw), F32)
    for j, s in enumerate(scores):
        p = jnp.exp(s - m)
        l = l + jnp.sum(p, axis=1, keepdims=True)
        acc = acc + _dot_nt(p.astype(BF16), v_pages[j][...].astype(BF16))
    acc = jnp.where(own, acc, 0.0)
    o = acc[:, :HEAD_DIM]
    for g in range(1, C_KV_HEADS):
        o = o + acc[:, g * HEAD_DIM:(g + 1) * HEAD_DIM]
    o_ref[...] = (o + p_new * vn_ref[...]) / l


def dsa_sample_attn(page_table, q3, k_new16, v_new16, mask3, bias3, b0, pool_kt, pool_vt, li):
    bx, n_pages = page_table.shape
    kvw = C_KV_HEADS * HEAD_DIM
    hspec = pl.BlockSpec((None, C_HEADS, HEAD_DIM), lambda b, pt: (b, 0, 0))
    page_specs = [pl.BlockSpec((None, None, kvw, PAGE_SIZE),
                               functools.partial(lambda b, pt, j: (li, pt[b, j], 0, 0), j=j))
                  for j in range(n_pages)]
    return pl.pallas_call(
        _dsa_sample_attn_kernel,
        grid_spec=pltpu.PrefetchScalarGridSpec(
            num_scalar_prefetch=1,
            grid=(bx,),
            in_specs=[hspec, hspec, hspec,
                      pl.BlockSpec((None,) + mask3.shape[1:], lambda b, pt: (b, 0, 0)),
                      pl.BlockSpec(bias3.shape, lambda b, pt: (0, 0, 0)),
                      pl.BlockSpec(b0.shape, lambda b, pt: (0, 0)),
                      pl.BlockSpec((HEAD_DIM, kvw), lambda b, pt: (0, 0))] + page_specs + page_specs,
            out_specs=hspec,
        ),
        out_shape=jax.ShapeDtypeStruct((bx, C_HEADS, HEAD_DIM), F32),
        compiler_params=_params("parallel"),
        name="dsa_sample_attn",
    )(page_table, q3, k_new16, v_new16, mask3, bias3, b0, _lane_tiling_matrix(),
      *([pool_kt] * n_pages), *([pool_vt] * n_pages))


def sample_forward(x, p, cache_a_k, cache_a_v, state_b, cache_c_k, cache_c_v, cache_c_ik, page_table, w,
                   rel_bias):
    bx, t_len, d = x.shape
    assert t_len == 1
    hd = A_HEADS * HEAD_DIM
    wb = cache_a_k.shape[2]
    n_pages = page_table.shape[1]
    past = n_pages * PAGE_SIZE
    n_rows = -(-(past + 1) // (2 * LANES)) * 2
    dil_bias, dil_b0 = dilated_sample_bias(rel_bias, wb)
    dsa_bias, dsa_b0 = dsa_sample_bias(rel_bias, past)
    cache_a_kt = cache_a_k.transpose(0, 1, 3, 4, 2)
    cache_a_vt = cache_a_v.transpose(0, 1, 3, 4, 2)
    pool_shape = cache_c_k.shape[:2] + (C_KV_HEADS * HEAD_DIM, PAGE_SIZE)
    pool_kt = cache_c_k.transpose(0, 1, 3, 4, 2).reshape(pool_shape)
    pool_vt = cache_c_v.transpose(0, 1, 3, 4, 2).reshape(pool_shape)
    pool_ikt = cache_c_ik.transpose(0, 1, 3, 2)
    grp = C_HEADS // C_KV_HEADS
    xf = x.reshape(bx, d)
    a_k, a_v, b_s, c_k, c_v, c_ik = [], [], [], [], [], []
    for l in range(len(w["ffn"])):
        li = l // 2
        if l % 2 == 0:
            ab = w["ab"][li]
            qk = norm_proj(xf, w["g_mix"][l], ab["w1"], ab["s1"], head_norm=True, tm=bx)
            y = norm_proj(xf, w["g_mix"][l], ab["wy"], ab["sy"], head_norm=False, tm=bx)
            v_new = y[:, :hd].reshape(bx, A_HEADS, HEAD_DIM)
            acol = lambda a: a.reshape(bx, A_HEADS, HEAD_DIM, 1)
            oa = dilated_sample(acol(qk[:, :hd]), acol(qk[:, hd:]), acol(y[:, :hd]), cache_a_kt, cache_a_vt, li,
                                dil_bias, dil_b0)
            g = gla_gate(y, ab["w2p"], w["b_gate_b"][li])
            nk = B_HEADS * B_DK
            colv = lambda a: a.reshape(bx, B_HEADS, B_DK, 1)
            rowv = lambda a: a.reshape(bx, B_HEADS, 1, B_DV)
            ob, s_fin = gla_step(colv(y[:, hd:hd + nk]), colv(y[:, hd + nk:hd + 2 * nk]), colv(g),
                                 rowv(y[:, 2 * hd:2 * hd + B_HEADS * B_DV]),
                                 rowv(y[:, 2 * hd + B_HEADS * B_DV:2 * hd + 2 * B_HEADS * B_DV]),
                                 w["b_out_norm"][li], state_b, li, tb=8)
            xf = out_proj([oa.reshape(bx, hd), ob.reshape(bx, B_HEADS * B_DV)], [ab["wo_a"], ab["wo_b"]], xf, tm=bx)
            a_k.append(qk[:, hd:].reshape(bx, 1, A_HEADS, HEAD_DIM))
            a_v.append(v_new.reshape(bx, 1, A_HEADS, HEAD_DIM))
            b_s.append(s_fin)
        else:
            c = w["c"][li]
            y1 = norm_proj(xf, w["g_mix"][l], c["w1"], c["s1"], head_norm=True, tm=bx)
            y2 = norm_proj(xf, w["g_mix"][l], c["w2"], c["s2"], head_norm=False, tm=bx)
            k0 = C_KV_COL
            k_new = y1[:, C_KV_COL:C_IK_COL].reshape(bx, C_KV_HEADS, HEAD_DIM)
            v_new = y2[:, C_KV_COL:C_IK_COL].reshape(bx, C_KV_HEADS, HEAD_DIM)
            scores = dsa_sample_scores(page_table, y2[:, :C_IDX_HEADS * LANES].reshape(bx, C_IDX_HEADS, LANES),
                                       y2[:, C_IW_COL:C_IW_COL + C_IDX_HEADS].reshape(bx, C_IDX_HEADS, 1),
                                       y1[:, C_IK_COL:C_IK_COL + LANES].reshape(bx, 1, LANES),
                                       pool_ikt, li, n_rows)
            mask = dsa_select_sample(scores.reshape(bx, n_rows * LANES), past + 1)
            oc = dsa_sample_attn(page_table, y1[:, :k0].reshape(bx, C_HEADS, HEAD_DIM),
                                 jnp.repeat(k_new, grp, axis=1), jnp.repeat(v_new, grp, axis=1),
                                 mask.reshape(bx, n_rows, LANES), dsa_bias, dsa_b0, pool_kt, pool_vt, li)
            xf = out_proj([oc.reshape(bx, k0)], [c["wo"]], xf, tm=bx)
            c_k.append(k_new.reshape(bx, 1, C_KV_HEADS, HEAD_DIM))
            c_v.append(v_new.reshape(bx, 1, C_KV_HEADS, HEAD_DIM))
            c_ik.append(y1[:, C_IK_COL:C_IK_COL + C_IDX_DIM].reshape(bx, 1, C_IDX_DIM))
        f = w["ffn"][l]
        xf = ffn_ple(xf, p.reshape(p.shape[0], bx, -1), l, f["gf"], f["w1"], f["w3"], f["w2"], f["gp"], f["wg"], f["wp"],
                     tm=bx, tf=256)
    return (xf.reshape(bx, 1, d), jnp.stack(a_k), jnp.stack(a_v), jnp.stack(b_s),
            jnp.stack(c_k), jnp.stack(c_v), jnp.stack(c_ik))


def prep_weights(rel_bias, g_mix, w_in_ab, a_q_norm, a_k_norm, b_gate_w2, b_gate_b, b_out_norm, w_out_ab,
                 w_in_c, c_q_norm, c_k_norm, c_ik_norm, w_out_c, g_ffn, w_ff1, w_ff3, w_ff2, g_ple,
                 w_ple_gate, w_ple_proj):
    w = dict(g_mix=g_mix, b_gate_b=b_gate_b, b_out_norm=b_out_norm)
    w["ab"] = [prep_ab(w_in_ab[i], a_q_norm[i], a_k_norm[i], b_gate_w2[i], w_out_ab[i])
               for i in range(w_in_ab.shape[0])]
    w["c"] = [prep_c(w_in_c[i], c_q_norm[i], c_k_norm[i], c_ik_norm[i], w_out_c[i])
              for i in range(w_in_c.shape[0])]
    w["ffn"] = [dict(gf=g_ffn[l], w1=w_ff1[l].astype(BF16), w3=w_ff3[l].astype(BF16), w2=w_ff2[l].astype(BF16),
                     gp=g_ple[l], wg=w_ple_gate[l].astype(BF16), wp=w_ple_proj[l].astype(BF16))
                for l in range(g_ffn.shape[0])]
    tables = dict(dil=dilated_bias_table(rel_bias), dsa=dsa_bias_table(rel_bias))
    return w, tables


def kernel(x_prompt, x_sample, cache_a_k, cache_a_v, state_b, cache_c_k, cache_c_v, cache_c_ik, page_table,
           p_prompt, p_sample, rel_bias, g_mix, w_in_ab, a_q_norm, a_k_norm, b_gate_w2, b_gate_b, b_out_norm,
           w_out_ab, w_in_c, c_q_norm, c_k_norm, c_ik_norm, w_out_c, g_ffn, w_ff1, w_ff3, w_ff2, g_ple,
           w_ple_gate, w_ple_proj):
    w, tables = prep_weights(rel_bias, g_mix, w_in_ab, a_q_norm, a_k_norm, b_gate_w2, b_gate_b, b_out_norm,
                             w_out_ab, w_in_c, c_q_norm, c_k_norm, c_ik_norm, w_out_c, g_ffn, w_ff1, w_ff3,
                             w_ff2, g_ple, w_ple_gate, w_ple_proj)
    prompt = prompt_forward(x_prompt, p_prompt, w, tables)
    sample = sample_forward(x_sample, p_sample, cache_a_k, cache_a_v, state_b, cache_c_k, cache_c_v,
                            cache_c_ik, page_table, w, rel_bias)
    return (prompt[0], sample[0]) + tuple(prompt[1:]) + tuple(sample[1:])
```

```python
import functools
import math

import jax
import jax.numpy as jnp
import numpy as np
from jax import lax
from jax.experimental import pallas as pl
from jax.experimental.pallas import tpu as pltpu

F32 = jnp.float32
BF16 = jnp.bfloat16

LANES = 128
HEAD_DIM = 64
A_HEADS = 8
A_BRANCHES = ((128, 1), (512, 4), (2048, 16))
A_WIN_MAX = 2048
B_HEADS = 4
B_DK = 64
B_DV = 128
B_GATE_RANK = 16
B_GATE_TAU = 16.0
C_HEADS = 16
C_KV_HEADS = 4
C_IDX_HEADS = 8
C_IDX_DIM = 64
C_TOPK_MAX = 256
N_BUCKETS = 32
REL_MAX_DIST = A_WIN_MAX
Q_BLOCK = 128
PAGE_SIZE = 128
NORM_EPS = 1e-6
NEG = -1e30
LOG2E = math.log2(math.e)
VMEM_LIMIT = 56 * 1024 * 1024


def _params(*sem):
    return pltpu.CompilerParams(dimension_semantics=sem, vmem_limit_bytes=VMEM_LIMIT)


def _rms(x, g):
    ms = jnp.mean(x * x, axis=-1, keepdims=True)
    return x * lax.rsqrt(ms + NORM_EPS) * g


def _dot(a, b):
    return jnp.dot(a, b, preferred_element_type=F32)


def _dot_nt(a, b):
    return lax.dot_general(a, b, (((1,), (1,)), ((), ())), preferred_element_type=F32)


def _dot_tn(a, b):
    return lax.dot_general(a, b, (((0,), (0,)), ((), ())), preferred_element_type=F32)


PROJ_CHUNK = 256


def _proj_kernel(x_ref, g_ref, w_ref, cs_ref, p_ref, o_ref, *, head_norm):
    hn = _rms(x_ref[...], g_ref[...]).astype(BF16)
    for c in range(o_ref.shape[1] // PROJ_CHUNK):
        sl = slice(c * PROJ_CHUNK, (c + 1) * PROJ_CHUNK)
        y = _dot(hn, w_ref[:, sl])
        if head_norm:
            ms = _dot((y * y).astype(BF16), p_ref[...])
            y = y * lax.rsqrt(ms + NORM_EPS)
        o_ref[:, sl] = y * cs_ref[:, sl]


def _group_mean_matrix():
    r = np.arange(PROJ_CHUNK) // HEAD_DIM
    return jnp.asarray((r[:, None] == r[None, :]).astype(np.float32) / HEAD_DIM, BF16)


def norm_proj(x, g, w, colscale, *, head_norm, tm):
    n, d = x.shape
    dout = w.shape[1]
    assert n % tm == 0 and dout % PROJ_CHUNK == 0
    return pl.pallas_call(
        functools.partial(_proj_kernel, head_norm=head_norm),
        grid=(n // tm,),
        in_specs=[
            pl.BlockSpec((tm, d), lambda i: (i, 0)),
            pl.BlockSpec((1, d), lambda i: (0, 0)),
            pl.BlockSpec((d, dout), lambda i: (0, 0)),
            pl.BlockSpec((1, dout), lambda i: (0, 0)),
            pl.BlockSpec((PROJ_CHUNK, PROJ_CHUNK), lambda i: (0, 0)),
        ],
        out_specs=pl.BlockSpec((tm, dout), lambda i: (i, 0)),
        out_shape=jax.ShapeDtypeStruct((n, dout), F32),
        compiler_params=_params("parallel"),
        name="norm_proj_hn" if head_norm else "norm_proj",
    )(x, g.reshape(1, d), w, colscale.reshape(1, dout), _group_mean_matrix())


def _out_kernel(*refs, n_in):
    res_ref, o_ref = refs[2 * n_in], refs[2 * n_in + 1]
    acc = res_ref[...]
    for a_ref, w_ref in zip(refs[:n_in], refs[n_in:2 * n_in]):
        acc = acc + _dot(a_ref[...].astype(BF16), w_ref[...])
    o_ref[...] = acc


def out_proj(a_list, w_list, res, *, tm):
    n, d = res.shape
    n_in = len(a_list)
    in_specs = [pl.BlockSpec((tm, a.shape[1]), lambda i: (i, 0)) for a in a_list]
    in_specs += [pl.BlockSpec(w.shape, lambda i: (0, 0)) for w in w_list]
    in_specs += [pl.BlockSpec((tm, d), lambda i: (i, 0))]
    return pl.pallas_call(
        functools.partial(_out_kernel, n_in=n_in),
        grid=(n // tm,),
        in_specs=in_specs,
        out_specs=pl.BlockSpec((tm, d), lambda i: (i, 0)),
        out_shape=jax.ShapeDtypeStruct((n, d), F32),
        compiler_params=_params("parallel"),
        name="out_proj",
    )(*a_list, *w_list, res)


def _ffn_kernel(x_ref, gf_ref, w1_ref, w3_ref, w2_ref, gp_ref, wg_ref, p_ref, wp_ref, o_ref,
                hn_ref, acc_ref):
    f = pl.program_id(1)

    @pl.when(f == 0)
    def _():
        hn_ref[...] = _rms(x_ref[...], gf_ref[...]).astype(BF16)
        acc_ref[...] = jnp.zeros_like(acc_ref)

    hn = hn_ref[...]
    h1 = _dot(hn, w1_ref[...])
    h3 = _dot(hn, w3_ref[...])
    a = h1 * jax.nn.sigmoid(h1) * h3
    acc_ref[...] += _dot(a.astype(BF16), w2_ref[...])

    @pl.when(f == pl.num_programs(1) - 1)
    def _():
        x2 = x_ref[...] + acc_ref[...]
        u = _rms(x2, gp_ref[...]).astype(BF16)
        gate = jax.nn.sigmoid(_dot(u, wg_ref[...]))
        o_ref[...] = x2 + gate * _dot(p_ref[...].astype(BF16), wp_ref[...])


FFN_ROWS = 1024


def ffn_ple(x, p, layer, gf, w1, w3, w2, gp, wg, wp, *, tm, tf):
    n, d = x.shape
    dff = w1.shape[1]
    dple = p.shape[2]
    assert n % tm == 0 and dff % tf == 0
    return pl.pallas_call(
        _ffn_kernel,
        grid=(n // tm, dff // tf),
        in_specs=[
            pl.BlockSpec((tm, d), lambda i, f: (i, 0)),
            pl.BlockSpec((1, d), lambda i, f: (0, 0)),
            pl.BlockSpec((d, tf), lambda i, f: (0, f)),
            pl.BlockSpec((d, tf), lambda i, f: (0, f)),
            pl.BlockSpec((tf, d), lambda i, f: (f, 0)),
            pl.BlockSpec((1, d), lambda i, f: (0, 0)),
            pl.BlockSpec((d, d), lambda i, f: (0, 0)),
            pl.BlockSpec((None, tm, dple), lambda i, f: (layer, i, 0)),
            pl.BlockSpec((dple, d), lambda i, f: (0, 0)),
        ],
        out_specs=pl.BlockSpec((tm, d), lambda i, f: (i, 0)),
        out_shape=jax.ShapeDtypeStruct((n, d), F32),
        scratch_shapes=[pltpu.VMEM((tm, d), BF16), pltpu.VMEM((tm, d), F32)],
        compiler_params=_params("parallel", "arbitrary"),
        name="ffn_ple",
    )(x, gf.reshape(1, d), w1, w3, w2, gp.reshape(1, d), wg, p, wp)


def _rel_bucket(dist):
    n = jnp.maximum(dist, 0)
    exact = N_BUCKETS // 2
    nf = jnp.maximum(n, exact).astype(F32)
    large = exact + (jnp.log(nf / exact) / math.log(REL_MAX_DIST / exact)
                     * (N_BUCKETS - exact)).astype(jnp.int32)
    return jnp.where(n < exact, n, jnp.minimum(large, N_BUCKETS - 1))


def _branch_multiplicity(delta):
    mult = jnp.zeros(delta.shape, jnp.int32)
    for w, d in A_BRANCHES:
        mult = mult + ((delta >= 0) & (delta <= w) & (delta % d == 0)).astype(jnp.int32)
    return mult


DIL_WIN_CHUNKS = A_WIN_MAX // Q_BLOCK + 1


def _toeplitz(value_of_delta, base, width):
    period = width + Q_BLOCK - 1
    x = jnp.concatenate([jnp.arange(width), jnp.arange(-(Q_BLOCK - 1), 0)])
    v = value_of_delta(base - x)
    t = jnp.tile(v, (1, Q_BLOCK))[:, :Q_BLOCK * (period - 1)]
    return t.reshape(v.shape[0], Q_BLOCK, period - 1)[:, :, :width]


def dilated_bias_table(rel_bias):
    def value(delta):
        mult = _branch_multiplicity(delta)
        bias = rel_bias[_rel_bucket(delta)][:, :A_HEADS].astype(F32).T
        return jnp.where(mult[None] > 0, bias + jnp.log(jnp.maximum(mult, 1).astype(F32))[None], NEG)

    t = _toeplitz(value, A_WIN_MAX, DIL_WIN_CHUNKS * Q_BLOCK)
    t = t.reshape(A_HEADS // 2, 2 * Q_BLOCK, DIL_WIN_CHUNKS * Q_BLOCK)
    return jnp.pad(t * LOG2E, ((0, 0), (0, 0), (0, Q_BLOCK)), constant_values=NEG)


DSA_BIAS_TILES = 14


def dsa_bias_table(rel_bias):
    def value(delta):
        return rel_bias[_rel_bucket(delta)][:, :C_HEADS].astype(F32).T

    last = DSA_BIAS_TILES - 1
    t = _toeplitz(value, last * Q_BLOCK, DSA_BIAS_TILES * Q_BLOCK)
    grp = C_HEADS // C_KV_HEADS
    t = t.reshape(C_KV_HEADS, grp, Q_BLOCK, DSA_BIAS_TILES, Q_BLOCK)[:, :, :, ::-1, :]
    return (t * LOG2E).transpose(0, 3, 4, 1, 2).reshape(C_KV_HEADS, DSA_BIAS_TILES, Q_BLOCK, grp * Q_BLOCK)


def _lane_lt64():
    return lax.broadcasted_iota(jnp.int32, (1, LANES), 1) < HEAD_DIM


DIL_SUB_BLOCKS = 4


def _dil_kernel(q_ref, k_ref, v_ref, t_ref, o_ref, s_ref, p_ref):
    nw = s_ref.shape[2] // Q_BLOCK
    last = DIL_WIN_CHUNKS - 1
    lo = _lane_lt64()
    for u in range(DIL_SUB_BLOCKS):
        i = pl.program_id(2) * DIL_SUB_BLOCKS + u
        qrows = slice(u * Q_BLOCK, (u + 1) * Q_BLOCK)
        q = q_ref[qrows, :] * LOG2E
        q2 = jnp.concatenate([jnp.where(lo, q, 0.0), jnp.where(lo, 0.0, q)], axis=0).astype(BF16)
        w0 = jnp.maximum(i - (nw - 1), 0)
        rows = pl.ds(pl.multiple_of(w0 * Q_BLOCK, Q_BLOCK), nw * Q_BLOCK)
        s_ref[u] = _dot_nt(q2, k_ref[rows, :].astype(BF16))
        m = jnp.full((2 * Q_BLOCK, LANES), NEG, F32)
        for w in range(nw):
            c = last - i + w0 + w
            tcol = pl.multiple_of(jnp.where(c <= last, c, last + 1) * Q_BLOCK, Q_BLOCK)
            cols = slice(w * Q_BLOCK, (w + 1) * Q_BLOCK)
            s = s_ref[u, :, cols] + t_ref[0, :, pl.ds(tcol, Q_BLOCK)]
            s_ref[u, :, cols] = s
            m = jnp.maximum(m, s)
        m = jnp.max(m, axis=1, keepdims=True)
        l = jnp.zeros((2 * Q_BLOCK, LANES), F32)
        for w in range(nw):
            cols = slice(w * Q_BLOCK, (w + 1) * Q_BLOCK)
            p = jnp.exp2(s_ref[u, :, cols] - m)
            l = l + p
            p_ref[u, :, cols] = p.astype(BF16)
        o = _dot(p_ref[u], v_ref[rows, :].astype(BF16)) / jnp.sum(l, axis=1, keepdims=True)
        o_ref[qrows, :] = jnp.where(lo, o[:Q_BLOCK], o[Q_BLOCK:])


def dilated_prompt(qk, yv, table, batch, seq):
    n = batch * seq
    nsub = DIL_SUB_BLOCKS
    assert seq % (nsub * Q_BLOCK) == 0
    nstep = seq // (nsub * Q_BLOCK)
    npair = A_HEADS // 2
    win = min(DIL_WIN_CHUNKS, seq // Q_BLOCK) * Q_BLOCK
    return pl.pallas_call(
        _dil_kernel,
        grid=(batch, npair, nstep),
        in_specs=[
            pl.BlockSpec((nsub * Q_BLOCK, LANES), lambda b, p, i: (b * nstep + i, p)),
            pl.BlockSpec((seq, LANES), lambda b, p, i: (b, npair + p)),
            pl.BlockSpec((seq, LANES), lambda b, p, i: (b, p)),
            pl.BlockSpec((1,) + table.shape[1:], lambda b, p, i: (p, 0, 0)),
        ],
        out_specs=pl.BlockSpec((nsub * Q_BLOCK, LANES), lambda b, p, i: (b * nstep + i, p)),
        out_shape=jax.ShapeDtypeStruct((n, A_HEADS * HEAD_DIM), F32),
        scratch_shapes=[pltpu.VMEM((nsub, 2 * Q_BLOCK, win), F32), pltpu.VMEM((nsub, 2 * Q_BLOCK, win), BF16)],
        compiler_params=_params("parallel", "parallel", "arbitrary"),
        name="dilated_prompt",
    )(qk, qk, yv, table)


GLA_CHUNK = 64
GLA_SUB = 16
GLA_EXP_CLAMP = 60.0


def _log_sigmoid(z):
    return jnp.minimum(z, 0.0) - jnp.log1p(jnp.exp(-jnp.abs(z)))


def _gla_kernel(bq_ref, bk_ref, bv_ref, bog_ref, blr_ref, w2_ref, gb_ref, on_ref, o_ref, s_ref, st_ref):
    it = pl.program_id(1)

    @pl.when(it == 0)
    def _():
        st_ref[...] = jnp.zeros_like(st_ref)

    lo = _lane_lt64()
    ch = GLA_CHUNK
    row = lax.broadcasted_iota(jnp.int32, (ch, ch), 0)
    col = lax.broadcasted_iota(jnp.int32, (ch, ch), 1)
    causal = row >= col
    ltri = jnp.where(causal, 1.0, 0.0).astype(BF16)
    top_half = lax.broadcasted_iota(jnp.int32, (LANES, LANES), 0) < HEAD_DIM

    def chunk_body(c, carry):
        rows = pl.ds(pl.multiple_of(c * ch, ch), ch)
        for bi, p in [(bi, p) for bi in range(bq_ref.shape[0]) for p in range(B_HEADS // 2)]:
            sl = slice(p * LANES, (p + 1) * LANES)
            z = _dot(blr_ref[bi, rows, :].astype(BF16), w2_ref[:, sl]) + gb_ref[:, sl]
            g = _log_sigmoid(z) / B_GATE_TAU
            q = bq_ref[bi, rows, sl]
            k = bk_ref[bi, rows, sl]
            g_hi = g.astype(BF16)
            g_lo = (g - g_hi.astype(F32)).astype(BF16)
            cum = _dot(ltri, g_hi) + _dot(ltri, g_lo)
            last = cum[ch - 1:ch, :]
            st = st_ref[bi, p]
            qd = q * jnp.exp(cum)
            q2 = jnp.concatenate([jnp.where(lo, qd, 0.0), jnp.where(lo, 0.0, qd)], axis=0)
            o_inter = _dot(q2.astype(BF16), st.astype(BF16))
            atts = []
            for sb in range(ch // GLA_SUB):
                rs = slice(sb * GLA_SUB, (sb + 1) * GLA_SUB)
                ref_row = cum[sb * GLA_SUB:sb * GLA_SUB + 1, :]
                qs = q[rs] * jnp.exp(cum[rs] - ref_row)
                ks = k * jnp.exp(jnp.minimum(ref_row - cum, GLA_EXP_CLAMP))
                qq = jnp.concatenate([jnp.where(lo, qs, 0.0), jnp.where(lo, 0.0, qs)], axis=0)
                atts.append(_dot_nt(qq.astype(BF16), ks.astype(BF16)))
            kd = (k * jnp.exp(last - cum)).astype(BF16)
            upd = []
            for e in range(2):
                hs = slice((2 * p + e) * LANES, (2 * p + e + 1) * LANES)
                v = bv_ref[bi, rows, hs].astype(BF16)
                att = jnp.concatenate([a[e * GLA_SUB:(e + 1) * GLA_SUB] for a in atts], axis=0)
                att = jnp.where(causal, att, 0.0)
                o = o_inter[e * ch:(e + 1) * ch] + _dot(att.astype(BF16), v)
                og = _rms(o, on_ref[...])
                gate = bog_ref[bi, rows, hs]
                o_ref[bi, rows, hs] = og * (gate * jax.nn.sigmoid(gate))
                upd.append(_dot_tn(kd, v))
            decay = jnp.transpose(jnp.broadcast_to(jnp.exp(last), (LANES, LANES)))
            st_ref[bi, p] = decay * st + jnp.where(top_half, upd[0], upd[1])
        return carry

    lax.fori_loop(0, bq_ref.shape[1] // ch, chunk_body, 0)

    @pl.when(it == pl.num_programs(1) - 1)
    def _():
        for bi in range(s_ref.shape[0]):
            for p in range(B_HEADS // 2):
                s_ref[bi, 2 * p] = st_ref[bi, p, :HEAD_DIM, :]
                s_ref[bi, 2 * p + 1] = st_ref[bi, p, HEAD_DIM:, :]


GLA_BATCH = 4


def gla_prompt(y, w2p, gate_b, out_norm, batch, seq, *, tg):
    nb = GLA_BATCH if batch % GLA_BATCH == 0 else 1
    nt = seq // tg
    dv = B_HEADS * B_DV
    y3 = y.reshape(batch, seq, y.shape[1])
    o, s_fin = pl.pallas_call(
        _gla_kernel,
        grid=(batch // nb, nt),
        in_specs=[
            pl.BlockSpec((nb, tg, 256), lambda b, t: (b, t, 2)),
            pl.BlockSpec((nb, tg, 256), lambda b, t: (b, t, 3)),
            pl.BlockSpec((nb, tg, dv), lambda b, t: (b, t, 2)),
            pl.BlockSpec((nb, tg, dv), lambda b, t: (b, t, 3)),
            pl.BlockSpec((nb, tg, LANES), lambda b, t: (b, t, 16)),
            pl.BlockSpec((LANES, 256), lambda b, t: (0, 0)),
            pl.BlockSpec((1, 256), lambda b, t: (0, 0)),
            pl.BlockSpec((1, B_DV), lambda b, t: (0, 0)),
        ],
        out_specs=[
            pl.BlockSpec((nb, tg, dv), lambda b, t: (b, t, 0)),
            pl.BlockSpec((nb, B_HEADS, B_DK, B_DV), lambda b, t: (b, 0, 0, 0)),
        ],
        out_shape=[jax.ShapeDtypeStruct((batch, seq, dv), F32),
                   jax.ShapeDtypeStruct((batch, B_HEADS, B_DK, B_DV), F32)],
        scratch_shapes=[pltpu.VMEM((nb, B_HEADS // 2, LANES, LANES), F32)],
        compiler_params=_params("parallel", "arbitrary"),
        name="gla_prompt",
    )(y3, y3, y3, y3, y3, w2p, gate_b.reshape(1, 256), out_norm.reshape(1, B_DV))
    return o.reshape(batch * seq, dv), s_fin


SEL_CHUNK = 256
INT_MIN = -2 ** 31


def _sortable_key(score):
    bits = pltpu.bitcast(score + 0.0, jnp.int32)
    return jnp.where(bits < 0, bits ^ jnp.int32(0x7FFFFFFF), bits)


def _threshold_select(key_ref, n_ch, topk, idx_bits, row_limit, write):
    r = key_ref.shape[0]
    kc = SEL_CHUNK
    lane = lax.broadcasted_iota(jnp.int32, (1, kc), 1)

    def count(hits):
        def body(c, acc):
            c0 = pl.multiple_of(c * kc, kc)
            hit = hits(key_ref[:, pl.ds(c0, kc)], c0 + lane)
            for t in range(kc // LANES):
                acc = acc + hit[:, t * LANES:(t + 1) * LANES]
            return acc
        acc = lax.fori_loop(0, n_ch, body, jnp.zeros((r, LANES), jnp.int32))
        return jnp.sum(acc, axis=1, keepdims=True)

    n_nonneg = count(lambda k, _: jnp.where(k >= 0, 1, 0))
    base = jnp.where(n_nonneg >= topk, 0, INT_MIN).astype(jnp.int32)
    n_ge = jnp.where(n_nonneg >= topk, n_nonneg, n_ch * kc)

    def bit_body(b, carry):
        base, n_ge = carry
        cand = base | jnp.left_shift(jnp.int32(1), 30 - b)
        cnt = count(lambda k, _: jnp.where(k >= cand, 1, 0))
        return jnp.where(cnt >= topk, cand, base), jnp.where(cnt >= topk, cnt, n_ge)

    tau, n_ge = lax.fori_loop(0, 31, bit_body, (base, n_ge))

    def tie_break():
        need = topk - count(lambda k, _: jnp.where(k > tau, 1, 0))

        def idx_body(b, lo):
            cand = lo | jnp.left_shift(jnp.int32(1), idx_bits - 1 - b)
            cnt = count(lambda k, col: jnp.where(k == tau, jnp.where(col < cand, 1, 0), 0))
            return jnp.where(cnt < need, cand, lo)

        return lax.fori_loop(0, idx_bits, idx_body, jnp.zeros((r, 1), jnp.int32))

    last_eq = lax.cond(jnp.max(n_ge) > topk, tie_break,
                       lambda: jnp.full((r, 1), 2 ** idx_bits, jnp.int32))

    def out_body(c, carry):
        c0 = pl.multiple_of(c * kc, kc)
        k = key_ref[:, pl.ds(c0, kc)]
        col = c0 + lane
        val = jnp.where(k > tau, 0.0, jnp.where(k == tau, jnp.where(col <= last_eq, 0.0, NEG), NEG))
        write(c0, jnp.where(col <= row_limit, val, NEG))
        return carry

    lax.fori_loop(0, n_ch, out_body, 0)


def _threshold_select_t(key_ref, n_ch, topk, idx_bits, t_idx, write):
    nq = key_ref.shape[1]
    kc = SEL_CHUNK
    rowi = lax.broadcasted_iota(jnp.int32, (kc, 1), 0)

    def count(hits):
        def body(c, acc):
            c0 = pl.multiple_of(c * kc, kc)
            hit = hits(key_ref[pl.ds(c0, kc), :], c0 + rowi)
            return acc + jnp.sum(hit.reshape(kc // 8, 8, nq), axis=0)
        acc = lax.fori_loop(0, n_ch, body, jnp.zeros((8, nq), jnp.int32))
        return jnp.sum(acc, axis=0, keepdims=True)

    n_nonneg = count(lambda k, _: jnp.where(k >= 0, 1, 0))
    base = jnp.where(n_nonneg >= topk, 0, INT_MIN).astype(jnp.int32)
    n_ge = jnp.where(n_nonneg >= topk, n_nonneg, n_ch * kc)

    def bit_body(b, carry):
        base, n_ge = carry
        cand = base | jnp.left_shift(jnp.int32(1), 30 - b)
        cnt = count(lambda k, _: jnp.where(k >= cand, 1, 0))
        return jnp.where(cnt >= topk, cand, base), jnp.where(cnt >= topk, cnt, n_ge)

    tau, n_ge = lax.fori_loop(0, 31, bit_body, (base, n_ge))

    def tie_break():
        need = topk - count(lambda k, _: jnp.where(k > tau, 1, 0))

        def idx_body(b, lo):
            cand = lo | jnp.left_shift(jnp.int32(1), idx_bits - 1 - b)
            cnt = count(lambda k, s: jnp.where(k == tau, jnp.where(s < cand, 1, 0), 0))
            return jnp.where(cnt < need, cand, lo)

        return lax.fori_loop(0, idx_bits, idx_body, jnp.zeros((1, nq), jnp.int32))

    last_eq = lax.cond(jnp.max(n_ge) > topk, tie_break,
                       lambda: jnp.full((1, nq), 2 ** idx_bits, jnp.int32))

    def out_body(c, carry):
        c0 = pl.multiple_of(c * kc, kc)
        k = key_ref[pl.ds(c0, kc), :]
        s = c0 + rowi
        val = jnp.where(k > tau, 0.0, jnp.where(k == tau, jnp.where(s <= last_eq, 0.0, NEG), NEG))
        write(c0, jnp.where(s <= t_idx, val, NEG))
        return carry

    lax.fori_loop(0, n_ch, out_body, 0)


SEL_QUERIES = 256


def _select_kernel(iq_ref, iwt_ref, ik_ref, o_ref, key_ref, qs_ref, *, topk, idx_bits):
    qb = pl.program_id(1)
    kc = SEL_CHUNK
    nq = SEL_QUERIES
    n_ch = (qb * nq + nq + kc - 1) // kc
    t_idx = qb * nq + lax.broadcasted_iota(jnp.int32, (1, nq), 1)
    rowi = lax.broadcasted_iota(jnp.int32, (kc, 1), 0)
    iwt = iwt_ref[...]
    for h in range(C_IDX_HEADS):
        qs_ref[h] = iq_ref[:, h * LANES:(h + 1) * LANES].astype(BF16)

    def score_body(c, carry):
        c0 = pl.multiple_of(c * kc, kc)
        ik = ik_ref[pl.ds(c0, kc), :].astype(BF16)
        sc = jnp.zeros((kc, nq), F32)
        for h in range(C_IDX_HEADS):
            sc = sc + jnp.maximum(_dot_nt(ik, qs_ref[h]), 0.0) * iwt[h:h + 1, :]
        key_ref[pl.ds(c0, kc), :] = jnp.where(c0 + rowi <= t_idx, _sortable_key(sc), INT_MIN)
        return carry

    lax.fori_loop(0, n_ch, score_body, 0)
    o_ref[...] = jnp.full(o_ref.shape, NEG, o_ref.dtype)

    def write(c0, val):
        o_ref[pl.ds(c0, kc), :] = val.astype(o_ref.dtype)

    _threshold_select_t(key_ref, n_ch, topk, idx_bits, t_idx, write)


def dsa_select_prompt(y2, iwt, y1, batch, seq):
    assert seq % SEL_CHUNK == 0 and seq % SEL_QUERIES == 0
    nqb = seq // SEL_QUERIES
    topk = min(C_TOPK_MAX, seq // 4)
    return pl.pallas_call(
        functools.partial(_select_kernel, topk=topk, idx_bits=max(1, (seq - 1).bit_length())),
        grid=(batch, nqb),
        in_specs=[
            pl.BlockSpec((SEL_QUERIES, C_IDX_HEADS * LANES), lambda b, i: (b * nqb + i, 0)),
            pl.BlockSpec((C_IDX_HEADS, SEL_QUERIES), lambda b, i: (0, b * nqb + i)),
            pl.BlockSpec((seq, LANES), lambda b, i: (b, 10)),
        ],
        out_specs=pl.BlockSpec((None, seq, SEL_QUERIES), lambda b, i: (b, 0, i)),
        out_shape=jax.ShapeDtypeStruct((batch, seq, seq), BF16),
        scratch_shapes=[pltpu.VMEM((seq, SEL_QUERIES), jnp.int32),
                        pltpu.VMEM((C_IDX_HEADS, SEL_QUERIES, LANES), BF16)],
        compiler_params=_params("parallel", "arbitrary"),
        name="dsa_select_prompt",
    )(y2, iwt, y1)


def _dsa_attn_kernel(q_ref, k_ref, vt_ref, mask_ref, tb_ref, o_ref, st_ref, p_ref, acc_ref):
    g = pl.program_id(1)
    i = pl.program_id(2)
    kc = DSA_KEY_CHUNK
    grp = C_HEADS // C_KV_HEADS
    lo = _lane_lt64()
    first = (g % 2) == 0
    lane_half = jnp.right_shift(lax.broadcasted_iota(jnp.int32, (1, LANES), 1), HEAD_DIM.bit_length() - 1)
    own_half = lane_half == g % 2
    nsub = q_ref.shape[0] // Q_BLOCK
    cols = grp * Q_BLOCK
    parts = []
    for u in range(nsub):
        for j in range(grp // 2):
            qp = q_ref[u * Q_BLOCK:(u + 1) * Q_BLOCK, j * LANES:(j + 1) * LANES]
            qr = pltpu.roll(qp, HEAD_DIM, 1)
            parts.append(jnp.where(own_half, jnp.where(first, qp, qr), 0.0))
            parts.append(jnp.where(own_half, jnp.where(first, qr, qp), 0.0))
    q4 = (jnp.concatenate(parts, axis=0) * LOG2E).astype(BF16)

    n_ch = (i * nsub * Q_BLOCK + nsub * Q_BLOCK + kc - 1) // kc

    acc_ref[...] = jnp.zeros_like(acc_ref)

    def body(c, carry):
        m, l = carry
        c0 = pl.multiple_of(c * kc, kc)
        st_ref[...] = _dot_nt(k_ref[pl.ds(c0, kc), :].astype(BF16), q4)
        m_out, l_out, alphas = [], [], []
        for u in range(nsub):
            for h in range(grp):
                lanes = slice((u * grp + h) * Q_BLOCK, (u * grp + h + 1) * Q_BLOCK)
                halves = []
                for hf in range(kc // Q_BLOCK):
                    e = jnp.clip(i * nsub + u - (c * (kc // Q_BLOCK) + hf), 0, DSA_BIAS_TILES - 1)
                    mk = mask_ref[pl.ds(c0 + hf * Q_BLOCK, Q_BLOCK), u * Q_BLOCK:(u + 1) * Q_BLOCK]
                    halves.append(st_ref[hf * Q_BLOCK:(hf + 1) * Q_BLOCK, lanes]
                                  + tb_ref[0, e, :, h * Q_BLOCK:(h + 1) * Q_BLOCK] + mk.astype(F32))
                s = jnp.concatenate(halves, axis=0)
                m_new = jnp.maximum(m[:, lanes], jnp.max(s, axis=0, keepdims=True))
                alpha = jnp.exp2(m[:, lanes] - m_new)
                p = jnp.exp2(s - m_new)
                p_ref[:, lanes] = p.astype(BF16)
                m_out.append(m_new)
                l_out.append(alpha * l[:, lanes] + jnp.sum(p, axis=0, keepdims=True))
                alphas.append(alpha)
        pv = _dot(vt_ref[:, pl.ds(c0, kc)].astype(BF16), p_ref[...])
        for t, alpha in enumerate(alphas):
            lanes = slice(t * Q_BLOCK, (t + 1) * Q_BLOCK)
            acc_ref[:, lanes] = alpha * acc_ref[:, lanes] + pv[:, lanes]
        return jnp.concatenate(m_out, axis=1), jnp.concatenate(l_out, axis=1)

    init = (jnp.full((1, nsub * cols), NEG, F32), jnp.zeros((1, nsub * cols), F32))
    _, l = lax.fori_loop(0, n_ch, body, init)
    ot = acc_ref[...] / l
    for u in range(nsub):
        heads = [jnp.transpose(ot[:, (u * grp + h) * Q_BLOCK:(u * grp + h + 1) * Q_BLOCK]) for h in range(grp)]
        for j in range(grp // 2):
            a, b = heads[2 * j], heads[2 * j + 1]
            left = jnp.where(first, a, pltpu.roll(a, HEAD_DIM, 1))
            right = jnp.where(first, pltpu.roll(b, HEAD_DIM, 1), b)
            o_ref[u * Q_BLOCK:(u + 1) * Q_BLOCK, j * LANES:(j + 1) * LANES] = jnp.where(lo, left, right)


DSA_ATTN_QUERIES = 512
DSA_KEY_CHUNK = 512


def dsa_attn_prompt(y1, vt, layer, mask_t, table, batch, seq):
    n = batch * seq
    nq = DSA_ATTN_QUERIES
    assert seq % nq == 0
    nblk = seq // nq
    qw = (C_HEADS // C_KV_HEADS) * HEAD_DIM
    lanes_total = (C_HEADS // C_KV_HEADS) * nq
    return pl.pallas_call(
        _dsa_attn_kernel,
        grid=(batch, C_KV_HEADS, nblk),
        in_specs=[
            pl.BlockSpec((nq, qw), lambda b, g, i: (b * nblk + i, g)),
            pl.BlockSpec((seq, LANES), lambda b, g, i: (b, 8 + g // 2)),
            pl.BlockSpec((None, None, LANES, seq), lambda b, g, i: (layer, b, g // 2, 0)),
            pl.BlockSpec((None, seq, nq), lambda b, g, i: (b, 0, i)),
            pl.BlockSpec((1,) + table.shape[1:], lambda b, g, i: (g, 0, 0, 0)),
        ],
        out_specs=pl.BlockSpec((nq, qw), lambda b, g, i: (b * nblk + i, g)),
        out_shape=jax.ShapeDtypeStruct((n, C_HEADS * HEAD_DIM), F32),
        scratch_shapes=[pltpu.VMEM((DSA_KEY_CHUNK, lanes_total), F32), pltpu.VMEM((DSA_KEY_CHUNK, lanes_total), BF16),
                        pltpu.VMEM((LANES, lanes_total), F32)],
        compiler_params=_params("parallel", "parallel", "arbitrary"),
        name="dsa_attn_prompt",
    )(y1, y1, vt, mask_t, table)


IW_SCALE = (C_IDX_HEADS ** -0.5) * (C_IDX_DIM ** -0.5)
Q_SCALE = HEAD_DIM ** -0.5
AB_Y_COLS = 2304
C_Y_COLS = 1536
C_KV_COL = C_HEADS * HEAD_DIM
C_IK_COL = C_KV_COL + C_KV_HEADS * HEAD_DIM
C_IW_COL = C_IK_COL


def _pad_heads(w, n_heads):
    d = w.shape[0]
    w = w.reshape(d, n_heads, HEAD_DIM)
    return jnp.pad(w, ((0, 0), (0, 0), (0, HEAD_DIM))).reshape(d, n_heads * LANES)


def _pad_cols(w, total):
    return jnp.pad(w, ((0, 0), (0, total - w.shape[1])))


def prep_ab(w_in, a_q_norm, a_k_norm, w2, w_out):
    hd = A_HEADS * HEAD_DIM
    w1 = w_in[:, :2 * hd]
    s1 = jnp.concatenate([jnp.tile(a_q_norm, A_HEADS) * Q_SCALE, jnp.tile(a_k_norm, A_HEADS)])
    nbq = B_HEADS * B_DK
    blr0 = 3 * hd + 2 * nbq + B_HEADS * B_DV
    wy = jnp.concatenate([w_in[:, 2 * hd:blr0], w_in[:, blr0 + B_GATE_RANK:], w_in[:, blr0:blr0 + B_GATE_RANK]],
                         axis=1)
    wy = _pad_cols(wy, AB_Y_COLS)
    sy = jnp.ones((AB_Y_COLS,), F32).at[hd:hd + nbq].set(Q_SCALE)
    w2p = jnp.pad(w2, ((0, LANES - B_GATE_RANK), (0, 0)))
    return dict(w1=w1.astype(BF16), s1=s1, wy=wy.astype(BF16), sy=sy, w2p=w2p.astype(BF16),
                wo_a=w_out[:hd].astype(BF16), wo_b=w_out[hd:].astype(BF16))


def prep_c(w_in, c_q_norm, c_k_norm, c_ik_norm, w_out):
    o = np.cumsum((0, C_HEADS * HEAD_DIM, C_KV_HEADS * HEAD_DIM, C_KV_HEADS * HEAD_DIM,
                   C_IDX_HEADS * C_IDX_DIM, C_IDX_DIM, C_IDX_HEADS))
    cq, ck, cv, iq, ik, iw = (w_in[:, o[t]:o[t + 1]] for t in range(6))
    w1 = _pad_cols(jnp.concatenate([cq, ck, ik], axis=1), C_Y_COLS)
    s1 = jnp.concatenate([jnp.tile(c_q_norm, C_HEADS) * Q_SCALE, jnp.tile(c_k_norm, C_KV_HEADS), c_ik_norm])
    s1 = jnp.pad(s1, (0, C_Y_COLS - s1.shape[0]))
    w2 = _pad_cols(jnp.concatenate([_pad_heads(iq, C_IDX_HEADS), cv, iw], axis=1), C_Y_COLS)
    s2 = jnp.ones((C_Y_COLS,), F32).at[C_IW_COL:C_IW_COL + C_IDX_HEADS].set(IW_SCALE)
    return dict(w1=w1.astype(BF16), s1=s1, w2=w2.astype(BF16), s2=s2, wo=w_out.astype(BF16))


CACHE_POS_TILE = 512


def _to_cache_kernel(*refs, n_feat):
    src_ref, o_ref = refs[0], refs[-1]
    o_ref[...] = jnp.transpose(src_ref[...])[:n_feat]


def to_cache(src, col_block, width, n_feat, batch, seq, pos0, layer, n_layers, dst=None):
    n_pos = seq - pos0
    tl = min(CACHE_POS_TILE, n_pos)
    assert n_pos % tl == 0 and pos0 % tl == 0 and seq % tl == 0
    shape = (n_layers, batch, n_feat, n_pos)
    if dst is None:
        dst = jnp.zeros(shape, F32)
    return pl.pallas_call(
        functools.partial(_to_cache_kernel, n_feat=n_feat),
        grid=(batch, n_pos // tl),
        in_specs=[pl.BlockSpec((tl, width), lambda b, i: ((b * seq + pos0) // tl + i, col_block)),
                  pl.BlockSpec(memory_space=pl.ANY)],
        out_specs=pl.BlockSpec((None, None, n_feat, tl), lambda b, i: (layer, b, 0, i)),
        out_shape=jax.ShapeDtypeStruct(shape, F32),
        input_output_aliases={1: 0},
        compiler_params=_params("parallel", "parallel"),
        name="to_cache",
    )(src, dst)


def _cache_result(buf, n_heads):
    nl, b, f, s = buf.shape
    if n_heads is None:
        return buf.transpose(0, 1, 3, 2)
    return buf.reshape(nl, b, n_heads, f // n_heads, s).transpose(0, 1, 4, 2, 3)


def prompt_forward(x, p, w, tables):
    batch, seq, d = x.shape
    n = batch * seq
    tm = FFN_ROWS if n % FFN_ROWS == 0 else min(512, n)
    wp = min(A_WIN_MAX, seq)
    hd = A_HEADS * HEAD_DIM
    n_ab, n_c = len(w["ab"]), len(w["c"])
    xf = x.reshape(n, d)
    a_k = a_v = c_k = c_v = c_ik = None
    b_s = []
    for l in range(len(w["ffn"])):
        li = l // 2
        if l % 2 == 0:
            ab = w["ab"][li]
            qk = norm_proj(xf, w["g_mix"][l], ab["w1"], ab["s1"], head_norm=True, tm=tm)
            y = norm_proj(xf, w["g_mix"][l], ab["wy"], ab["sy"], head_norm=False, tm=tm)
            oa = dilated_prompt(qk, y, tables["dil"], batch, seq)
            ob, s_fin = gla_prompt(y, ab["w2p"], w["b_gate_b"][li], w["b_out_norm"][li], batch, seq,
                                   tg=min(256, seq))
            xf = out_proj([oa, ob], [ab["wo_a"], ab["wo_b"]], xf, tm=tm)
            a_k = to_cache(qk, 1, hd, hd, batch, seq, seq - wp, li, n_ab, a_k)
            a_v = to_cache(y, 0, hd, hd, batch, seq, seq - wp, li, n_ab, a_v)
            b_s.append(s_fin)
        else:
            c = w["c"][li]
            kvw = C_KV_HEADS * HEAD_DIM
            y1 = norm_proj(xf, w["g_mix"][l], c["w1"], c["s1"], head_norm=True, tm=tm)
            y2 = norm_proj(xf, w["g_mix"][l], c["w2"], c["s2"], head_norm=False, tm=tm)
            c_v = to_cache(y2, C_KV_COL // kvw, kvw, kvw, batch, seq, 0, li, n_c, c_v)
            mask_t = dsa_select_prompt(y2, y2[:, C_IW_COL:C_IW_COL + C_IDX_HEADS].T, y1, batch, seq)
            oc = dsa_attn_prompt(y1, c_v, li, mask_t, tables["dsa"], batch, seq)
            xf = out_proj([oc], [c["wo"]], xf, tm=tm)
            c_k = to_cache(y1, C_KV_COL // kvw, kvw, kvw, batch, seq, 0, li, n_c, c_k)
            c_ik = to_cache(y1, C_IK_COL // LANES, LANES, C_IDX_DIM, batch, seq, 0, li, n_c, c_ik)
        f = w["ffn"][l]
        xf = ffn_ple(xf, p.reshape(p.shape[0], n, -1), l, f["gf"], f["w1"], f["w3"], f["w2"], f["gp"], f["wg"], f["wp"],
                     tm=tm, tf=256)
    return (xf.reshape(batch, seq, d), _cache_result(a_k, A_HEADS), _cache_result(a_v, A_HEADS), jnp.stack(b_s),
            _cache_result(c_k, C_KV_HEADS), _cache_result(c_v, C_KV_HEADS), _cache_result(c_ik, None))


def dilated_sample_bias(rel_bias, wb):
    delta = wb - jnp.arange(wb)
    mult = _branch_multiplicity(delta)
    bias = rel_bias[_rel_bucket(delta)][:, :A_HEADS].astype(F32).T
    t = jnp.where(mult[None] > 0, bias + jnp.log(jnp.maximum(mult, 1).astype(F32))[None], NEG)
    b0 = rel_bias[_rel_bucket(jnp.zeros((), jnp.int32))][:A_HEADS].astype(F32)
    return t[:, None, :], (b0 + math.log(len(A_BRANCHES)))[:, None, None]


def _dil_sample_kernel(q_ref, kn_ref, vn_ref, kt_ref, vt_ref, t_ref, b0_ref, o_ref):
    q = q_ref[...]
    s_new = jnp.sum(q * kn_ref[...], axis=1, keepdims=True) + b0_ref[...]
    s = jnp.sum(q * kt_ref[...], axis=1, keepdims=True) + t_ref[...]
    m = jnp.maximum(s_new, jnp.max(s, axis=2, keepdims=True))
    p = jnp.exp(s - m)
    p_new = jnp.exp(s_new - m)
    l = p_new + jnp.sum(p, axis=2, keepdims=True)
    acc = p_new * vn_ref[...] + jnp.sum(p * vt_ref[...], axis=2, keepdims=True)
    o_ref[...] = acc / l


def dilated_sample(q_col, kn_col, vn_col, cache_kt, cache_vt, li, table, b0):
    bx, nh, hd, _ = q_col.shape
    wb = cache_kt.shape[-1]
    col = pl.BlockSpec((None, nh, hd, 1), lambda b: (b, 0, 0, 0))
    cache = pl.BlockSpec((None, None, nh, hd, wb), lambda b: (li, b, 0, 0, 0))
    return pl.pallas_call(
        _dil_sample_kernel,
        grid=(bx,),
        in_specs=[col, col, col, cache, cache,
                  pl.BlockSpec(table.shape, lambda b: (0, 0, 0)),
                  pl.BlockSpec(b0.shape, lambda b: (0, 0, 0))],
        out_specs=col,
        out_shape=jax.ShapeDtypeStruct((bx, nh, hd, 1), F32),
        compiler_params=_params("parallel"),
        name="dilated_sample",
    )(q_col, kn_col, vn_col, cache_kt, cache_vt, table, b0)


def _gla_gate_kernel(blr_ref, w2_ref, gb_ref, o_ref):
    z = _dot(blr_ref[...].astype(BF16), w2_ref[...]) + gb_ref[...]
    o_ref[...] = _log_sigmoid(z) / B_GATE_TAU


def gla_gate(y, w2p, gate_b):
    n = y.shape[0]
    nk = B_HEADS * B_DK
    return pl.pallas_call(
        _gla_gate_kernel,
        grid=(1,),
        in_specs=[pl.BlockSpec((n, LANES), lambda i: (0, 16)),
                  pl.BlockSpec((LANES, nk), lambda i: (0, 0)),
                  pl.BlockSpec((1, nk), lambda i: (0, 0))],
        out_specs=pl.BlockSpec((n, nk), lambda i: (0, 0)),
        out_shape=jax.ShapeDtypeStruct((n, nk), F32),
        compiler_params=_params("arbitrary"),
        name="gla_gate",
    )(y, w2p, gate_b.reshape(1, nk))


def _gla_step_kernel(q_ref, k_ref, g_ref, v_ref, bog_ref, on_ref, s_ref, o_ref, sn_ref):
    st = jnp.exp(g_ref[...]) * s_ref[...] + k_ref[...] * v_ref[...]
    sn_ref[...] = st
    o = jnp.sum(q_ref[...] * st, axis=2, keepdims=True)
    gate = bog_ref[...]
    o_ref[...] = _rms(o, on_ref[...]) * (gate * jax.nn.sigmoid(gate))


def gla_step(q, k, g, v, bog, out_norm, state, li, *, tb):
    bx = q.shape[0]
    col = pl.BlockSpec((tb, B_HEADS, B_DK, 1), lambda i: (i, 0, 0, 0))
    rowspec = pl.BlockSpec((tb, B_HEADS, 1, B_DV), lambda i: (i, 0, 0, 0))
    return pl.pallas_call(
        _gla_step_kernel,
        grid=(bx // tb,),
        in_specs=[col, col, col, rowspec, rowspec,
                  pl.BlockSpec((1, B_DV), lambda i: (0, 0)),
                  pl.BlockSpec((None, tb, B_HEADS, B_DK, B_DV), lambda i: (li, i, 0, 0, 0))],
        out_specs=[rowspec, pl.BlockSpec((tb, B_HEADS, B_DK, B_DV), lambda i: (i, 0, 0, 0))],
        out_shape=[jax.ShapeDtypeStruct((bx, B_HEADS, 1, B_DV), F32),
                   jax.ShapeDtypeStruct((bx, B_HEADS, B_DK, B_DV), F32)],
        compiler_params=_params("parallel"),
        name="gla_step",
    )(q, k, g, v, bog, out_norm.reshape(1, B_DV), state)


def _dsa_sample_score_kernel(pt_ref, iq_ref, iw_ref, ikn_ref, *refs):
    pages, o_ref = refs[:-1], refs[-1]
    iq = iq_ref[...].astype(BF16)
    iw = iw_ref[...]
    for j, pg in enumerate(pages):
        s = _dot(iq[:, :C_IDX_DIM], pg[...].astype(BF16))
        o_ref[j:j + 1, :] = jnp.sum(jnp.maximum(s, 0.0) * iw, axis=0, keepdims=True)
    s_new = jnp.sum(iq.astype(F32) * ikn_ref[...].astype(BF16).astype(F32), axis=-1, keepdims=True)
    sc_new = jnp.sum(jnp.maximum(s_new, 0.0) * iw, axis=0, keepdims=True)
    np_ = len(pages)
    o_ref[np_:, :] = jnp.broadcast_to(sc_new, (o_ref.shape[0] - np_, LANES))


def dsa_sample_scores(page_table, iq3, iw3, ik_new3, pool_ik, li, n_rows):
    bx, n_pages = page_table.shape
    page_specs = [pl.BlockSpec((None, None, C_IDX_DIM, PAGE_SIZE),
                               functools.partial(lambda b, pt, j: (li, pt[b, j], 0, 0), j=j))
                  for j in range(n_pages)]
    return pl.pallas_call(
        _dsa_sample_score_kernel,
        grid_spec=pltpu.PrefetchScalarGridSpec(
            num_scalar_prefetch=1,
            grid=(bx,),
            in_specs=[pl.BlockSpec((None, C_IDX_HEADS, LANES), lambda b, pt: (b, 0, 0)),
                      pl.BlockSpec((None, C_IDX_HEADS, 1), lambda b, pt: (b, 0, 0)),
                      pl.BlockSpec((None, 1, LANES), lambda b, pt: (b, 0, 0))] + page_specs,
            out_specs=pl.BlockSpec((None, n_rows, LANES), lambda b, pt: (b, 0, 0)),
        ),
        out_shape=jax.ShapeDtypeStruct((bx, n_rows, LANES), F32),
        compiler_params=_params("parallel"),
        name="dsa_sample_scores",
    )(page_table, iq3, iw3, ik_new3, *([pool_ik] * n_pages))


def _select_rows_kernel(sc_ref, o_ref, key_ref, *, topk, idx_bits, n_keys):
    width = sc_ref.shape[1]
    col = lax.broadcasted_iota(jnp.int32, (1, width), 1)
    key_ref[...] = jnp.where(col < n_keys, _sortable_key(sc_ref[...]), INT_MIN)

    def write(c0, val):
        o_ref[:, pl.ds(c0, SEL_CHUNK)] = val

    _threshold_select(key_ref, width // SEL_CHUNK, topk, idx_bits, n_keys - 1, write)


def dsa_select_sample(scores, n_keys):
    bx, width = scores.shape
    assert width % SEL_CHUNK == 0
    topk = min(C_TOPK_MAX, n_keys // 4)
    return pl.pallas_call(
        functools.partial(_select_rows_kernel, topk=topk, idx_bits=max(1, (width - 1).bit_length()),
                          n_keys=n_keys),
        grid=(1,),
        in_specs=[pl.BlockSpec((bx, width), lambda i: (0, 0))],
        out_specs=pl.BlockSpec((bx, width), lambda i: (0, 0)),
        out_shape=jax.ShapeDtypeStruct((bx, width), F32),
        scratch_shapes=[pltpu.VMEM((bx, width), jnp.int32)],
        compiler_params=_params("arbitrary"),
        name="dsa_select_sample",
    )(scores)


def dsa_sample_bias(rel_bias, past):
    n_pages = past // PAGE_SIZE
    bias = rel_bias[_rel_bucket(past - jnp.arange(past))][:, :C_HEADS].astype(F32)
    b0 = rel_bias[_rel_bucket(jnp.zeros((), jnp.int32))][:C_HEADS].astype(F32)[:, None]
    return bias.reshape(n_pages, PAGE_SIZE, C_HEADS).transpose(0, 2, 1), b0


def _lane_tiling_matrix():
    return jnp.asarray(np.tile(np.eye(HEAD_DIM, dtype=np.float32), (1, C_KV_HEADS)), BF16)


def _dsa_sample_attn_kernel(pt_ref, q_ref, kn_ref, vn_ref, mask_ref, bias_ref, b0_ref, e_ref, *refs):
    n_pages = (len(refs) - 1) // 2
    k_pages, v_pages, o_ref = refs[:n_pages], refs[n_pages:2 * n_pages], refs[-1]
    kvw = C_KV_HEADS * HEAD_DIM
    grp_shift = (C_HEADS // C_KV_HEADS).bit_length() - 1
    own = (jnp.right_shift(lax.broadcasted_iota(jnp.int32, (C_HEADS, kvw), 1), HEAD_DIM.bit_length() - 1)
           == jnp.right_shift(lax.broadcasted_iota(jnp.int32, (C_HEADS, kvw), 0), grp_shift))
    q = q_ref[...]
    q_bd = jnp.where(own, _dot(q.astype(BF16), e_ref[...]), 0.0).astype(BF16)
    s_new = (jnp.sum(q * kn_ref[...], axis=-1, keepdims=True) + b0_ref[...]
             + mask_ref[n_pages:n_pages + 1, 0:1])
    scores = [_dot(q_bd, k_pages[j][...].astype(BF16)) + bias_ref[j] + mask_ref[j:j + 1, :]
              for j in range(n_pages)]
    m = s_new
    for s in scores:
        m = jnp.maximum(m, jnp.max(s, axis=1, keepdims=True))
    p_new = jnp.exp(s_new - m)
    l = p_new
    acc = jnp.zeros((C_HEADS, kvw), F32)
    for j, s in enumerate(scores):
        p = jnp.exp(s - m)
        l = l + jnp.sum(p, axis=1, keepdims=True)
        acc = acc + _dot_nt(p.astype(BF16), v_pages[j][...].astype(BF16))
    acc = jnp.where(own, acc, 0.0)
    o = acc[:, :HEAD_DIM]
    for g in range(1, C_KV_HEADS):
        o = o + acc[:, g * HEAD_DIM:(g + 1) * HEAD_DIM]
    o_ref[...] = (o + p_new * vn_ref[...]) / l


def dsa_sample_attn(page_table, q3, k_new16, v_new16, mask3, bias3, b0, pool_kt, pool_vt, li):
    bx, n_pages = page_table.shape
    kvw = C_KV_HEADS * HEAD_DIM
    hspec = pl.BlockSpec((None, C_HEADS, HEAD_DIM), lambda b, pt: (b, 0, 0))
    page_specs = [pl.BlockSpec((None, None, kvw, PAGE_SIZE),
                               functools.partial(lambda b, pt, j: (li, pt[b, j], 0, 0), j=j))
                  for j in range(n_pages)]
    return pl.pallas_call(
        _dsa_sample_attn_kernel,
        grid_spec=pltpu.PrefetchScalarGridSpec(
            num_scalar_prefetch=1,
            grid=(bx,),
            in_specs=[hspec, hspec, hspec,
                      pl.BlockSpec((None,) + mask3.shape[1:], lambda b, pt: (b, 0, 0)),
                      pl.BlockSpec(bias3.shape, lambda b, pt: (0, 0, 0)),
                      pl.BlockSpec(b0.shape, lambda b, pt: (0, 0)),
                      pl.BlockSpec((HEAD_DIM, kvw), lambda b, pt: (0, 0))] + page_specs + page_specs,
            out_specs=hspec,
        ),
        out_shape=jax.ShapeDtypeStruct((bx, C_HEADS, HEAD_DIM), F32),
        compiler_params=_params("parallel"),
        name="dsa_sample_attn",
    )(page_table, q3, k_new16, v_new16, mask3, bias3, b0, _lane_tiling_matrix(),
      *([pool_kt] * n_pages), *([pool_vt] * n_pages))


def sample_forward(x, p, cache_a_k, cache_a_v, state_b, cache_c_k, cache_c_v, cache_c_ik, page_table, w,
                   rel_bias):
    bx, t_len, d = x.shape
    assert t_len == 1
    hd = A_HEADS * HEAD_DIM
    wb = cache_a_k.shape[2]
    n_pages = page_table.shape[1]
    past = n_pages * PAGE_SIZE
    n_rows = -(-(past + 1) // (2 * LANES)) * 2
    dil_bias, dil_b0 = dilated_sample_bias(rel_bias, wb)
    dsa_bias, dsa_b0 = dsa_sample_bias(rel_bias, past)
    cache_a_kt = cache_a_k.transpose(0, 1, 3, 4, 2)
    cache_a_vt = cache_a_v.transpose(0, 1, 3, 4, 2)
    pool_shape = cache_c_k.shape[:2] + (C_KV_HEADS * HEAD_DIM, PAGE_SIZE)
    pool_kt = cache_c_k.transpose(0, 1, 3, 4, 2).reshape(pool_shape)
    pool_vt = cache_c_v.transpose(0, 1, 3, 4, 2).reshape(pool_shape)
    pool_ikt = cache_c_ik.transpose(0, 1, 3, 2)
    grp = C_HEADS // C_KV_HEADS
    xf = x.reshape(bx, d)
    a_k, a_v, b_s, c_k, c_v, c_ik = [], [], [], [], [], []
    for l in range(len(w["ffn"])):
        li = l // 2
        if l % 2 == 0:
            ab = w["ab"][li]
            qk = norm_proj(xf, w["g_mix"][l], ab["w1"], ab["s1"], head_norm=True, tm=bx)
            y = norm_proj(xf, w["g_mix"][l], ab["wy"], ab["sy"], head_norm=False, tm=bx)
            v_new = y[:, :hd].reshape(bx, A_HEADS, HEAD_DIM)
            acol = lambda a: a.reshape(bx, A_HEADS, HEAD_DIM, 1)
            oa = dilated_sample(acol(qk[:, :hd]), acol(qk[:, hd:]), acol(y[:, :hd]), cache_a_kt, cache_a_vt, li,
                                dil_bias, dil_b0)
            g = gla_gate(y, ab["w2p"], w["b_gate_b"][li])
            nk = B_HEADS * B_DK
            colv = lambda a: a.reshape(bx, B_HEADS, B_DK, 1)
            rowv = lambda a: a.reshape(bx, B_HEADS, 1, B_DV)
            ob, s_fin = gla_step(colv(y[:, hd:hd + nk]), colv(y[:, hd + nk:hd + 2 * nk]), colv(g),
                                 rowv(y[:, 2 * hd:2 * hd + B_HEADS * B_DV]),
                                 rowv(y[:, 2 * hd + B_HEADS * B_DV:2 * hd + 2 * B_HEADS * B_DV]),
                                 w["b_out_norm"][li], state_b, li, tb=8)
            xf = out_proj([oa.reshape(bx, hd), ob.reshape(bx, B_HEADS * B_DV)], [ab["wo_a"], ab["wo_b"]], xf, tm=bx)
            a_k.append(qk[:, hd:].reshape(bx, 1, A_HEADS, HEAD_DIM))
            a_v.append(v_new.reshape(bx, 1, A_HEADS, HEAD_DIM))
            b_s.append(s_fin)
        else:
            c = w["c"][li]
            y1 = norm_proj(xf, w["g_mix"][l], c["w1"], c["s1"], head_norm=True, tm=bx)
            y2 = norm_proj(xf, w["g_mix"][l], c["w2"], c["s2"], head_norm=False, tm=bx)
            k0 = C_KV_COL
            k_new = y1[:, C_KV_COL:C_IK_COL].reshape(bx, C_KV_HEADS, HEAD_DIM)
            v_new = y2[:, C_KV_COL:C_IK_COL].reshape(bx, C_KV_HEADS, HEAD_DIM)
            scores = dsa_sample_scores(page_table, y2[:, :C_IDX_HEADS * LANES].reshape(bx, C_IDX_HEADS, LANES),
                                       y2[:, C_IW_COL:C_IW_COL + C_IDX_HEADS].reshape(bx, C_IDX_HEADS, 1),
                                       y1[:, C_IK_COL:C_IK_COL + LANES].reshape(bx, 1, LANES),
                                       pool_ikt, li, n_rows)
            mask = dsa_select_sample(scores.reshape(bx, n_rows * LANES), past + 1)
            oc = dsa_sample_attn(page_table, y1[:, :k0].reshape(bx, C_HEADS, HEAD_DIM),
                                 jnp.repeat(k_new, grp, axis=1), jnp.repeat(v_new, grp, axis=1),
                                 mask.reshape(bx, n_rows, LANES), dsa_bias, dsa_b0, pool_kt, pool_vt, li)
            xf = out_proj([oc.reshape(bx, k0)], [c["wo"]], xf, tm=bx)
            c_k.append(k_new.reshape(bx, 1, C_KV_HEADS, HEAD_DIM))
            c_v.append(v_new.reshape(bx, 1, C_KV_HEADS, HEAD_DIM))
            c_ik.append(y1[:, C_IK_COL:C_IK_COL + C_IDX_DIM].reshape(bx, 1, C_IDX_DIM))
        f = w["ffn"][l]
        xf = ffn_ple(xf, p.reshape(p.shape[0], bx, -1), l, f["gf"], f["w1"], f["w3"], f["w2"], f["gp"], f["wg"], f["wp"],
                     tm=bx, tf=256)
    return (xf.reshape(bx, 1, d), jnp.stack(a_k), jnp.stack(a_v), jnp.stack(b_s),
            jnp.stack(c_k), jnp.stack(c_v), jnp.stack(c_ik))


def prep_weights(rel_bias, g_mix, w_in_ab, a_q_norm, a_k_norm, b_gate_w2, b_gate_b, b_out_norm, w_out_ab,
                 w_in_c, c_q_norm, c_k_norm, c_ik_norm, w_out_c, g_ffn, w_ff1, w_ff3, w_ff2, g_ple,
                 w_ple_gate, w_ple_proj):
    w = dict(g_mix=g_mix, b_gate_b=b_gate_b, b_out_norm=b_out_norm)
    w["ab"] = [prep_ab(w_in_ab[i], a_q_norm[i], a_k_norm[i], b_gate_w2[i], w_out_ab[i])
               for i in range(w_in_ab.shape[0])]
    w["c"] = [prep_c(w_in_c[i], c_q_norm[i], c_k_norm[i], c_ik_norm[i], w_out_c[i])
              for i in range(w_in_c.shape[0])]
    w["ffn"] = [dict(gf=g_ffn[l], w1=w_ff1[l].astype(BF16), w3=w_ff3[l].astype(BF16), w2=w_ff2[l].astype(BF16),
                     gp=g_ple[l], wg=w_ple_gate[l].astype(BF16), wp=w_ple_proj[l].astype(BF16))
                for l in range(g_ffn.shape[0])]
    tables = dict(dil=dilated_bias_table(rel_bias), dsa=dsa_bias_table(rel_bias))
    return w, tables


def kernel(x_prompt, x_sample, cache_a_k, cache_a_v, state_b, cache_c_k, cache_c_v, cache_c_ik, page_table,
           p_prompt, p_sample, rel_bias, g_mix, w_in_ab, a_q_norm, a_k_norm, b_gate_w2, b_gate_b, b_out_norm,
           w_out_ab, w_in_c, c_q_norm, c_k_norm, c_ik_norm, w_out_c, g_ffn, w_ff1, w_ff3, w_ff2, g_ple,
           w_ple_gate, w_ple_proj):
    w, tables = prep_weights(rel_bias, g_mix, w_in_ab, a_q_norm, a_k_norm, b_gate_w2, b_gate_b, b_out_norm,
                             w_out_ab, w_in_c, c_q_norm, c_k_norm, c_ik_norm, w_out_c, g_ffn, w_ff1, w_ff3,
                             w_ff2, g_ple, w_ple_gate, w_ple_proj)
    prompt = prompt_forward(x_prompt, p_prompt, w, tables)
    sample = sample_forward(x_sample, p_sample, cache_a_k, cache_a_v, state_b, cache_c_k, cache_c_v,
                            cache_c_ik, page_table, w, rel_bias)
    return (prompt[0], sample[0]) + tuple(prompt[1:]) + tuple(sample[1:])
```

```python
import functools
import math

import jax
import jax.numpy as jnp
import numpy as np
from jax import lax
from jax.experimental import pallas as pl
from jax.experimental.pallas import tpu as pltpu

F32 = jnp.float32
BF16 = jnp.bfloat16

LANES = 128
HEAD_DIM = 64
A_HEADS = 8
A_BRANCHES = ((128, 1), (512, 4), (2048, 16))
A_WIN_MAX = 2048
B_HEADS = 4
B_DK = 64
B_DV = 128
B_GATE_RANK = 16
B_GATE_TAU = 16.0
C_HEADS = 16
C_KV_HEADS = 4
C_IDX_HEADS = 8
C_IDX_DIM = 64
C_TOPK_MAX = 256
N_BUCKETS = 32
REL_MAX_DIST = A_WIN_MAX
Q_BLOCK = 128
PAGE_SIZE = 128
NORM_EPS = 1e-6
NEG = -1e30
LOG2E = math.log2(math.e)
VMEM_LIMIT = 56 * 1024 * 1024


def _params(*sem):
    return pltpu.CompilerParams(dimension_semantics=sem, vmem_limit_bytes=VMEM_LIMIT)


def _rms(x, g):
    ms = jnp.mean(x * x, axis=-1, keepdims=True)
    return x * lax.rsqrt(ms + NORM_EPS) * g


def _dot(a, b):
    return jnp.dot(a, b, preferred_element_type=F32)


def _dot_nt(a, b):
    return lax.dot_general(a, b, (((1,), (1,)), ((), ())), preferred_element_type=F32)


def _dot_tn(a, b):
    return lax.dot_general(a, b, (((0,), (0,)), ((), ())), preferred_element_type=F32)


PROJ_CHUNK = 256


def _proj_kernel(x_ref, g_ref, w_ref, cs_ref, p_ref, o_ref, *, head_norm):
    hn = _rms(x_ref[...], g_ref[...]).astype(BF16)
    for c in range(o_ref.shape[1] // PROJ_CHUNK):
        sl = slice(c * PROJ_CHUNK, (c + 1) * PROJ_CHUNK)
        y = _dot(hn, w_ref[:, sl])
        if head_norm:
            ms = _dot((y * y).astype(BF16), p_ref[...])
            y = y * lax.rsqrt(ms + NORM_EPS)
        o_ref[:, sl] = y * cs_ref[:, sl]


def _group_mean_matrix():
    r = np.arange(PROJ_CHUNK) // HEAD_DIM
    return jnp.asarray((r[:, None] == r[None, :]).astype(np.float32) / HEAD_DIM, BF16)


def norm_proj(x, g, w, colscale, *, head_norm, tm):
    n, d = x.shape
    dout = w.shape[1]
    assert n % tm == 0 and dout % PROJ_CHUNK == 0
    return pl.pallas_call(
        functools.partial(_proj_kernel, head_norm=head_norm),
        grid=(n // tm,),
        in_specs=[
            pl.BlockSpec((tm, d), lambda i: (i, 0)),
            pl.BlockSpec((1, d), lambda i: (0, 0)),
            pl.BlockSpec((d, dout), lambda i: (0, 0)),
            pl.BlockSpec((1, dout), lambda i: (0, 0)),
            pl.BlockSpec((PROJ_CHUNK, PROJ_CHUNK), lambda i: (0, 0)),
        ],
        out_specs=pl.BlockSpec((tm, dout), lambda i: (i, 0)),
        out_shape=jax.ShapeDtypeStruct((n, dout), F32),
        compiler_params=_params("parallel"),
        name="norm_proj_hn" if head_norm else "norm_proj",
    )(x, g.reshape(1, d), w, colscale.reshape(1, dout), _group_mean_matrix())


def _out_kernel(*refs, n_in):
    res_ref, o_ref = refs[2 * n_in], refs[2 * n_in + 1]
    acc = res_ref[...]
    for a_ref, w_ref in zip(refs[:n_in], refs[n_in:2 * n_in]):
        acc = acc + _dot(a_ref[...].astype(BF16), w_ref[...])
    o_ref[...] = acc


def out_proj(a_list, w_list, res, *, tm):
    n, d = res.shape
    n_in = len(a_list)
    in_specs = [pl.BlockSpec((tm, a.shape[1]), lambda i: (i, 0)) for a in a_list]
    in_specs += [pl.BlockSpec(w.shape, lambda i: (0, 0)) for w in w_list]
    in_specs += [pl.BlockSpec((tm, d), lambda i: (i, 0))]
    return pl.pallas_call(
        functools.partial(_out_kernel, n_in=n_in),
        grid=(n // tm,),
        in_specs=in_specs,
        out_specs=pl.BlockSpec((tm, d), lambda i: (i, 0)),
        out_shape=jax.ShapeDtypeStruct((n, d), F32),
        compiler_params=_params("parallel"),
        name="out_proj",
    )(*a_list, *w_list, res)


def _ffn_kernel(x_ref, gf_ref, w1_ref, w3_ref, w2_ref, gp_ref, wg_ref, p_ref, wp_ref, o_ref,
                hn_ref, acc_ref):
    f = pl.program_id(1)

    @pl.when(f == 0)
    def _():
        hn_ref[...] = _rms(x_ref[...], gf_ref[...]).astype(BF16)
        acc_ref[...] = jnp.zeros_like(acc_ref)

    hn = hn_ref[...]
    h1 = _dot(hn, w1_ref[...])
    h3 = _dot(hn, w3_ref[...])
    a = h1 * jax.nn.sigmoid(h1) * h3
    acc_ref[...] += _dot(a.astype(BF16), w2_ref[...])

    @pl.when(f == pl.num_programs(1) - 1)
    def _():
        x2 = x_ref[...] + acc_ref[...]
        u = _rms(x2, gp_ref[...]).astype(BF16)
        gate = jax.nn.sigmoid(_dot(u, wg_ref[...]))
        o_ref[...] = x2 + gate * _dot(p_ref[...].astype(BF16), wp_ref[...])


FFN_ROWS = 1024


def ffn_ple(x, p, layer, gf, w1, w3, w2, gp, wg, wp, *, tm, tf):
    n, d = x.shape
    dff = w1.shape[1]
    dple = p.shape[2]
    assert n % tm == 0 and dff % tf == 0
    return pl.pallas_call(
        _ffn_kernel,
        grid=(n // tm, dff // tf),
        in_specs=[
            pl.BlockSpec((tm, d), lambda i, f: (i, 0)),
            pl.BlockSpec((1, d), lambda i, f: (0, 0)),
            pl.BlockSpec((d, tf), lambda i, f: (0, f)),
            pl.BlockSpec((d, tf), lambda i, f: (0, f)),
            pl.BlockSpec((tf, d), lambda i, f: (f, 0)),
            pl.BlockSpec((1, d), lambda i, f: (0, 0)),
            pl.BlockSpec((d, d), lambda i, f: (0, 0)),
            pl.BlockSpec((None, tm, dple), lambda i, f: (layer, i, 0)),
            pl.BlockSpec((dple, d), lambda i, f: (0, 0)),
        ],
        out_specs=pl.BlockSpec((tm, d), lambda i, f: (i, 0)),
        out_shape=jax.ShapeDtypeStruct((n, d), F32),
        scratch_shapes=[pltpu.VMEM((tm, d), BF16), pltpu.VMEM((tm, d), F32)],
        compiler_params=_params("parallel", "arbitrary"),
        name="ffn_ple",
    )(x, gf.reshape(1, d), w1, w3, w2, gp.reshape(1, d), wg, p, wp)


def _rel_bucket(dist):
    n = jnp.maximum(dist, 0)
    exact = N_BUCKETS // 2
    nf = jnp.maximum(n, exact).astype(F32)
    large = exact + (jnp.log(nf / exact) / math.log(REL_MAX_DIST / exact)
                     * (N_BUCKETS - exact)).astype(jnp.int32)
    return jnp.where(n < exact, n, jnp.minimum(large, N_BUCKETS - 1))


def _branch_multiplicity(delta):
    mult = jnp.zeros(delta.shape, jnp.int32)
    for w, d in A_BRANCHES:
        mult = mult + ((delta >= 0) & (delta <= w) & (delta % d == 0)).astype(jnp.int32)
    return mult


DIL_WIN_CHUNKS = A_WIN_MAX // Q_BLOCK + 1


def _toeplitz(value_of_delta, base, width):
    period = width + Q_BLOCK - 1
    x = jnp.concatenate([jnp.arange(width), jnp.arange(-(Q_BLOCK - 1), 0)])
    v = value_of_delta(base - x)
    t = jnp.tile(v, (1, Q_BLOCK))[:, :Q_BLOCK * (period - 1)]
    return t.reshape(v.shape[0], Q_BLOCK, period - 1)[:, :, :width]


def dilated_bias_table(rel_bias):
    def value(delta):
        mult = _branch_multiplicity(delta)
        bias = rel_bias[_rel_bucket(delta)][:, :A_HEADS].astype(F32).T
        return jnp.where(mult[None] > 0, bias + jnp.log(jnp.maximum(mult, 1).astype(F32))[None], NEG)

    t = _toeplitz(value, A_WIN_MAX, DIL_WIN_CHUNKS * Q_BLOCK)
    t = t.reshape(A_HEADS // 2, 2 * Q_BLOCK, DIL_WIN_CHUNKS * Q_BLOCK)
    return jnp.pad(t * LOG2E, ((0, 0), (0, 0), (0, Q_BLOCK)), constant_values=NEG)


DSA_BIAS_TILES = 14


def dsa_bias_table(rel_bias):
    def value(delta):
        return rel_bias[_rel_bucket(delta)][:, :C_HEADS].astype(F32).T

    last = DSA_BIAS_TILES - 1
    t = _toeplitz(value, last * Q_BLOCK, DSA_BIAS_TILES * Q_BLOCK)
    grp = C_HEADS // C_KV_HEADS
    t = t.reshape(C_KV_HEADS, grp, Q_BLOCK, DSA_BIAS_TILES, Q_BLOCK)[:, :, :, ::-1, :]
    return (t * LOG2E).transpose(0, 3, 4, 1, 2).reshape(C_KV_HEADS, DSA_BIAS_TILES, Q_BLOCK, grp * Q_BLOCK)


def _lane_lt64():
    return lax.broadcasted_iota(jnp.int32, (1, LANES), 1) < HEAD_DIM


DIL_SUB_BLOCKS = 8


def _dil_kernel(q_ref, k_ref, v_ref, t_ref, o_ref, s_ref, p_ref):
    nw = s_ref.shape[2] // Q_BLOCK
    last = DIL_WIN_CHUNKS - 1
    lo = _lane_lt64()
    for u in range(DIL_SUB_BLOCKS):
        i = pl.program_id(2) * DIL_SUB_BLOCKS + u
        qrows = slice(u * Q_BLOCK, (u + 1) * Q_BLOCK)
        q = q_ref[qrows, :] * LOG2E
        q2 = jnp.concatenate([jnp.where(lo, q, 0.0), jnp.where(lo, 0.0, q)], axis=0).astype(BF16)
        w0 = jnp.maximum(i - (nw - 1), 0)
        rows = pl.ds(pl.multiple_of(w0 * Q_BLOCK, Q_BLOCK), nw * Q_BLOCK)
        s_ref[u] = _dot_nt(q2, k_ref[rows, :].astype(BF16))
        m = jnp.full((2 * Q_BLOCK, LANES), NEG, F32)
        for w in range(nw):
            c = last - i + w0 + w
            tcol = pl.multiple_of(jnp.where(c <= last, c, last + 1) * Q_BLOCK, Q_BLOCK)
            cols = slice(w * Q_BLOCK, (w + 1) * Q_BLOCK)
            s = s_ref[u, :, cols] + t_ref[0, :, pl.ds(tcol, Q_BLOCK)]
            s_ref[u, :, cols] = s
            m = jnp.maximum(m, s)
        m = jnp.max(m, axis=1, keepdims=True)
        l = jnp.zeros((2 * Q_BLOCK, LANES), F32)
        for w in range(nw):
            cols = slice(w * Q_BLOCK, (w + 1) * Q_BLOCK)
            p = jnp.exp2(s_ref[u, :, cols] - m)
            l = l + p
            p_ref[u, :, cols] = p.astype(BF16)
        o = _dot(p_ref[u], v_ref[rows, :].astype(BF16)) / jnp.sum(l, axis=1, keepdims=True)
        o_ref[qrows, :] = jnp.where(lo, o[:Q_BLOCK], o[Q_BLOCK:])


def dilated_prompt(qk, yv, table, batch, seq):
    n = batch * seq
    nsub = DIL_SUB_BLOCKS
    assert seq % (nsub * Q_BLOCK) == 0
    nstep = seq // (nsub * Q_BLOCK)
    npair = A_HEADS // 2
    win = min(DIL_WIN_CHUNKS, seq // Q_BLOCK) * Q_BLOCK
    return pl.pallas_call(
        _dil_kernel,
        grid=(batch, npair, nstep),
        in_specs=[
            pl.BlockSpec((nsub * Q_BLOCK, LANES), lambda b, p, i: (b * nstep + i, p)),
            pl.BlockSpec((seq, LANES), lambda b, p, i: (b, npair + p)),
            pl.BlockSpec((seq, LANES), lambda b, p, i: (b, p)),
            pl.BlockSpec((1,) + table.shape[1:], lambda b, p, i: (p, 0, 0)),
        ],
        out_specs=pl.BlockSpec((nsub * Q_BLOCK, LANES), lambda b, p, i: (b * nstep + i, p)),
        out_shape=jax.ShapeDtypeStruct((n, A_HEADS * HEAD_DIM), F32),
        scratch_shapes=[pltpu.VMEM((nsub, 2 * Q_BLOCK, win), F32), pltpu.VMEM((nsub, 2 * Q_BLOCK, win), BF16)],
        compiler_params=_params("parallel", "parallel", "arbitrary"),
        name="dilated_prompt",
    )(qk, qk, yv, table)


GLA_CHUNK = 64
GLA_SUB = 16
GLA_EXP_CLAMP = 60.0


def _log_sigmoid(z):
    return jnp.minimum(z, 0.0) - jnp.log1p(jnp.exp(-jnp.abs(z)))


def _gla_kernel(bq_ref, bk_ref, bv_ref, bog_ref, blr_ref, w2_ref, gb_ref, on_ref, o_ref, s_ref, st_ref):
    it = pl.program_id(1)

    @pl.when(it == 0)
    def _():
        st_ref[...] = jnp.zeros_like(st_ref)

    lo = _lane_lt64()
    ch = GLA_CHUNK
    row = lax.broadcasted_iota(jnp.int32, (ch, ch), 0)
    col = lax.broadcasted_iota(jnp.int32, (ch, ch), 1)
    causal = row >= col
    ltri = jnp.where(causal, 1.0, 0.0).astype(BF16)
    top_half = lax.broadcasted_iota(jnp.int32, (LANES, LANES), 0) < HEAD_DIM

    def chunk_body(c, carry):
        rows = pl.ds(pl.multiple_of(c * ch, ch), ch)
        for bi, p in [(bi, p) for bi in range(bq_ref.shape[0]) for p in range(B_HEADS // 2)]:
            sl = slice(p * LANES, (p + 1) * LANES)
            z = _dot(blr_ref[bi, rows, :].astype(BF16), w2_ref[:, sl]) + gb_ref[:, sl]
            g = _log_sigmoid(z) / B_GATE_TAU
            q = bq_ref[bi, rows, sl]
            k = bk_ref[bi, rows, sl]
            g_hi = g.astype(BF16)
            g_lo = (g - g_hi.astype(F32)).astype(BF16)
            cum = _dot(ltri, g_hi) + _dot(ltri, g_lo)
            last = cum[ch - 1:ch, :]
            st = st_ref[bi, p]
            qd = q * jnp.exp(cum)
            q2 = jnp.concatenate([jnp.where(lo, qd, 0.0), jnp.where(lo, 0.0, qd)], axis=0)
            o_inter = _dot(q2.astype(BF16), st.astype(BF16))
            atts = []
            for sb in range(ch // GLA_SUB):
                rs = slice(sb * GLA_SUB, (sb + 1) * GLA_SUB)
                ref_row = cum[sb * GLA_SUB:sb * GLA_SUB + 1, :]
                qs = q[rs] * jnp.exp(cum[rs] - ref_row)
                ks = k * jnp.exp(jnp.minimum(ref_row - cum, GLA_EXP_CLAMP))
                qq = jnp.concatenate([jnp.where(lo, qs, 0.0), jnp.where(lo, 0.0, qs)], axis=0)
                atts.append(_dot_nt(qq.astype(BF16), ks.astype(BF16)))
            kd = (k * jnp.exp(last - cum)).astype(BF16)
            upd = []
            for e in range(2):
                hs = slice((2 * p + e) * LANES, (2 * p + e + 1) * LANES)
                v = bv_ref[bi, rows, hs].astype(BF16)
                att = jnp.concatenate([a[e * GLA_SUB:(e + 1) * GLA_SUB] for a in atts], axis=0)
                att = jnp.where(causal, att, 0.0)
                o = o_inter[e * ch:(e + 1) * ch] + _dot(att.astype(BF16), v)
                og = _rms(o, on_ref[...])
                gate = bog_ref[bi, rows, hs]
                o_ref[bi, rows, hs] = og * (gate * jax.nn.sigmoid(gate))
                upd.append(_dot_tn(kd, v))
            decay = jnp.transpose(jnp.broadcast_to(jnp.exp(last), (LANES, LANES)))
            st_ref[bi, p] = decay * st + jnp.where(top_half, upd[0], upd[1])
        return carry

    lax.fori_loop(0, bq_ref.shape[1] // ch, chunk_body, 0)

    @pl.when(it == pl.num_programs(1) - 1)
    def _():
        for bi in range(s_ref.shape[0]):
            for p in range(B_HEADS // 2):
                s_ref[bi, 2 * p] = st_ref[bi, p, :HEAD_DIM, :]
                s_ref[bi, 2 * p + 1] = st_ref[bi, p, HEAD_DIM:, :]


GLA_BATCH = 4


def gla_prompt(y, w2p, gate_b, out_norm, batch, seq, *, tg):
    nb = GLA_BATCH if batch % GLA_BATCH == 0 else 1
    nt = seq // tg
    dv = B_HEADS * B_DV
    y3 = y.reshape(batch, seq, y.shape[1])
    o, s_fin = pl.pallas_call(
        _gla_kernel,
        grid=(batch // nb, nt),
        in_specs=[
            pl.BlockSpec((nb, tg, 256), lambda b, t: (b, t, 2)),
            pl.BlockSpec((nb, tg, 256), lambda b, t: (b, t, 3)),
            pl.BlockSpec((nb, tg, dv), lambda b, t: (b, t, 2)),
            pl.BlockSpec((nb, tg, dv), lambda b, t: (b, t, 3)),
            pl.BlockSpec((nb, tg, LANES), lambda b, t: (b, t, 16)),
            pl.BlockSpec((LANES, 256), lambda b, t: (0, 0)),
            pl.BlockSpec((1, 256), lambda b, t: (0, 0)),
            pl.BlockSpec((1, B_DV), lambda b, t: (0, 0)),
        ],
        out_specs=[
            pl.BlockSpec((nb, tg, dv), lambda b, t: (b, t, 0)),
            pl.BlockSpec((nb, B_HEADS, B_DK, B_DV), lambda b, t: (b, 0, 0, 0)),
        ],
        out_shape=[jax.ShapeDtypeStruct((batch, seq, dv), F32),
                   jax.ShapeDtypeStruct((batch, B_HEADS, B_DK, B_DV), F32)],
        scratch_shapes=[pltpu.VMEM((nb, B_HEADS // 2, LANES, LANES), F32)],
        compiler_params=_params("parallel", "arbitrary"),
        name="gla_prompt",
    )(y3, y3, y3, y3, y3, w2p, gate_b.reshape(1, 256), out_norm.reshape(1, B_DV))
    return o.reshape(batch * seq, dv), s_fin


SEL_CHUNK = 256
INT_MIN = -2 ** 31


def _sortable_key(score):
    bits = pltpu.bitcast(score + 0.0, jnp.int32)
    return jnp.where(bits < 0, bits ^ jnp.int32(0x7FFFFFFF), bits)


def _threshold_select(key_ref, n_ch, topk, idx_bits, row_limit, write):
    r = key_ref.shape[0]
    kc = SEL_CHUNK
    lane = lax.broadcasted_iota(jnp.int32, (1, kc), 1)

    def count(hits):
        def body(c, acc):
            c0 = pl.multiple_of(c * kc, kc)
            hit = hits(key_ref[:, pl.ds(c0, kc)], c0 + lane)
            for t in range(kc // LANES):
                acc = acc + hit[:, t * LANES:(t + 1) * LANES]
            return acc
        acc = lax.fori_loop(0, n_ch, body, jnp.zeros((r, LANES), jnp.int32))
        return jnp.sum(acc, axis=1, keepdims=True)

    n_nonneg = count(lambda k, _: jnp.where(k >= 0, 1, 0))
    base = jnp.where(n_nonneg >= topk, 0, INT_MIN).astype(jnp.int32)
    n_ge = jnp.where(n_nonneg >= topk, n_nonneg, n_ch * kc)

    def bit_body(b, carry):
        base, n_ge = carry
        cand = base | jnp.left_shift(jnp.int32(1), 30 - b)
        cnt = count(lambda k, _: jnp.where(k >= cand, 1, 0))
        return jnp.where(cnt >= topk, cand, base), jnp.where(cnt >= topk, cnt, n_ge)

    tau, n_ge = lax.fori_loop(0, 31, bit_body, (base, n_ge))

    def tie_break():
        need = topk - count(lambda k, _: jnp.where(k > tau, 1, 0))

        def idx_body(b, lo):
            cand = lo | jnp.left_shift(jnp.int32(1), idx_bits - 1 - b)
            cnt = count(lambda k, col: jnp.where(k == tau, jnp.where(col < cand, 1, 0), 0))
            return jnp.where(cnt < need, cand, lo)

        return lax.fori_loop(0, idx_bits, idx_body, jnp.zeros((r, 1), jnp.int32))

    last_eq = lax.cond(jnp.max(n_ge) > topk, tie_break,
                       lambda: jnp.full((r, 1), 2 ** idx_bits, jnp.int32))

    def out_body(c, carry):
        c0 = pl.multiple_of(c * kc, kc)
        k = key_ref[:, pl.ds(c0, kc)]
        col = c0 + lane
        val = jnp.where(k > tau, 0.0, jnp.where(k == tau, jnp.where(col <= last_eq, 0.0, NEG), NEG))
        write(c0, jnp.where(col <= row_limit, val, NEG))
        return carry

    lax.fori_loop(0, n_ch, out_body, 0)


def _threshold_select_t(key_ref, n_ch, topk, idx_bits, t_idx, write):
    nq = key_ref.shape[1]
    kc = SEL_CHUNK
    rowi = lax.broadcasted_iota(jnp.int32, (kc, 1), 0)

    def count(hits):
        def body(c, acc):
            c0 = pl.multiple_of(c * kc, kc)
            hit = hits(key_ref[pl.ds(c0, kc), :], c0 + rowi)
            return acc + jnp.sum(hit.reshape(kc // 8, 8, nq), axis=0)
        acc = lax.fori_loop(0, n_ch, body, jnp.zeros((8, nq), jnp.int32))
        return jnp.sum(acc, axis=0, keepdims=True)

    n_nonneg = count(lambda k, _: jnp.where(k >= 0, 1, 0))
    base = jnp.where(n_nonneg >= topk, 0, INT_MIN).astype(jnp.int32)
    n_ge = jnp.where(n_nonneg >= topk, n_nonneg, n_ch * kc)

    def bit_body(b, carry):
        base, n_ge = carry
        cand = base | jnp.left_shift(jnp.int32(1), 30 - b)
        cnt = count(lambda k, _: jnp.where(k >= cand, 1, 0))
        return jnp.where(cnt >= topk, cand, base), jnp.where(cnt >= topk, cnt, n_ge)

    tau, n_ge = lax.fori_loop(0, 31, bit_body, (base, n_ge))

    def tie_break():
        need = topk - count(lambda k, _: jnp.where(k > tau, 1, 0))

        def idx_body(b, lo):
            cand = lo | jnp.left_shift(jnp.int32(1), idx_bits - 1 - b)
            cnt = count(lambda k, s: jnp.where(k == tau, jnp.where(s < cand, 1, 0), 0))
            return jnp.where(cnt < need, cand, lo)

        return lax.fori_loop(0, idx_bits, idx_body, jnp.zeros((1, nq), jnp.int32))

    last_eq = lax.cond(jnp.max(n_ge) > topk, tie_break,
                       lambda: jnp.full((1, nq), 2 ** idx_bits, jnp.int32))

    def out_body(c, carry):
        c0 = pl.multiple_of(c * kc, kc)
        k = key_ref[pl.ds(c0, kc), :]
        s = c0 + rowi
        val = jnp.where(k > tau, 0.0, jnp.where(k == tau, jnp.where(s <= last_eq, 0.0, NEG), NEG))
        write(c0, jnp.where(s <= t_idx, val, NEG))
        return carry

    lax.fori_loop(0, n_ch, out_body, 0)


SEL_QUERIES = 256


def _select_kernel(iq_ref, iwt_ref, ik_ref, o_ref, key_ref, qs_ref, *, topk, idx_bits):
    qb = pl.program_id(1)
    kc = SEL_CHUNK
    nq = SEL_QUERIES
    n_ch = (qb * nq + nq + kc - 1) // kc
    t_idx = qb * nq + lax.broadcasted_iota(jnp.int32, (1, nq), 1)
    rowi = lax.broadcasted_iota(jnp.int32, (kc, 1), 0)
    iwt = iwt_ref[...]
    for h in range(C_IDX_HEADS):
        qs_ref[h] = iq_ref[:, h * LANES:(h + 1) * LANES].astype(BF16)

    def score_body(c, carry):
        c0 = pl.multiple_of(c * kc, kc)
        ik = ik_ref[pl.ds(c0, kc), :].astype(BF16)
        sc = jnp.zeros((kc, nq), F32)
        for h in range(C_IDX_HEADS):
            sc = sc + jnp.maximum(_dot_nt(ik, qs_ref[h]), 0.0) * iwt[h:h + 1, :]
        key_ref[pl.ds(c0, kc), :] = jnp.where(c0 + rowi <= t_idx, _sortable_key(sc), INT_MIN)
        return carry

    lax.fori_loop(0, n_ch, score_body, 0)
    o_ref[...] = jnp.full(o_ref.shape, NEG, o_ref.dtype)

    def write(c0, val):
        o_ref[pl.ds(c0, kc), :] = val.astype(o_ref.dtype)

    _threshold_select_t(key_ref, n_ch, topk, idx_bits, t_idx, write)


def dsa_select_prompt(y2, iwt, y1, batch, seq):
    assert seq % SEL_CHUNK == 0 and seq % SEL_QUERIES == 0
    nqb = seq // SEL_QUERIES
    topk = min(C_TOPK_MAX, seq // 4)
    return pl.pallas_call(
        functools.partial(_select_kernel, topk=topk, idx_bits=max(1, (seq - 1).bit_length())),
        grid=(batch, nqb),
        in_specs=[
            pl.BlockSpec((SEL_QUERIES, C_IDX_HEADS * LANES), lambda b, i: (b * nqb + i, 0)),
            pl.BlockSpec((C_IDX_HEADS, SEL_QUERIES), lambda b, i: (0, b * nqb + i)),
            pl.BlockSpec((seq, LANES), lambda b, i: (b, 10)),
        ],
        out_specs=pl.BlockSpec((None, seq, SEL_QUERIES), lambda b, i: (b, 0, i)),
        out_shape=jax.ShapeDtypeStruct((batch, seq, seq), BF16),
        scratch_shapes=[pltpu.VMEM((seq, SEL_QUERIES), jnp.int32),
                        pltpu.VMEM((C_IDX_HEADS, SEL_QUERIES, LANES), BF16)],
        compiler_params=_params("parallel", "arbitrary"),
        name="dsa_select_prompt",
    )(y2, iwt, y1)


def _dsa_attn_kernel(q_ref, k_ref, vt_ref, mask_ref, tb_ref, o_ref, st_ref, p_ref, acc_ref):
    g = pl.program_id(1)
    i = pl.program_id(2)
    kc = DSA_KEY_CHUNK
    grp = C_HEADS // C_KV_HEADS
    lo = _lane_lt64()
    first = (g % 2) == 0
    lane_half = jnp.right_shift(lax.broadcasted_iota(jnp.int32, (1, LANES), 1), HEAD_DIM.bit_length() - 1)
    own_half = lane_half == g % 2
    nsub = q_ref.shape[0] // Q_BLOCK
    cols = grp * Q_BLOCK
    parts = []
    for u in range(nsub):
        for j in range(grp // 2):
            qp = q_ref[u * Q_BLOCK:(u + 1) * Q_BLOCK, j * LANES:(j + 1) * LANES]
            qr = pltpu.roll(qp, HEAD_DIM, 1)
            parts.append(jnp.where(own_half, jnp.where(first, qp, qr), 0.0))
            parts.append(jnp.where(own_half, jnp.where(first, qr, qp), 0.0))
    q4 = (jnp.concatenate(parts, axis=0) * LOG2E).astype(BF16)

    n_ch = (i * nsub * Q_BLOCK + nsub * Q_BLOCK + kc - 1) // kc

    acc_ref[...] = jnp.zeros_like(acc_ref)

    def body(c, carry):
        m, l = carry
        c0 = pl.multiple_of(c * kc, kc)
        st_ref[...] = _dot_nt(k_ref[pl.ds(c0, kc), :].astype(BF16), q4)
        m_out, l_out, alphas = [], [], []
        for u in range(nsub):
            for h in range(grp):
                lanes = slice((u * grp + h) * Q_BLOCK, (u * grp + h + 1) * Q_BLOCK)
                halves = []
                for hf in range(kc // Q_BLOCK):
                    e = jnp.clip(i * nsub + u - (c * (kc // Q_BLOCK) + hf), 0, DSA_BIAS_TILES - 1)
                    mk = mask_ref[pl.ds(c0 + hf * Q_BLOCK, Q_BLOCK), u * Q_BLOCK:(u + 1) * Q_BLOCK]
                    halves.append(st_ref[hf * Q_BLOCK:(hf + 1) * Q_BLOCK, lanes]
                                  + tb_ref[0, e, :, h * Q_BLOCK:(h + 1) * Q_BLOCK] + mk.astype(F32))
                s = jnp.concatenate(halves, axis=0)
                m_new = jnp.maximum(m[:, lanes], jnp.max(s, axis=0, keepdims=True))
                alpha = jnp.exp2(m[:, lanes] - m_new)
                p = jnp.exp2(s - m_new)
                p_ref[:, lanes] = p.astype(BF16)
                m_out.append(m_new)
                l_out.append(alpha * l[:, lanes] + jnp.sum(p, axis=0, keepdims=True))
                alphas.append(alpha)
        pv = _dot(vt_ref[:, pl.ds(c0, kc)].astype(BF16), p_ref[...])
        for t, alpha in enumerate(alphas):
            lanes = slice(t * Q_BLOCK, (t + 1) * Q_BLOCK)
            acc_ref[:, lanes] = alpha * acc_ref[:, lanes] + pv[:, lanes]
        return jnp.concatenate(m_out, axis=1), jnp.concatenate(l_out, axis=1)

    init = (jnp.full((1, nsub * cols), NEG, F32), jnp.zeros((1, nsub * cols), F32))
    _, l = lax.fori_loop(0, n_ch, body, init)
    ot = acc_ref[...] / l
    for u in range(nsub):
        heads = [jnp.transpose(ot[:, (u * grp + h) * Q_BLOCK:(u * grp + h + 1) * Q_BLOCK]) for h in range(grp)]
        for j in range(grp // 2):
            a, b = heads[2 * j], heads[2 * j + 1]
            left = jnp.where(first, a, pltpu.roll(a, HEAD_DIM, 1))
            right = jnp.where(first, pltpu.roll(b, HEAD_DIM, 1), b)
            o_ref[u * Q_BLOCK:(u + 1) * Q_BLOCK, j * LANES:(j + 1) * LANES] = jnp.where(lo, left, right)


DSA_ATTN_QUERIES = 512
DSA_KEY_CHUNK = 512


def dsa_attn_prompt(y1, vt, layer, mask_t, table, batch, seq):
    n = batch * seq
    nq = DSA_ATTN_QUERIES
    assert seq % nq == 0
    nblk = seq // nq
    qw = (C_HEADS // C_KV_HEADS) * HEAD_DIM
    lanes_total = (C_HEADS // C_KV_HEADS) * nq
    return pl.pallas_call(
        _dsa_attn_kernel,
        grid=(batch, C_KV_HEADS, nblk),
        in_specs=[
            pl.BlockSpec((nq, qw), lambda b, g, i: (b * nblk + i, g)),
            pl.BlockSpec((seq, LANES), lambda b, g, i: (b, 8 + g // 2)),
            pl.BlockSpec((None, None, LANES, seq), lambda b, g, i: (layer, b, g // 2, 0)),
            pl.BlockSpec((None, seq, nq), lambda b, g, i: (b, 0, i)),
            pl.BlockSpec((1,) + table.shape[1:], lambda b, g, i: (g, 0, 0, 0)),
        ],
        out_specs=pl.BlockSpec((nq, qw), lambda b, g, i: (b * nblk + i, g)),
        out_shape=jax.ShapeDtypeStruct((n, C_HEADS * HEAD_DIM), F32),
        scratch_shapes=[pltpu.VMEM((DSA_KEY_CHUNK, lanes_total), F32), pltpu.VMEM((DSA_KEY_CHUNK, lanes_total), BF16),
                        pltpu.VMEM((LANES, lanes_total), F32)],
        compiler_params=_params("parallel", "parallel", "arbitrary"),
        name="dsa_attn_prompt",
    )(y1, y1, vt, mask_t, table)


IW_SCALE = (C_IDX_HEADS ** -0.5) * (C_IDX_DIM ** -0.5)
Q_SCALE = HEAD_DIM ** -0.5
AB_Y_COLS = 2304
C_Y_COLS = 1536
C_KV_COL = C_HEADS * HEAD_DIM
C_IK_COL = C_KV_COL + C_KV_HEADS * HEAD_DIM
C_IW_COL = C_IK_COL


def _pad_heads(w, n_heads):
    d = w.shape[0]
    w = w.reshape(d, n_heads, HEAD_DIM)
    return jnp.pad(w, ((0, 0), (0, 0), (0, HEAD_DIM))).reshape(d, n_heads * LANES)


def _pad_cols(w, total):
    return jnp.pad(w, ((0, 0), (0, total - w.shape[1])))


def prep_ab(w_in, a_q_norm, a_k_norm, w2, w_out):
    hd = A_HEADS * HEAD_DIM
    w1 = w_in[:, :2 * hd]
    s1 = jnp.concatenate([jnp.tile(a_q_norm, A_HEADS) * Q_SCALE, jnp.tile(a_k_norm, A_HEADS)])
    nbq = B_HEADS * B_DK
    blr0 = 3 * hd + 2 * nbq + B_HEADS * B_DV
    wy = jnp.concatenate([w_in[:, 2 * hd:blr0], w_in[:, blr0 + B_GATE_RANK:], w_in[:, blr0:blr0 + B_GATE_RANK]],
                         axis=1)
    wy = _pad_cols(wy, AB_Y_COLS)
    sy = jnp.ones((AB_Y_COLS,), F32).at[hd:hd + nbq].set(Q_SCALE)
    w2p = jnp.pad(w2, ((0, LANES - B_GATE_RANK), (0, 0)))
    return dict(w1=w1.astype(BF16), s1=s1, wy=wy.astype(BF16), sy=sy, w2p=w2p.astype(BF16),
                wo_a=w_out[:hd].astype(BF16), wo_b=w_out[hd:].astype(BF16))


def prep_c(w_in, c_q_norm, c_k_norm, c_ik_norm, w_out):
    o = np.cumsum((0, C_HEADS * HEAD_DIM, C_KV_HEADS * HEAD_DIM, C_KV_HEADS * HEAD_DIM,
                   C_IDX_HEADS * C_IDX_DIM, C_IDX_DIM, C_IDX_HEADS))
    cq, ck, cv, iq, ik, iw = (w_in[:, o[t]:o[t + 1]] for t in range(6))
    w1 = _pad_cols(jnp.concatenate([cq, ck, ik], axis=1), C_Y_COLS)
    s1 = jnp.concatenate([jnp.tile(c_q_norm, C_HEADS) * Q_SCALE, jnp.tile(c_k_norm, C_KV_HEADS), c_ik_norm])
    s1 = jnp.pad(s1, (0, C_Y_COLS - s1.shape[0]))
    w2 = _pad_cols(jnp.concatenate([_pad_heads(iq, C_IDX_HEADS), cv, iw], axis=1), C_Y_COLS)
    s2 = jnp.ones((C_Y_COLS,), F32).at[C_IW_COL:C_IW_COL + C_IDX_HEADS].set(IW_SCALE)
    return dict(w1=w1.astype(BF16), s1=s1, w2=w2.astype(BF16), s2=s2, wo=w_out.astype(BF16))


CACHE_POS_TILE = 512


def _to_cache_kernel(*refs, n_feat):
    src_ref, o_ref = refs[0], refs[-1]
    o_ref[...] = jnp.transpose(src_ref[...])[:n_feat]


def to_cache(src, col_block, width, n_feat, batch, seq, pos0, layer, n_layers, dst=None):
    n_pos = seq - pos0
    tl = min(CACHE_POS_TILE, n_pos)
    assert n_pos % tl == 0 and pos0 % tl == 0 and seq % tl == 0
    shape = (n_layers, batch, n_feat, n_pos)
    if dst is None:
        dst = jnp.zeros(shape, F32)
    return pl.pallas_call(
        functools.partial(_to_cache_kernel, n_feat=n_feat),
        grid=(batch, n_pos // tl),
        in_specs=[pl.BlockSpec((tl, width), lambda b, i: ((b * seq + pos0) // tl + i, col_block)),
                  pl.BlockSpec(memory_space=pl.ANY)],
        out_specs=pl.BlockSpec((None, None, n_feat, tl), lambda b, i: (layer, b, 0, i)),
        out_shape=jax.ShapeDtypeStruct(shape, F32),
        input_output_aliases={1: 0},
        compiler_params=_params("parallel", "parallel"),
        name="to_cache",
    )(src, dst)


def _cache_result(buf, n_heads):
    nl, b, f, s = buf.shape
    if n_heads is None:
        return buf.transpose(0, 1, 3, 2)
    return buf.reshape(nl, b, n_heads, f // n_heads, s).transpose(0, 1, 4, 2, 3)


def prompt_forward(x, p, w, tables):
    batch, seq, d = x.shape
    n = batch * seq
    tm = FFN_ROWS if n % FFN_ROWS == 0 else min(512, n)
    wp = min(A_WIN_MAX, seq)
    hd = A_HEADS * HEAD_DIM
    n_ab, n_c = len(w["ab"]), len(w["c"])
    xf = x.reshape(n, d)
    a_k = a_v = c_k = c_v = c_ik = None
    b_s = []
    for l in range(len(w["ffn"])):
        li = l // 2
        if l % 2 == 0:
            ab = w["ab"][li]
            qk = norm_proj(xf, w["g_mix"][l], ab["w1"], ab["s1"], head_norm=True, tm=tm)
            y = norm_proj(xf, w["g_mix"][l], ab["wy"], ab["sy"], head_norm=False, tm=tm)
            oa = dilated_prompt(qk, y, tables["dil"], batch, seq)
            ob, s_fin = gla_prompt(y, ab["w2p"], w["b_gate_b"][li], w["b_out_norm"][li], batch, seq,
                                   tg=min(256, seq))
            xf = out_proj([oa, ob], [ab["wo_a"], ab["wo_b"]], xf, tm=tm)
            a_k = to_cache(qk, 1, hd, hd, batch, seq, seq - wp, li, n_ab, a_k)
            a_v = to_cache(y, 0, hd, hd, batch, seq, seq - wp, li, n_ab, a_v)
            b_s.append(s_fin)
        else:
            c = w["c"][li]
            kvw = C_KV_HEADS * HEAD_DIM
            y1 = norm_proj(xf, w["g_mix"][l], c["w1"], c["s1"], head_norm=True, tm=tm)
            y2 = norm_proj(xf, w["g_mix"][l], c["w2"], c["s2"], head_norm=False, tm=tm)
            c_v = to_cache(y2, C_KV_COL // kvw, kvw, kvw, batch, seq, 0, li, n_c, c_v)
            mask_t = dsa_select_prompt(y2, y2[:, C_IW_COL:C_IW_COL + C_IDX_HEADS].T, y1, batch, seq)
            oc = dsa_attn_prompt(y1, c_v, li, mask_t, tables["dsa"], batch, seq)
            xf = out_proj([oc], [c["wo"]], xf, tm=tm)
            c_k = to_cache(y1, C_KV_COL // kvw, kvw, kvw, batch, seq, 0, li, n_c, c_k)
            c_ik = to_cache(y1, C_IK_COL // LANES, LANES, C_IDX_DIM, batch, seq, 0, li, n_c, c_ik)
        f = w["ffn"][l]
        xf = ffn_ple(xf, p.reshape(p.shape[0], n, -1), l, f["gf"], f["w1"], f["w3"], f["w2"], f["gp"], f["wg"], f["wp"],
                     tm=tm, tf=256)
    return (xf.reshape(batch, seq, d), _cache_result(a_k, A_HEADS), _cache_result(a_v, A_HEADS), jnp.stack(b_s),
            _cache_result(c_k, C_KV_HEADS), _cache_result(c_v, C_KV_HEADS), _cache_result(c_ik, None))


def dilated_sample_bias(rel_bias, wb):
    delta = wb - jnp.arange(wb)
    mult = _branch_multiplicity(delta)
    bias = rel_bias[_rel_bucket(delta)][:, :A_HEADS].astype(F32).T
    t = jnp.where(mult[None] > 0, bias + jnp.log(jnp.maximum(mult, 1).astype(F32))[None], NEG)
    b0 = rel_bias[_rel_bucket(jnp.zeros((), jnp.int32))][:A_HEADS].astype(F32)
    return t[:, None, :], (b0 + math.log(len(A_BRANCHES)))[:, None, None]


def _dil_sample_kernel(q_ref, kn_ref, vn_ref, kt_ref, vt_ref, t_ref, b0_ref, o_ref):
    q = q_ref[...]
    s_new = jnp.sum(q * kn_ref[...], axis=1, keepdims=True) + b0_ref[...]
    s = jnp.sum(q * kt_ref[...], axis=1, keepdims=True) + t_ref[...]
    m = jnp.maximum(s_new, jnp.max(s, axis=2, keepdims=True))
    p = jnp.exp(s - m)
    p_new = jnp.exp(s_new - m)
    l = p_new + jnp.sum(p, axis=2, keepdims=True)
    acc = p_new * vn_ref[...] + jnp.sum(p * vt_ref[...], axis=2, keepdims=True)
    o_ref[...] = acc / l


def dilated_sample(q_col, kn_col, vn_col, cache_kt, cache_vt, li, table, b0):
    bx, nh, hd, _ = q_col.shape
    wb = cache_kt.shape[-1]
    col = pl.BlockSpec((None, nh, hd, 1), lambda b: (b, 0, 0, 0))
    cache = pl.BlockSpec((None, None, nh, hd, wb), lambda b: (li, b, 0, 0, 0))
    return pl.pallas_call(
        _dil_sample_kernel,
        grid=(bx,),
        in_specs=[col, col, col, cache, cache,
                  pl.BlockSpec(table.shape, lambda b: (0, 0, 0)),
                  pl.BlockSpec(b0.shape, lambda b: (0, 0, 0))],
        out_specs=col,
        out_shape=jax.ShapeDtypeStruct((bx, nh, hd, 1), F32),
        compiler_params=_params("parallel"),
        name="dilated_sample",
    )(q_col, kn_col, vn_col, cache_kt, cache_vt, table, b0)


def _gla_gate_kernel(blr_ref, w2_ref, gb_ref, o_ref):
    z = _dot(blr_ref[...].astype(BF16), w2_ref[...]) + gb_ref[...]
    o_ref[...] = _log_sigmoid(z) / B_GATE_TAU


def gla_gate(y, w2p, gate_b):
    n = y.shape[0]
    nk = B_HEADS * B_DK
    return pl.pallas_call(
        _gla_gate_kernel,
        grid=(1,),
        in_specs=[pl.BlockSpec((n, LANES), lambda i: (0, 16)),
                  pl.BlockSpec((LANES, nk), lambda i: (0, 0)),
                  pl.BlockSpec((1, nk), lambda i: (0, 0))],
        out_specs=pl.BlockSpec((n, nk), lambda i: (0, 0)),
        out_shape=jax.ShapeDtypeStruct((n, nk), F32),
        compiler_params=_params("arbitrary"),
        name="gla_gate",
    )(y, w2p, gate_b.reshape(1, nk))


def _gla_step_kernel(q_ref, k_ref, g_ref, v_ref, bog_ref, on_ref, s_ref, o_ref, sn_ref):
    st = jnp.exp(g_ref[...]) * s_ref[...] + k_ref[...] * v_ref[...]
    sn_ref[...] = st
    o = jnp.sum(q_ref[...] * st, axis=2, keepdims=True)
    gate = bog_ref[...]
    o_ref[...] = _rms(o, on_ref[...]) * (gate * jax.nn.sigmoid(gate))


def gla_step(q, k, g, v, bog, out_norm, state, li, *, tb):
    bx = q.shape[0]
    col = pl.BlockSpec((tb, B_HEADS, B_DK, 1), lambda i: (i, 0, 0, 0))
    rowspec = pl.BlockSpec((tb, B_HEADS, 1, B_DV), lambda i: (i, 0, 0, 0))
    return pl.pallas_call(
        _gla_step_kernel,
        grid=(bx // tb,),
        in_specs=[col, col, col, rowspec, rowspec,
                  pl.BlockSpec((1, B_DV), lambda i: (0, 0)),
                  pl.BlockSpec((None, tb, B_HEADS, B_DK, B_DV), lambda i: (li, i, 0, 0, 0))],
        out_specs=[rowspec, pl.BlockSpec((tb, B_HEADS, B_DK, B_DV), lambda i: (i, 0, 0, 0))],
        out_shape=[jax.ShapeDtypeStruct((bx, B_HEADS, 1, B_DV), F32),
                   jax.ShapeDtypeStruct((bx, B_HEADS, B_DK, B_DV), F32)],
        compiler_params=_params("parallel"),
        name="gla_step",
    )(q, k, g, v, bog, out_norm.reshape(1, B_DV), state)


def _dsa_sample_score_kernel(pt_ref, iq_ref, iw_ref, ikn_ref, *refs):
    pages, o_ref = refs[:-1], refs[-1]
    iq = iq_ref[...].astype(BF16)
    iw = iw_ref[...]
    for j, pg in enumerate(pages):
        s = _dot(iq[:, :C_IDX_DIM], pg[...].astype(BF16))
        o_ref[j:j + 1, :] = jnp.sum(jnp.maximum(s, 0.0) * iw, axis=0, keepdims=True)
    s_new = jnp.sum(iq.astype(F32) * ikn_ref[...].astype(BF16).astype(F32), axis=-1, keepdims=True)
    sc_new = jnp.sum(jnp.maximum(s_new, 0.0) * iw, axis=0, keepdims=True)
    np_ = len(pages)
    o_ref[np_:, :] = jnp.broadcast_to(sc_new, (o_ref.shape[0] - np_, LANES))


def dsa_sample_scores(page_table, iq3, iw3, ik_new3, pool_ik, li, n_rows):
    bx, n_pages = page_table.shape
    page_specs = [pl.BlockSpec((None, None, C_IDX_DIM, PAGE_SIZE),
                               functools.partial(lambda b, pt, j: (li, pt[b, j], 0, 0), j=j))
                  for j in range(n_pages)]
    return pl.pallas_call(
        _dsa_sample_score_kernel,
        grid_spec=pltpu.PrefetchScalarGridSpec(
            num_scalar_prefetch=1,
            grid=(bx,),
            in_specs=[pl.BlockSpec((None, C_IDX_HEADS, LANES), lambda b, pt: (b, 0, 0)),
                      pl.BlockSpec((None, C_IDX_HEADS, 1), lambda b, pt: (b, 0, 0)),
                      pl.BlockSpec((None, 1, LANES), lambda b, pt: (b, 0, 0))] + page_specs,
            out_specs=pl.BlockSpec((None, n_rows, LANES), lambda b, pt: (b, 0, 0)),
        ),
        out_shape=jax.ShapeDtypeStruct((bx, n_rows, LANES), F32),
        compiler_params=_params("parallel"),
        name="dsa_sample_scores",
    )(page_table, iq3, iw3, ik_new3, *([pool_ik] * n_pages))


def _select_rows_kernel(sc_ref, o_ref, key_ref, *, topk, idx_bits, n_keys):
    width = sc_ref.shape[1]
    col = lax.broadcasted_iota(jnp.int32, (1, width), 1)
    key_ref[...] = jnp.where(col < n_keys, _sortable_key(sc_ref[...]), INT_MIN)

    def write(c0, val):
        o_ref[:, pl.ds(c0, SEL_CHUNK)] = val

    _threshold_select(key_ref, width // SEL_CHUNK, topk, idx_bits, n_keys - 1, write)


def dsa_select_sample(scores, n_keys):
    bx, width = scores.shape
    assert width % SEL_CHUNK == 0
    topk = min(C_TOPK_MAX, n_keys // 4)
    return pl.pallas_call(
        functools.partial(_select_rows_kernel, topk=topk, idx_bits=max(1, (width - 1).bit_length()),
                          n_keys=n_keys),
        grid=(1,),
        in_specs=[pl.BlockSpec((bx, width), lambda i: (0, 0))],
        out_specs=pl.BlockSpec((bx, width), lambda i: (0, 0)),
        out_shape=jax.ShapeDtypeStruct((bx, width), F32),
        scratch_shapes=[pltpu.VMEM((bx, width), jnp.int32)],
        compiler_params=_params("arbitrary"),
        name="dsa_select_sample",
    )(scores)


def dsa_sample_bias(rel_bias, past):
    n_pages = past // PAGE_SIZE
    bias = rel_bias[_rel_bucket(past - jnp.arange(past))][:, :C_HEADS].astype(F32)
    b0 = rel_bias[_rel_bucket(jnp.zeros((), jnp.int32))][:C_HEADS].astype(F32)[:, None]
    return bias.reshape(n_pages, PAGE_SIZE, C_HEADS).transpose(0, 2, 1), b0


def _lane_tiling_matrix():
    return jnp.asarray(np.tile(np.eye(HEAD_DIM, dtype=np.float32), (1, C_KV_HEADS)), BF16)


def _dsa_sample_attn_kernel(pt_ref, q_ref, kn_ref, vn_ref, mask_ref, bias_ref, b0_ref, e_ref, *refs):
    n_pages = (len(refs) - 1) // 2
    k_pages, v_pages, o_ref = refs[:n_pages], refs[n_pages:2 * n_pages], refs[-1]
    kvw = C_KV_HEADS * HEAD_DIM
    grp_shift = (C_HEADS // C_KV_HEADS).bit_length() - 1
    own = (jnp.right_shift(lax.broadcasted_iota(jnp.int32, (C_HEADS, kvw), 1), HEAD_DIM.bit_length() - 1)
           == jnp.right_shift(lax.broadcasted_iota(jnp.int32, (C_HEADS, kvw), 0), grp_shift))
    q = q_ref[...]
    q_bd = jnp.where(own, _dot(q.astype(BF16), e_ref[...]), 0.0).astype(BF16)
    s_new = (jnp.sum(q * kn_ref[...], axis=-1, keepdims=True) + b0_ref[...]
             + mask_ref[n_pages:n_pages + 1, 0:1])
    scores = [_dot(q_bd, k_pages[j][...].astype(BF16)) + bias_ref[j] + mask_ref[j:j + 1, :]
              for j in range(n_pages)]
    m = s_new
    for s in scores:
        m = jnp.maximum(m, jnp.max(s, axis=1, keepdims=True))
    p_new = jnp.exp(s_new - m)
    l = p_new
    acc = jnp.zeros((C_HEADS, kvw), F32)
    for j, s in enumerate(scores):
        p = jnp.exp(s - m)
        l = l + jnp.sum(p, axis=1, keepdims=True)
        acc = acc + _dot_nt(p.astype(BF16), v_pages[j][...].astype(BF16))
    acc = jnp.where(own, acc, 0.0)
    o = acc[:, :HEAD_DIM]
    for g in range(1, C_KV_HEADS):
        o = o + acc[:, g * HEAD_DIM:(g + 1) * HEAD_DIM]
    o_ref[...] = (o + p_new * vn_ref[...]) / l


def dsa_sample_attn(page_table, q3, k_new16, v_new16, mask3, bias3, b0, pool_kt, pool_vt, li):
    bx, n_pages = page_table.shape
    kvw = C_KV_HEADS * HEAD_DIM
    hspec = pl.BlockSpec((None, C_HEADS, HEAD_DIM), lambda b, pt: (b, 0, 0))
    page_specs = [pl.BlockSpec((None, None, kvw, PAGE_SIZE),
                               functools.partial(lambda b, pt, j: (li, pt[b, j], 0, 0), j=j))
                  for j in range(n_pages)]
    return pl.pallas_call(
        _dsa_sample_attn_kernel,
        grid_spec=pltpu.PrefetchScalarGridSpec(
            num_scalar_prefetch=1,
            grid=(bx,),
            in_specs=[hspec, hspec, hspec,
                      pl.BlockSpec((None,) + mask3.shape[1:], lambda b, pt: (b, 0, 0)),
                      pl.BlockSpec(bias3.shape, lambda b, pt: (0, 0, 0)),
                      pl.BlockSpec(b0.shape, lambda b, pt: (0, 0)),
                      pl.BlockSpec((HEAD_DIM, kvw), lambda b, pt: (0, 0))] + page_specs + page_specs,
            out_specs=hspec,
        ),
        out_shape=jax.ShapeDtypeStruct((bx, C_HEADS, HEAD_DIM), F32),
        compiler_params=_params("parallel"),
        name="dsa_sample_attn",
    )(page_table, q3, k_new16, v_new16, mask3, bias3, b0, _lane_tiling_matrix(),
      *([pool_kt] * n_pages), *([pool_vt] * n_pages))


def sample_forward(x, p, cache_a_k, cache_a_v, state_b, cache_c_k, cache_c_v, cache_c_ik, page_table, w,
                   rel_bias):
    bx, t_len, d = x.shape
    assert t_len == 1
    hd = A_HEADS * HEAD_DIM
    wb = cache_a_k.shape[2]
    n_pages = page_table.shape[1]
    past = n_pages * PAGE_SIZE
    n_rows = -(-(past + 1) // (2 * LANES)) * 2
    dil_bias, dil_b0 = dilated_sample_bias(rel_bias, wb)
    dsa_bias, dsa_b0 = dsa_sample_bias(rel_bias, past)
    cache_a_kt = cache_a_k.transpose(0, 1, 3, 4, 2)
    cache_a_vt = cache_a_v.transpose(0, 1, 3, 4, 2)
    pool_shape = cache_c_k.shape[:2] + (C_KV_HEADS * HEAD_DIM, PAGE_SIZE)
    pool_kt = cache_c_k.transpose(0, 1, 3, 4, 2).reshape(pool_shape)
    pool_vt = cache_c_v.transpose(0, 1, 3, 4, 2).reshape(pool_shape)
    pool_ikt = cache_c_ik.transpose(0, 1, 3, 2)
    grp = C_HEADS // C_KV_HEADS
    xf = x.reshape(bx, d)
    a_k, a_v, b_s, c_k, c_v, c_ik = [], [], [], [], [], []
    for l in range(len(w["ffn"])):
        li = l // 2
        if l % 2 == 0:
            ab = w["ab"][li]
            qk = norm_proj(xf, w["g_mix"][l], ab["w1"], ab["s1"], head_norm=True, tm=bx)
            y = norm_proj(xf, w["g_mix"][l], ab["wy"], ab["sy"], head_norm=False, tm=bx)
            v_new = y[:, :hd].reshape(bx, A_HEADS, HEAD_DIM)
            acol = lambda a: a.reshape(bx, A_HEADS, HEAD_DIM, 1)
            oa = dilated_sample(acol(qk[:, :hd]), acol(qk[:, hd:]), acol(y[:, :hd]), cache_a_kt, cache_a_vt, li,
                                dil_bias, dil_b0)
            g = gla_gate(y, ab["w2p"], w["b_gate_b"][li])
            nk = B_HEADS * B_DK
            colv = lambda a: a.reshape(bx, B_HEADS, B_DK, 1)
            rowv = lambda a: a.reshape(bx, B_HEADS, 1, B_DV)
            ob, s_fin = gla_step(colv(y[:, hd:hd + nk]), colv(y[:, hd + nk:hd + 2 * nk]), colv(g),
                                 rowv(y[:, 2 * hd:2 * hd + B_HEADS * B_DV]),
                                 rowv(y[:, 2 * hd + B_HEADS * B_DV:2 * hd + 2 * B_HEADS * B_DV]),
                                 w["b_out_norm"][li], state_b, li, tb=8)
            xf = out_proj([oa.reshape(bx, hd), ob.reshape(bx, B_HEADS * B_DV)], [ab["wo_a"], ab["wo_b"]], xf, tm=bx)
            a_k.append(qk[:, hd:].reshape(bx, 1, A_HEADS, HEAD_DIM))
            a_v.append(v_new.reshape(bx, 1, A_HEADS, HEAD_DIM))
            b_s.append(s_fin)
        else:
            c = w["c"][li]
            y1 = norm_proj(xf, w["g_mix"][l], c["w1"], c["s1"], head_norm=True, tm=bx)
            y2 = norm_proj(xf, w["g_mix"][l], c["w2"], c["s2"], head_norm=False, tm=bx)
            k0 = C_KV_COL
            k_new = y1[:, C_KV_COL:C_IK_COL].reshape(bx, C_KV_HEADS, HEAD_DIM)
            v_new = y2[:, C_KV_COL:C_IK_COL].reshape(bx, C_KV_HEADS, HEAD_DIM)
            scores = dsa_sample_scores(page_table, y2[:, :C_IDX_HEADS * LANES].reshape(bx, C_IDX_HEADS, LANES),
                                       y2[:, C_IW_COL:C_IW_COL + C_IDX_HEADS].reshape(bx, C_IDX_HEADS, 1),
                                       y1[:, C_IK_COL:C_IK_COL + LANES].reshape(bx, 1, LANES),
                                       pool_ikt, li, n_rows)
            mask = dsa_select_sample(scores.reshape(bx, n_rows * LANES), past + 1)
            oc = dsa_sample_attn(page_table, y1[:, :k0].reshape(bx, C_HEADS, HEAD_DIM),
                                 jnp.repeat(k_new, grp, axis=1), jnp.repeat(v_new, grp, axis=1),
                                 mask.reshape(bx, n_rows, LANES), dsa_bias, dsa_b0, pool_kt, pool_vt, li)
            xf = out_proj([oc.reshape(bx, k0)], [c["wo"]], xf, tm=bx)
            c_k.append(k_new.reshape(bx, 1, C_KV_HEADS, HEAD_DIM))
            c_v.append(v_new.reshape(bx, 1, C_KV_HEADS, HEAD_DIM))
            c_ik.append(y1[:, C_IK_COL:C_IK_COL + C_IDX_DIM].reshape(bx, 1, C_IDX_DIM))
        f = w["ffn"][l]
        xf = ffn_ple(xf, p.reshape(p.shape[0], bx, -1), l, f["gf"], f["w1"], f["w3"], f["w2"], f["gp"], f["wg"], f["wp"],
                     tm=bx, tf=256)
    return (xf.reshape(bx, 1, d), jnp.stack(a_k), jnp.stack(a_v), jnp.stack(b_s),
            jnp.stack(c_k), jnp.stack(c_v), jnp.stack(c_ik))


def prep_weights(rel_bias, g_mix, w_in_ab, a_q_norm, a_k_norm, b_gate_w2, b_gate_b, b_out_norm, w_out_ab,
                 w_in_c, c_q_norm, c_k_norm, c_ik_norm, w_out_c, g_ffn, w_ff1, w_ff3, w_ff2, g_ple,
                 w_ple_gate, w_ple_proj):
    w = dict(g_mix=g_mix, b_gate_b=b_gate_b, b_out_norm=b_out_norm)
    w["ab"] = [prep_ab(w_in_ab[i], a_q_norm[i], a_k_norm[i], b_gate_w2[i], w_out_ab[i])
               for i in range(w_in_ab.shape[0])]
    w["c"] = [prep_c(w_in_c[i], c_q_norm[i], c_k_norm[i], c_ik_norm[i], w_out_c[i])
              for i in range(w_in_c.shape[0])]
    w["ffn"] = [dict(gf=g_ffn[l], w1=w_ff1[l].astype(BF16), w3=w_ff3[l].astype(BF16), w2=w_ff2[l].astype(BF16),
                     gp=g_ple[l], wg=w_ple_gate[l].astype(BF16), wp=w_ple_proj[l].astype(BF16))
                for l in range(g_ffn.shape[0])]
    tables = dict(dil=dilated_bias_table(rel_bias), dsa=dsa_bias_table(rel_bias))
    return w, tables


def kernel(x_prompt, x_sample, cache_a_k, cache_a_v, state_b, cache_c_k, cache_c_v, cache_c_ik, page_table,
           p_prompt, p_sample, rel_bias, g_mix, w_in_ab, a_q_norm, a_k_norm, b_gate_w2, b_gate_b, b_out_norm,
           w_out_ab, w_in_c, c_q_norm, c_k_norm, c_ik_norm, w_out_c, g_ffn, w_ff1, w_ff3, w_ff2, g_ple,
           w_ple_gate, w_ple_proj):
    w, tables = prep_weights(rel_bias, g_mix, w_in_ab, a_q_norm, a_k_norm, b_gate_w2, b_gate_b, b_out_norm,
                             w_out_ab, w_in_c, c_q_norm, c_k_norm, c_ik_norm, w_out_c, g_ffn, w_ff1, w_ff3,
                             w_ff2, g_ple, w_ple_gate, w_ple_proj)
    prompt = prompt_forward(x_prompt, p_prompt, w, tables)
    sample = sample_forward(x_sample, p_sample, cache_a_k, cache_a_v, state_b, cache_c_k, cache_c_v,
                            cache_c_ik, page_table, w, rel_bias)
    return (prompt[0], sample[0]) + tuple(prompt[1:]) + tuple(sample[1:])
```
